```python
import math
import jax
import jax.numpy as jnp
from jax import lax
import numpy as np

D_MODEL = 2048
BATCH = 2
SEQ = 8192
DEPTH = 4

GRID_W = 64
CTX_LEN = 256
HEAD_DIM = 64
ROPE_THETA = 10000.0
Q_BLOCK = 128
NORM_EPS = 1e-6
N_GROUPS = 4
GROUP_WIDTH = D_MODEL // N_GROUPS
MIX_WIDTH = N_GROUPS * GROUP_WIDTH
GQA_Q_HEADS = GROUP_WIDTH // HEAD_DIM
GQA_KV_HEADS = max(1, GQA_Q_HEADS // 4)
GQA_REP = GQA_Q_HEADS // GQA_KV_HEADS
DIFF_V_DIM = 2 * HEAD_DIM
DIFF_HEADS = GROUP_WIDTH // DIFF_V_DIM
RET_V_DIM = 2 * HEAD_DIM
RET_HEADS = GROUP_WIDTH // RET_V_DIM
RET_CHUNK = 128
HY_WIDTH = GROUP_WIDTH
HY_ORDER = 2
HY_EMB_DIM = 33
HY_BANDS = (HY_EMB_DIM - 1) // 2
HY_FILT_HIDDEN = 64
HY_MIN_DECAY = math.log(1e-2) / 1.5
HY_MAX_DECAY = math.log(1e-2) / 0.3
SHORT_CONV_W = 3
D_FF_DENSE = 5632
N_EXPERTS = 8
TOP_K = 2
D_FF_EXPERT = 2 * D_MODEL
N_DENSE = (DEPTH + 1) // 2
N_MOE = DEPTH // 2

GQA_SIZES = (GQA_Q_HEADS * HEAD_DIM, GQA_KV_HEADS * HEAD_DIM, GQA_KV_HEADS * HEAD_DIM)
DIFF_SIZES = (DIFF_HEADS * 2 * HEAD_DIM, DIFF_HEADS * 2 * HEAD_DIM, DIFF_HEADS * DIFF_V_DIM)
RET_SIZES = (RET_HEADS * HEAD_DIM, RET_HEADS * HEAD_DIM, RET_HEADS * RET_V_DIM, RET_HEADS * RET_V_DIM)
HY_SIZES = (HY_WIDTH, HY_WIDTH, HY_WIDTH)
GROUP_IN_SIZES = (sum(GQA_SIZES), sum(DIFF_SIZES), sum(RET_SIZES), sum(HY_SIZES))
IN_WIDTH = sum(GROUP_IN_SIZES)

kernel_name = 'hybrid_parallel_heads_dit_trunk'


def split_cols(p, sizes):
    idx = [int(i) for i in np.cumsum(sizes)[:-1]]
    return jnp.split(p, idx, axis=-1)


def rmsnorm(x, g):
    xf = x.astype(jnp.float32)
    y = xf * lax.rsqrt(jnp.mean(xf * xf, axis=-1, keepdims=True) + NORM_EPS)
    return (y * g.astype(jnp.float32)).astype(x.dtype)


def layernorm_heads(o, g):
    mu = jnp.mean(o, axis=-1, keepdims=True)
    var = jnp.mean(jnp.square(o - mu), axis=-1, keepdims=True)
    return (o - mu) * lax.rsqrt(var + NORM_EPS) * g.astype(jnp.float32).reshape(o.shape[-2], o.shape[-1])


def modulate(h, shift, scale):
    return h * (1 + scale) + shift


def axial_rope_tables(n_tokens):
    rows = n_tokens // GRID_W
    row = jnp.repeat(jnp.arange(rows, dtype=jnp.float32), GRID_W)
    col = jnp.tile(jnp.arange(GRID_W, dtype=jnp.float32), rows)
    n_freq = HEAD_DIM // 4
    freqs = ROPE_THETA ** (-jnp.arange(n_freq, dtype=jnp.float32) / n_freq)
    ar = row[:, None] * freqs
    ac = col[:, None] * freqs
    cos = jnp.concatenate([jnp.cos(ar), jnp.cos(ar), jnp.cos(ac), jnp.cos(ac)], axis=-1)
    sin = jnp.concatenate([jnp.sin(ar), jnp.sin(ar), jnp.sin(ac), jnp.sin(ac)], axis=-1)
    return cos, sin


def apply_rope(x, cos, sin):
    shape = x.shape
    xr = x.reshape(shape[:-1] + (2, 2, HEAD_DIM // 4))
    rot = jnp.stack([-xr[..., 1, :], xr[..., 0, :]], axis=-2).reshape(shape)
    bshape = (cos.shape[0],) + (1,) * (x.ndim - 3) + (HEAD_DIM,)
    return (x * cos.reshape(bshape) + rot * sin.reshape(bshape)).astype(x.dtype)


def blocked_attention(q, k, v):
    B, T, G, R, d = q.shape
    nb = T // Q_BLOCK
    scale = d ** -0.5
    qb = jnp.moveaxis(q.reshape(B, nb, Q_BLOCK, G, R, d), 1, 0)

    def one_block(qblk):
        s = jnp.einsum('bqgrd,bkgd->bgrqk', qblk, k).astype(jnp.float32) * scale
        p = jax.nn.softmax(s, axis=-1).astype(v.dtype)
        return jnp.einsum('bgrqk,bkgv->bqgrv', p, v)

    out = lax.map(one_block, qb)
    return jnp.moveaxis(out, 0, 1).reshape(B, T, G, R, v.shape[-1])


def gqa_mixer(pc, pl, qk_g, cos, sin, with_ctx):
    def heads(p):
        B, T = p.shape[:2]
        q, k, v = split_cols(p, GQA_SIZES)
        q = rmsnorm(q.reshape(B, T, GQA_KV_HEADS, GQA_REP, HEAD_DIM), qk_g[0])
        k = rmsnorm(k.reshape(B, T, GQA_KV_HEADS, HEAD_DIM), qk_g[1])
        return q, k, v.reshape(B, T, GQA_KV_HEADS, HEAD_DIM)

    qc, kc, vc = heads(pc)
    ql, kl, vl = heads(pl)
    ql = apply_rope(ql, cos, sin)
    kl = apply_rope(kl, cos, sin)
    B, S = pl.shape[:2]
    out_l = blocked_attention(ql, jnp.concatenate([kc, kl], axis=1), jnp.concatenate([vc, vl], axis=1))
    out_l = out_l.reshape(B, S, GROUP_WIDTH).astype(pl.dtype)
    out_c = blocked_attention(qc, kc, vc).reshape(B, pc.shape[1], GROUP_WIDTH).astype(pc.dtype) if with_ctx else None
    return out_l, out_c


def diff_mixer(pc, pl, lam, subln_g, lambda_init, cos, sin, with_ctx):
    def heads(p):
        B, T = p.shape[:2]
        q, k, v = split_cols(p, DIFF_SIZES)
        return (q.reshape(B, T, DIFF_HEADS, 2, HEAD_DIM), k.reshape(B, T, DIFF_HEADS, 2, HEAD_DIM),
                v.reshape(B, T, DIFF_HEADS, DIFF_V_DIM))

    lamf = lam.astype(jnp.float32)
    lambda_full = jnp.exp(jnp.sum(lamf[0] * lamf[1])) - jnp.exp(jnp.sum(lamf[2] * lamf[3])) + lambda_init

    def diff_attend(q, k, v):
        B, T = q.shape[:2]
        a1 = blocked_attention(q[:, :, :, 0:1], k[:, :, :, 0], v)
        a2 = blocked_attention(q[:, :, :, 1:2], k[:, :, :, 1], v)
        o = (a1.astype(jnp.float32) - lambda_full * a2.astype(jnp.float32))[:, :, :, 0]
        o = rmsnorm(o, subln_g) * (1.0 - lambda_init)
        return o.reshape(B, T, GROUP_WIDTH).astype(v.dtype)

    qc, kc, vc = heads(pc)
    ql, kl, vl = heads(pl)
    ql = apply_rope(ql, cos, sin)
    kl = apply_rope(kl, cos, sin)
    out_l = diff_attend(ql, jnp.concatenate([kc, kl], axis=1), jnp.concatenate([vc, vl], axis=1))
    out_c = diff_attend(qc, kc, vc) if with_ctx else None
    return out_l, out_c


def ret_chunkwise(q, k, v, log_gamma, s0):
    B, L, H, dk = q.shape
    dv = v.shape[-1]
    nc = L // RET_CHUNK
    qc = q.reshape(B, nc, RET_CHUNK, H, dk)
    kc = k.reshape(B, nc, RET_CHUNK, H, dk)
    vc = v.reshape(B, nc, RET_CHUNK, H, dv)
    idx = jnp.arange(RET_CHUNK, dtype=jnp.float32)
    rel = idx[:, None] - idx[None, :]
    decay = jnp.where(rel[None] >= 0, jnp.exp(jnp.maximum(rel, 0.0)[None] * log_gamma[:, None, None]), 0.0)
    scores = jnp.einsum('bnqhk,bnshk->bnhqs', qc, kc) * decay
    inner = jnp.einsum('bnhqs,bnshv->bnqhv', scores, vc)
    zeta = jnp.exp((RET_CHUNK - 1 - idx)[:, None] * log_gamma[None])
    upd = jnp.einsum('bnshk,sh,bnshv->nbhkv', kc, zeta, vc)
    chunk_decay = jnp.exp(RET_CHUNK * log_gamma)[:, None, None]

    def step(s, u):
        return chunk_decay * s + u, s

    _, s_prev = lax.scan(step, s0, upd)
    xi = jnp.exp((idx + 1)[:, None] * log_gamma[None])
    cross = jnp.einsum('bnqhk,nbhkv,qh->bnqhv', qc, s_prev, xi)
    return (inner + cross).reshape(B, L, H, dv)


def ret_final_state(k, v, log_gamma):
    L = k.shape[1]
    expo = (L - 1 - jnp.arange(L, dtype=jnp.float32))[:, None] * log_gamma[None]
    return jnp.einsum('bmhk,mh,bmhv->bhkv', k, jnp.exp(expo), v)


def ret_mixer(pc, pl, log_decay, gn_g, cos, sin, with_ctx):
    def heads(p):
        B, T = p.shape[:2]
        q, k, v, g = split_cols(p, RET_SIZES)
        q = q.astype(jnp.float32).reshape(B, T, RET_HEADS, HEAD_DIM)
        k = k.astype(jnp.float32).reshape(B, T, RET_HEADS, HEAD_DIM) * (HEAD_DIM ** -0.5)
        v = v.astype(jnp.float32).reshape(B, T, RET_HEADS, RET_V_DIM)
        return q, k, v, g

    def flip(a):
        return a[:, ::-1]

    def gate_out(o, g):
        B, T = o.shape[:2]
        y = jax.nn.silu(g.astype(jnp.float32)) * layernorm_heads(o, gn_g).reshape(B, T, GROUP_WIDTH)
        return y.astype(g.dtype)

    lg = log_decay.astype(jnp.float32)
    lg_f, lg_b = lg[0], lg[1]
    qc, kc, vc, gc = heads(pc)
    ql, kl, vl, gl = heads(pl)
    ql = apply_rope(ql, cos, sin)
    kl = apply_rope(kl, cos, sin)
    s_f = ret_final_state(kc, vc, lg_f)
    s_b = ret_final_state(flip(kc), flip(vc), lg_b)
    o_l = ret_chunkwise(ql, kl, vl, lg_f, s_f) + flip(ret_chunkwise(flip(ql), flip(kl), flip(vl), lg_b, s_b))
    out_l = gate_out(o_l, gl)
    out_c = None
    if with_ctx:
        zero = jnp.zeros((pc.shape[0], RET_HEADS, HEAD_DIM, RET_V_DIM), jnp.float32)
        o_c = ret_chunkwise(qc, kc, vc, lg_f, zero) + flip(ret_chunkwise(flip(qc), flip(kc), flip(vc), lg_b, zero))
        out_c = gate_out(o_c, gc)
    return out_l, out_c


def hyena_filters(L, w1, b1, w2, b2, w3):
    t = jnp.arange(L, dtype=jnp.float32)
    t_norm = t / L
    f = jnp.linspace(1e-4, HY_BANDS - 1, HY_BANDS, dtype=jnp.float32)
    w = 2.0 * math.pi * t_norm
    feats = jnp.concatenate([t_norm[:, None], jnp.cos(w[:, None] * f), -jnp.sin(w[:, None] * f)], axis=-1)
    h = jnp.sin(feats @ w1.astype(jnp.float32) + b1.astype(jnp.float32))
    h = jnp.sin(h @ w2.astype(jnp.float32) + b2.astype(jnp.float32))
    h = (h @ w3.astype(jnp.float32)).reshape(L, HY_ORDER, 2, HY_WIDTH)
    deltas = jnp.abs(jnp.linspace(HY_MIN_DECAY, HY_MAX_DECAY, HY_WIDTH, dtype=jnp.float32))
    window = jnp.exp(-t_norm[:, None] * deltas[None])
    h = h * window[:, None, None, :]
    h_fwd, h_bwd = h[:, :, 0], h[:, :, 1]
    k_full = jnp.concatenate([h_fwd, jnp.zeros((1, HY_ORDER, HY_WIDTH), jnp.float32), h_bwd[1:][::-1]], axis=0)
    k_full = k_full / (jnp.sum(jnp.abs(k_full), axis=0, keepdims=True) + NORM_EPS)
    return jnp.fft.rfft(k_full, axis=0)


def short_conv(u, w, b):
    up = jnp.pad(u, ((0, 0), (1, 1), (0, 0)))
    return up[:, :-2] * w[0] + up[:, 1:-1] * w[1] + up[:, 2:] * w[2] + b


def hyena_seq(p, sw, sb, fw1, fb1, fw2, fb2, fw3, hbias):
    L = p.shape[1]
    v, x1, x2 = split_cols(short_conv(p, sw, sb), HY_SIZES)
    kf = hyena_filters(L, fw1, fb1, fw2, fb2, fw3)
    z = v.astype(jnp.float32)
    for o, gate in enumerate((x1, x2)):
        zf = jnp.fft.rfft(z, n=2 * L, axis=1)
        conv = jnp.fft.irfft(zf * kf[None, :, o], n=2 * L, axis=1)[:, :L]
        z = gate.astype(jnp.float32) * (conv + hbias[o].astype(jnp.float32) * z)
    return z.astype(p.dtype)


def token_mixer(hc, hl, cos, sin, lambda_init, with_ctx, w_in, w_out, beta, qk_g, lam, subln_g,
                log_decay, gn_g, sw, sb, fw1, fb1, fw2, fb2, fw3, hbias):
    pc = hc @ w_in
    pl = hl @ w_in
    ac, bc, rc, dc = split_cols(pc, GROUP_IN_SIZES)
    al, bl, rl, dl = split_cols(pl, GROUP_IN_SIZES)
    a_l, a_c = gqa_mixer(ac, al, qk_g, cos, sin, with_ctx)
    b_l, b_c = diff_mixer(bc, bl, lam, subln_g, lambda_init, cos, sin, with_ctx)
    r_l, r_c = ret_mixer(rc, rl, log_decay, gn_g, cos, sin, with_ctx)
    d_l = hyena_seq(dl, sw, sb, fw1, fb1, fw2, fb2, fw3, hbias)
    yl = (jnp.concatenate([a_l, b_l, r_l, d_l], axis=-1) * beta) @ w_out
    yc = None
    if with_ctx:
        d_c = hyena_seq(dc, sw, sb, fw1, fb1, fw2, fb2, fw3, hbias)
        yc = (jnp.concatenate([a_c, b_c, r_c, d_c], axis=-1) * beta) @ w_out
    return yl, yc


def swiglu(h, wg, wu, wd):
    return (jax.nn.silu(h @ wg) * (h @ wu)) @ wd


def moe_swiglu(h, router, wg, wu, wd):
    logits = jnp.einsum('btd,de->bte', h, router).astype(jnp.float32)
    top_v, top_i = lax.top_k(logits, TOP_K)
    top_w = jax.nn.softmax(top_v, axis=-1)
    combine = jnp.sum(jax.nn.one_hot(top_i, N_EXPERTS, dtype=jnp.float32) * top_w[..., None], axis=-2)
    out = jnp.zeros_like(h)
    for e in range(N_EXPERTS):
        out = out + combine[..., e:e + 1].astype(h.dtype) * swiglu(h, wg[e], wu[e], wd[e])
    return out


def setup_inputs(seed: int = 0) -> dict:
    key = jax.random.key(seed)
    ks = iter(jax.random.split(key, 40))

    def nrm(shape, scale):
        return scale * jax.random.normal(next(ks), shape, jnp.float32)

    base_decay = jnp.log(1.0 - 2.0 ** (-5.0 - jnp.arange(RET_HEADS, dtype=jnp.float32)))
    return {
        'x': nrm((BATCH, SEQ, D_MODEL), 1.0),
        'c': nrm((BATCH, D_MODEL), 1.0),
        'ctx': nrm((BATCH, CTX_LEN, D_MODEL), 1.0),
        'c_ctx': nrm((D_MODEL,), 1.0),
        'ada_w': nrm((DEPTH, D_MODEL, 6 * D_MODEL), 0.5 * D_MODEL ** -0.5),
        'ada_b': nrm((DEPTH, 6 * D_MODEL), 0.02),
        'norm_g': 1.0 + nrm((DEPTH, 4, D_MODEL), 0.05),
        'w_in': nrm((DEPTH, D_MODEL, IN_WIDTH), D_MODEL ** -0.5),
        'w_out': nrm((DEPTH, MIX_WIDTH, D_MODEL), MIX_WIDTH ** -0.5),
        'mix_beta': 1.0 + nrm((DEPTH, MIX_WIDTH), 0.05),
        'gqa_qk_g': 1.0 + nrm((DEPTH, 2, HEAD_DIM), 0.05),
        'diff_lambda': nrm((DEPTH, 4, HEAD_DIM), 0.1),
        'diff_subln_g': 1.0 + nrm((DEPTH, DIFF_V_DIM), 0.05),
        'ret_log_decay': base_decay[None, None, :] * jnp.exp(nrm((DEPTH, 2, RET_HEADS), 0.1)),
        'ret_gn_g': 1.0 + nrm((DEPTH, GROUP_WIDTH), 0.05),
        'hy_short_w': nrm((DEPTH, SHORT_CONV_W, 3 * HY_WIDTH), SHORT_CONV_W ** -0.5),
        'hy_short_b': nrm((DEPTH, 3 * HY_WIDTH), 0.02),
        'hy_filt_w1': nrm((DEPTH, HY_EMB_DIM, HY_FILT_HIDDEN), HY_EMB_DIM ** -0.5),
        'hy_filt_b1': nrm((DEPTH, HY_FILT_HIDDEN), 0.02),
        'hy_filt_w2': nrm((DEPTH, HY_FILT_HIDDEN, HY_FILT_HIDDEN), HY_FILT_HIDDEN ** -0.5),
        'hy_filt_b2': nrm((DEPTH, HY_FILT_HIDDEN), 0.02),
        'hy_filt_w3': nrm((DEPTH, HY_FILT_HIDDEN, HY_ORDER * 2 * HY_WIDTH), HY_FILT_HIDDEN ** -0.5),
        'hy_bias': nrm((DEPTH, HY_ORDER, HY_WIDTH), 0.5),
        'ffn_w_gate': nrm((N_DENSE, D_MODEL, D_FF_DENSE), D_MODEL ** -0.5),
        'ffn_w_up': nrm((N_DENSE, D_MODEL, D_FF_DENSE), D_MODEL ** -0.5),
        'ffn_w_down': nrm((N_DENSE, D_FF_DENSE, D_MODEL), D_FF_DENSE ** -0.5),
        'moe_router': nrm((N_MOE, D_MODEL, N_EXPERTS), D_MODEL ** -0.5),
        'moe_w_gate': nrm((N_MOE, N_EXPERTS, D_MODEL, D_FF_EXPERT), D_MODEL ** -0.5),
        'moe_w_up': nrm((N_MOE, N_EXPERTS, D_MODEL, D_FF_EXPERT), D_MODEL ** -0.5),
        'moe_w_down': nrm((N_MOE, N_EXPERTS, D_FF_EXPERT, D_MODEL), D_FF_EXPERT ** -0.5),
    }


def reference(x, c, ctx, c_ctx, ada_w, ada_b, norm_g, w_in, w_out, mix_beta, gqa_qk_g, diff_lambda,
              diff_subln_g, ret_log_decay, ret_gn_g, hy_short_w, hy_short_b, hy_filt_w1, hy_filt_b1,
              hy_filt_w2, hy_filt_b2, hy_filt_w3, hy_bias, ffn_w_gate, ffn_w_up, ffn_w_down,
              moe_router, moe_w_gate, moe_w_up, moe_w_down):
    S = x.shape[1]
    C = ctx.shape[1]
    cos, sin = axial_rope_tables(S)
    xl, xc = x, ctx
    for l in range(DEPTH):
        with_ctx = l < DEPTH - 1
        lambda_init = 0.8 - 0.6 * math.exp(-0.3 * l)
        mod_l = jax.nn.silu(c) @ ada_w[l] + ada_b[l]
        mod_c = jax.nn.silu(c_ctx) @ ada_w[l] + ada_b[l]
        sh_m, sc_m, gt_m, sh_f, sc_f, gt_f = jnp.split(mod_l[:, None, :], 6, axis=-1)
        csh_m, csc_m, cgt_m, csh_f, csc_f, cgt_f = jnp.split(mod_c, 6, axis=-1)
        g_pre_m, g_post_m, g_pre_f, g_post_f = norm_g[l, 0], norm_g[l, 1], norm_g[l, 2], norm_g[l, 3]

        hl = modulate(rmsnorm(xl, g_pre_m), sh_m, sc_m)
        hc = modulate(rmsnorm(xc, g_pre_m), csh_m, csc_m)
        yl, yc = token_mixer(hc, hl, cos, sin, lambda_init, with_ctx, w_in[l], w_out[l], mix_beta[l],
                             gqa_qk_g[l], diff_lambda[l], diff_subln_g[l], ret_log_decay[l], ret_gn_g[l],
                             hy_short_w[l], hy_short_b[l], hy_filt_w1[l], hy_filt_b1[l], hy_filt_w2[l],
                             hy_filt_b2[l], hy_filt_w3[l], hy_bias[l])
        xl = xl + gt_m * rmsnorm(yl, g_post_m)
        if with_ctx:
            xc = xc + cgt_m * rmsnorm(yc, g_post_m)

        fl = modulate(rmsnorm(xl, g_pre_f), sh_f, sc_f)
        if with_ctx:
            fc = modulate(rmsnorm(xc, g_pre_f), csh_f, csc_f)
            f_in = jnp.concatenate([fc, fl], axis=1)
        else:
            f_in = fl
        if l % 2 == 0:
            i = l // 2
            f_out = swiglu(f_in, ffn_w_gate[i], ffn_w_up[i], ffn_w_down[i])
        else:
            i = l // 2
            f_out = moe_swiglu(f_in, moe_router[i], moe_w_gate[i], moe_w_up[i], moe_w_down[i])
        xl = xl + gt_f * rmsnorm(f_out[:, f_out.shape[1] - S:], g_post_f)
        if with_ctx:
            xc = xc + cgt_f * rmsnorm(f_out[:, :C], g_post_f)
    return xl
```

```python
import functools
import math

import numpy as np
import jax
import jax.numpy as jnp
from jax import lax
from jax.experimental import pallas as pl
from jax.experimental.pallas import tpu as pltpu

F32 = jnp.float32
BF16 = jnp.bfloat16

D_MODEL = 2048
DEPTH = 4
GRID_W = 64
HEAD_DIM = 64
ROPE_THETA = 10000.0
NORM_EPS = 1e-6
GROUP_WIDTH = 512
GQA_Q_HEADS = 8
GQA_KV_HEADS = 2
GQA_REP = 4
DIFF_HEADS = 4
RET_HEADS = 4
HY_WIDTH = 512
HY_ORDER = 2
HY_EMB_DIM = 33
HY_BANDS = 16
HY_FILT_HIDDEN = 64
HY_MIN_DECAY = math.log(1e-2) / 1.5
HY_MAX_DECAY = math.log(1e-2) / 0.3
N_EXPERTS = 8

COL_ATTN = 3840
COL_DIFF = 768
COL_RET = 2304
COL_RET_GATE = 3328
COL_HY = 3840
IN_WIDTH = 5376

ROW_TILE = 256
FFT_N2 = 128
VMEM_LIMIT_MB = 48


def _cparams(sem, vmem_mb=VMEM_LIMIT_MB):
    return pltpu.CompilerParams(dimension_semantics=sem, vmem_limit_bytes=vmem_mb * 1024 * 1024)


def _pcall(kern, **kw):
    fn = kern.func if isinstance(kern, functools.partial) else kern
    return pl.pallas_call(kern, name=fn.__name__.strip("_"), **kw)


def _mm_kernel(a_ref, b_ref, o_ref):
    o_ref[...] = jnp.dot(a_ref[...].astype(BF16), b_ref[...].astype(BF16),
                         preferred_element_type=F32).astype(o_ref.dtype)


def _b_spec(b, lead, k, tn, col_off):
    nlead = len(lead)
    return pl.BlockSpec((None,) * nlead + (k, tn), lambda j, i: tuple(lead) + (0, j + col_off))


def mm(a, b, *, tm, tn, out_dtype, lead=(), col_off=0, n_cols=None):
    m, k = a.shape
    n = b.shape[-1] if n_cols is None else n_cols
    assert m % tm == 0 and n % tn == 0, (m, tm, n, tn)
    return _pcall(
        _mm_kernel,
        grid=(n // tn, m // tm),
        in_specs=[pl.BlockSpec((tm, k), lambda j, i: (i, 0)), _b_spec(b, lead, k, tn, col_off)],
        out_specs=pl.BlockSpec((tm, tn), lambda j, i: (i, j)),
        out_shape=jax.ShapeDtypeStruct((m, n), out_dtype),
        compiler_params=_cparams(("parallel", "parallel")),
    )(a, b)


def _swiglu_up_kernel(a_ref, wg_ref, wu_ref, o_ref):
    a = a_ref[...]
    g = jnp.dot(a, wg_ref[...].astype(BF16), preferred_element_type=F32)
    u = jnp.dot(a, wu_ref[...].astype(BF16), preferred_element_type=F32)
    o_ref[...] = (g * jax.nn.sigmoid(g) * u).astype(o_ref.dtype)


def swiglu_up(a, wg, wu, *, tm, tn, lead):
    m, k = a.shape
    n = wg.shape[-1]
    assert m % tm == 0 and n % tn == 0
    return _pcall(
        _swiglu_up_kernel,
        grid=(n // tn, m // tm),
        in_specs=[pl.BlockSpec((tm, k), lambda j, i: (i, 0)),
                  _b_spec(wg, lead, k, tn, 0), _b_spec(wu, lead, k, tn, 0)],
        out_specs=pl.BlockSpec((tm, tn), lambda j, i: (i, j)),
        out_shape=jax.ShapeDtypeStruct((m, n), BF16),
        compiler_params=_cparams(("parallel", "parallel")),
    )(a, wg, wu)


def _mm_acc_kernel(a_ref, b_ref, w_ref, prev_ref, o_ref):
    y = jnp.dot(a_ref[...], b_ref[...].astype(BF16), preferred_element_type=F32)
    o_ref[...] = prev_ref[...] + w_ref[...] * y


def mm_acc(a, b, wcol, prev, *, tm, tn, lead):
    m, k = a.shape
    n = b.shape[-1]
    return _pcall(
        _mm_acc_kernel,
        grid=(n // tn, m // tm),
        in_specs=[pl.BlockSpec((tm, k), lambda j, i: (i, 0)), _b_spec(b, lead, k, tn, 0),
                  pl.BlockSpec((tm, 1), lambda j, i: (i, 0)),
                  pl.BlockSpec((tm, tn), lambda j, i: (i, j))],
        out_specs=pl.BlockSpec((tm, tn), lambda j, i: (i, j)),
        out_shape=jax.ShapeDtypeStruct((m, n), F32),
        compiler_params=_cparams(("parallel", "parallel")),
    )(a, b, wcol, prev)


def _seg_map(n_t, n_c):
    return lambda i: (2 * (i // n_t) + jnp.where(i % n_t >= n_c, 1, 0), 0, 0)


def _rms(x):
    return x * lax.rsqrt(jnp.mean(x * x, axis=-1, keepdims=True) + NORM_EPS)


def _norm_mod_kernel(x_ref, g_ref, sh_ref, sc_ref, *o_refs):
    y = _rms(x_ref[...]) * g_ref[...]
    y = y * (1.0 + sc_ref[...]) + sh_ref[...]
    for o_ref in o_refs:
        o_ref[...] = y.astype(o_ref.dtype)


def norm_mod(x, g, shift, scale, n_t, n_c, out_dtypes):
    r, d = x.shape
    seg = _seg_map(n_t, n_c)
    row = pl.BlockSpec((ROW_TILE, d), lambda i: (i, 0))
    outs = _pcall(
        _norm_mod_kernel,
        grid=(r // ROW_TILE,),
        in_specs=[row, pl.BlockSpec((1, d), lambda i: (0, 0)),
                  pl.BlockSpec((None, 1, d), seg), pl.BlockSpec((None, 1, d), seg)],
        out_specs=[row] * len(out_dtypes),
        out_shape=[jax.ShapeDtypeStruct((r, d), dt) for dt in out_dtypes],
        compiler_params=_cparams(("parallel",)),
    )(x, g.reshape(1, d), shift, scale)
    return outs


def _gate_res_kernel(x_ref, y_ref, g_ref, gt_ref, o_ref):
    o_ref[...] = x_ref[...] + gt_ref[...] * (_rms(y_ref[...]) * g_ref[...])


def gate_res(x, y, g, gate, n_t, n_c):
    r, d = x.shape
    row = pl.BlockSpec((ROW_TILE, d), lambda i: (i, 0))
    return _pcall(
        _gate_res_kernel,
        grid=(r // ROW_TILE,),
        in_specs=[row, row, pl.BlockSpec((1, d), lambda i: (0, 0)),
                  pl.BlockSpec((None, 1, d), _seg_map(n_t, n_c))],
        out_specs=row,
        out_shape=jax.ShapeDtypeStruct((r, d), F32),
        compiler_params=_cparams(("parallel",)),
    )(x, y, g.reshape(1, d), gate)


def _router_kernel(f_ref, rt_ref, o_ref):
    lt = lax.dot_general(rt_ref[...], f_ref[...], (((1,), (1,)), ((), ())),
                         precision=lax.Precision.HIGHEST, preferred_element_type=F32)
    e_idx = lax.broadcasted_iota(jnp.int32, lt.shape, 0)
    m1 = jnp.max(lt, axis=0, keepdims=True)
    i1 = jnp.min(jnp.where(lt == m1, e_idx, N_EXPERTS), axis=0, keepdims=True)
    rest = jnp.where(e_idx == i1, -jnp.inf, lt)
    m2 = jnp.max(rest, axis=0, keepdims=True)
    i2 = jnp.min(jnp.where(rest == m2, e_idx, N_EXPERTS), axis=0, keepdims=True)
    e2 = jnp.exp(m2 - m1)
    w1 = 1.0 / (1.0 + e2)
    w2 = e2 / (1.0 + e2)
    o_ref[...] = jnp.where(e_idx == i1, w1, 0.0) + jnp.where(e_idx == i2, w2, 0.0)


def router(f, router_t):
    r, d = f.shape
    return _pcall(
        _router_kernel,
        grid=(r // ROW_TILE,),
        in_specs=[pl.BlockSpec((ROW_TILE, d), lambda i: (i, 0)),
                  pl.BlockSpec((N_EXPERTS, d), lambda i: (0, 0))],
        out_specs=pl.BlockSpec((N_EXPERTS, ROW_TILE), lambda i: (0, i)),
        out_shape=jax.ShapeDtypeStruct((N_EXPERTS, r), F32),
        compiler_params=_cparams(("parallel",)),
    )(f, router_t)


def _rope(x, c, sa, sb):
    return x * c + pltpu.roll(x, 112, 1) * sa + pltpu.roll(x, 16, 1) * sb


def _head_rms(x, g, bd):
    sq = x * x
    hi = sq.astype(BF16)
    lo = (sq - hi.astype(F32)).astype(BF16)
    ms = jnp.dot(hi, bd, preferred_element_type=F32) + jnp.dot(lo, bd, preferred_element_type=F32)
    return x * lax.rsqrt(ms + NORM_EPS) * g


def _prep_kernel(p_ref, c_ref, sa_ref, sb_ref, gq_ref, gk_ref, bd_ref,
                 aq_ref, ak_ref, av_ref, dq_ref, dk_ref, dv_ref, rq_ref, rk_ref, rv_ref):
    c, sa, sb = c_ref[...], sa_ref[...], sb_ref[...]
    bd = bd_ref[...]
    scale = HEAD_DIM ** -0.5

    def chunk(j):
        return p_ref[:, 128 * j:128 * (j + 1)]

    def put_heads(ref, first, val):
        ref[first] = val[:, :HEAD_DIM].astype(ref.dtype)
        ref[first + 1] = val[:, HEAD_DIM:].astype(ref.dtype)

    for j in range(4):
        put_heads(aq_ref, 2 * j, _rope(_head_rms(chunk(j), gq_ref[...], bd), c, sa, sb) * scale)
    put_heads(ak_ref, 0, _rope(_head_rms(chunk(4), gk_ref[...], bd), c, sa, sb))
    put_heads(av_ref, 0, chunk(5))
    for j in range(4):
        put_heads(dq_ref, 2 * j, _rope(chunk(6 + j), c, sa, sb) * scale)
        put_heads(dk_ref, 2 * j, _rope(chunk(10 + j), c, sa, sb))
        dv_ref[j] = chunk(14 + j).astype(dv_ref.dtype)
    for j in range(2):
        put_heads(rq_ref, 2 * j, _rope(chunk(18 + j), c, sa, sb))
        put_heads(rk_ref, 2 * j, _rope(chunk(20 + j), c, sa, sb) * scale)
    for j in range(4):
        rv_ref[j] = chunk(22 + j).astype(rv_ref.dtype)


def prep_heads(p_attn, rope_c, rope_sa, rope_sb, qk_g, bsz, n_t):
    t = n_t * ROW_TILE
    bd = jnp.asarray(np.kron(np.eye(2), np.full((HEAD_DIM, HEAD_DIM), 1.0 / HEAD_DIM)), F32).astype(BF16)
    gq = jnp.tile(qk_g[0], 2).reshape(1, 128)
    gk = jnp.tile(qk_g[1], 2).reshape(1, 128)
    tab = pl.BlockSpec((ROW_TILE, 128), lambda i: (i % n_t, 0))
    vec = pl.BlockSpec((1, 128), lambda i: (0, 0))

    def heads(nh, dh):
        return (pl.BlockSpec((None, nh, ROW_TILE, dh), lambda i: (i // n_t, 0, i % n_t, 0)),
                jax.ShapeDtypeStruct((bsz, nh, t, dh), BF16))

    outs = [heads(8, 64), heads(2, 64), heads(2, 64), heads(8, 64), heads(8, 64), heads(4, 128),
            heads(4, 64), heads(4, 64), heads(4, 128)]
    return _pcall(
        _prep_kernel,
        grid=(bsz * n_t,),
        in_specs=[pl.BlockSpec((ROW_TILE, COL_RET_GATE), lambda i: (i, 0)), tab, tab, tab, vec, vec,
                  pl.BlockSpec((128, 128), lambda i: (0, 0))],
        out_specs=[o[0] for o in outs],
        out_shape=[o[1] for o in outs],
        compiler_params=_cparams(("parallel",)),
    )(p_attn, rope_c, rope_sa, rope_sb, gq, gk, bd)


def _nt_dot(a, b):
    return lax.dot_general(a, b, (((1,), (1,)), ((), ())), preferred_element_type=F32)


def _softmax_chunks(score_fn, v_ref, rows, dv, n_chunks, tk, n_ctx_keys, is_ctx):
    def step(kk, carry, mask_keys):
        m, l, acc = carry
        start = pl.multiple_of(kk * tk, tk)
        s = score_fn(start, tk)
        if mask_keys:
            col = lax.broadcasted_iota(jnp.int32, s.shape, 1)
            s = jnp.where(col < n_ctx_keys, s, -1e30)
        m_new = jnp.maximum(m, jnp.max(s, axis=-1, keepdims=True))
        alpha = jnp.exp(m - m_new)
        p = jnp.exp(s - m_new)
        l = alpha * l + jnp.sum(p, axis=-1, keepdims=True)
        acc = alpha * acc + jnp.dot(p.astype(BF16), v_ref[pl.ds(start, tk), :], preferred_element_type=F32)
        return m_new, l, acc

    init = (jnp.full((rows, 1), -1e30, F32), jnp.zeros((rows, 1), F32), jnp.zeros((rows, dv), F32))
    _, l, acc = lax.cond(is_ctx,
                         lambda: step(0, init, True),
                         lambda: lax.fori_loop(0, n_chunks, lambda kk, c: step(kk, c, False), init))
    return acc / l


def _gqa_kernel(q_ref, k_ref, v_ref, beta_ref, o_ref, *, tq, tk, n_chunks, n_ctx_tiles, n_ctx_keys):
    is_ctx = pl.program_id(2) < n_ctx_tiles
    q = q_ref[...].reshape(GQA_REP * tq, HEAD_DIM)

    def scores(start, size):
        return _nt_dot(q, k_ref[pl.ds(start, size), :])

    o = _softmax_chunks(scores, v_ref, GQA_REP * tq, HEAD_DIM, n_chunks, tk, n_ctx_keys, is_ctx)
    for r in range(GQA_REP):
        sl = slice(HEAD_DIM * r, HEAD_DIM * (r + 1))
        o_ref[:, sl] = (o[r * tq:(r + 1) * tq] * beta_ref[:, sl]).astype(o_ref.dtype)


def gqa_attention(q, k, v, beta, *, n_t, n_c, tk):
    bsz, _, t, _ = q.shape
    tq = ROW_TILE
    kern = functools.partial(_gqa_kernel, tq=tq, tk=tk, n_chunks=t // tk, n_ctx_tiles=n_c,
                             n_ctx_keys=n_c * ROW_TILE)
    kv = pl.BlockSpec((None, None, t, HEAD_DIM), lambda b, g, i: (b, g, 0, 0))
    w = GQA_REP * HEAD_DIM
    return _pcall(
        kern,
        grid=(bsz, GQA_KV_HEADS, n_t),
        in_specs=[pl.BlockSpec((None, GQA_REP, tq, HEAD_DIM), lambda b, g, i: (b, g, i, 0)), kv, kv,
                  pl.BlockSpec((1, w), lambda b, g, i: (0, g))],
        out_specs=pl.BlockSpec((tq, w), lambda b, g, i: (b * n_t + i, g)),
        out_shape=jax.ShapeDtypeStruct((bsz * t, GROUP_WIDTH), BF16),
        compiler_params=_cparams(("parallel", "parallel", "arbitrary")),
    )(q, k, v, beta)


def _diff_kernel(q_ref, k_ref, v_ref, lam_ref, g_ref, beta_ref, o_ref, *, tq, tk, n_chunks, n_ctx_tiles,
                 n_ctx_keys, out_scale):
    is_ctx = pl.program_id(2) < n_ctx_tiles
    q0, q1 = q_ref[0], q_ref[1]

    def scores(start, size):
        return jnp.concatenate([_nt_dot(q0, k_ref[0, pl.ds(start, size), :]),
                                _nt_dot(q1, k_ref[1, pl.ds(start, size), :])], axis=0)

    a = _softmax_chunks(scores, v_ref, 2 * tq, 2 * HEAD_DIM, n_chunks, tk, n_ctx_keys, is_ctx)
    o = a[:tq] - lam_ref[...] * a[tq:]
    o = _rms(o) * g_ref[...] * out_scale
    o_ref[...] = (o * beta_ref[...]).astype(o_ref.dtype)


def diff_attention(q, k, v, lam, subln_g, beta, out_scale, *, n_t, n_c, tk):
    bsz, _, t, _ = q.shape
    tq = ROW_TILE
    dv = 2 * HEAD_DIM
    kern = functools.partial(_diff_kernel, tq=tq, tk=tk, n_chunks=t // tk, n_ctx_tiles=n_c,
                             n_ctx_keys=n_c * ROW_TILE, out_scale=out_scale)
    return _pcall(
        kern,
        grid=(bsz, DIFF_HEADS, n_t),
        in_specs=[pl.BlockSpec((None, 2, tq, HEAD_DIM), lambda b, h, i: (b, h, i, 0)),
                  pl.BlockSpec((None, 2, t, HEAD_DIM), lambda b, h, i: (b, h, 0, 0)),
                  pl.BlockSpec((None, None, t, dv), lambda b, h, i: (b, h, 0, 0)),
                  pl.BlockSpec((1, 1), lambda b, h, i: (0, 0)),
                  pl.BlockSpec((1, dv), lambda b, h, i: (0, 0)),
                  pl.BlockSpec((1, dv), lambda b, h, i: (0, h))],
        out_specs=pl.BlockSpec((tq, dv), lambda b, h, i: (b * n_t + i, h)),
        out_shape=jax.ShapeDtypeStruct((bsz * t, GROUP_WIDTH), BF16),
        compiler_params=_cparams(("parallel", "parallel", "arbitrary")),
    )(q, k, v, lam, subln_g, beta)


def _ret_kernel(lg_ref, q_ref, k_ref, v_ref, o_ref, state_ref):
    d = pl.program_id(2)
    c = ROW_TILE

    @pl.when(pl.program_id(3) == 0)
    def _():
        state_ref[...] = jnp.zeros_like(state_ref)

    lg = lg_ref[...]
    fwd = d == 0
    q, k, v = q_ref[...], k_ref[...], v_ref[...]
    ii = lax.broadcasted_iota(jnp.int32, (c, c), 0)
    jj = lax.broadcasted_iota(jnp.int32, (c, c), 1)
    rel = jnp.where(fwd, ii - jj, jj - ii).astype(F32)
    decay = jnp.where(rel >= 0, jnp.exp(jnp.maximum(rel, 0.0) * lg), 0.0)
    pos = lax.broadcasted_iota(jnp.int32, (c, 1), 0).astype(F32)
    xi = jnp.exp(jnp.where(fwd, pos + 1.0, c - pos) * lg)
    zeta = jnp.exp(jnp.where(fwd, c - 1.0 - pos, pos) * lg)
    state = state_ref[...]
    scores = _nt_dot(q, k) * decay
    inner = jnp.dot(scores.astype(BF16), v, preferred_element_type=F32)
    cross = jnp.dot(q, state.astype(BF16), preferred_element_type=F32) * xi
    o_ref[...] = inner + cross
    kz = (k.astype(F32) * zeta).astype(BF16)
    upd = lax.dot_general(kz, v, (((0,), (0,)), ((), ())), preferred_element_type=F32)
    state_ref[...] = jnp.exp(c * lg) * state + upd


def retention(q, k, v, log_decay, *, n_t, n_c):
    bsz, nh, t, dk = q.shape
    dv = v.shape[-1]

    def blk(d, j):
        back = jnp.where(j < n_c, n_c - 1 - j, n_t - 1 - (j - n_c))
        return jnp.where(d == 0, j, back)

    qk = pl.BlockSpec((None, None, ROW_TILE, dk), lambda b, h, d, j: (b, h, blk(d, j), 0))
    return _pcall(
        _ret_kernel,
        grid=(bsz, nh, 2, n_t),
        in_specs=[pl.BlockSpec((None, 1, 1), lambda b, h, d, j: (d * RET_HEADS + h, 0, 0)), qk, qk,
                  pl.BlockSpec((None, None, ROW_TILE, dv), lambda b, h, d, j: (b, h, blk(d, j), 0))],
        out_specs=pl.BlockSpec((None, ROW_TILE, dv), lambda b, h, d, j: (d, b * n_t + blk(d, j), h)),
        out_shape=jax.ShapeDtypeStruct((2, bsz * t, nh * dv), F32),
        scratch_shapes=[pltpu.VMEM((dk, dv), F32)],
        compiler_params=_cparams(("parallel", "parallel", "arbitrary", "arbitrary")),
    )(log_decay.reshape(2 * RET_HEADS, 1, 1), q, k, v)


def _ret_gate_kernel(of_ref, ob_ref, g_ref, gn_ref, beta_ref, o_ref):
    o = of_ref[...] + ob_ref[...]
    mu = jnp.mean(o, axis=-1, keepdims=True)
    var = jnp.mean(jnp.square(o - mu), axis=-1, keepdims=True)
    y = (o - mu) * lax.rsqrt(var + NORM_EPS) * gn_ref[...]
    g = g_ref[...]
    o_ref[...] = (g * jax.nn.sigmoid(g) * y * beta_ref[...]).astype(o_ref.dtype)


def ret_gate(o2, p_attn, gn_g, beta):
    _, r, w = o2.shape
    dv = 2 * HEAD_DIM
    gate_col0 = COL_RET_GATE // dv
    vec = pl.BlockSpec((1, dv), lambda i, h: (0, h))
    return _pcall(
        _ret_gate_kernel,
        grid=(r // ROW_TILE, RET_HEADS),
        in_specs=[pl.BlockSpec((None, ROW_TILE, dv), lambda i, h: (0, i, h)),
                  pl.BlockSpec((None, ROW_TILE, dv), lambda i, h: (1, i, h)),
                  pl.BlockSpec((ROW_TILE, dv), lambda i, h: (i, gate_col0 + h)), vec, vec],
        out_specs=pl.BlockSpec((ROW_TILE, dv), lambda i, h: (i, h)),
        out_shape=jax.ShapeDtypeStruct((r, w), BF16),
        compiler_params=_cparams(("parallel", "parallel")),
    )(o2, o2, p_attn, gn_g, beta)


def _short_conv_kernel(cur_ref, prev_ref, next_ref, w_ref, b_ref, z_ref, x1_ref, x2_ref, *, n_seq_tiles):
    j = pl.program_id(1)

    @pl.when(j < n_seq_tiles)
    def _():
        u = cur_ref[...]
        rows = lax.broadcasted_iota(jnp.int32, u.shape, 0)
        prev_row = jnp.where(j == 0, 0.0, prev_ref[7:8, :])
        next_row = jnp.where(j == n_seq_tiles - 1, 0.0, next_ref[0:1, :])
        up = jnp.where(rows == 0, prev_row, pltpu.roll(u, 1, 0))
        un = jnp.where(rows == ROW_TILE - 1, next_row, pltpu.roll(u, ROW_TILE - 1, 0))
        y = up * w_ref[0:1, :] + u * w_ref[1:2, :] + un * w_ref[2:3, :] + b_ref[...]
        z_ref[...] = y[:, :HY_WIDTH]
        x1_ref[...] = y[:, HY_WIDTH:2 * HY_WIDTH]
        x2_ref[...] = y[:, 2 * HY_WIDTH:]

    @pl.when(j >= n_seq_tiles)
    def _():
        z_ref[...] = jnp.zeros_like(z_ref)
        x1_ref[...] = jnp.zeros_like(x1_ref)
        x2_ref[...] = jnp.zeros_like(x2_ref)


def short_conv(p_hy, w, b, *, bsz, n_t, first_tile, n_seq_tiles, n_pad_tiles):
    w3 = 3 * HY_WIDTH
    sub = ROW_TILE // 8
    last_blk8 = p_hy.shape[0] // 8 - 1

    def cur(bb, j):
        return (bb * n_t + first_tile + jnp.minimum(j, n_seq_tiles - 1), 0)

    def prev(bb, j):
        return (jnp.maximum(cur(bb, j)[0] * sub - 1, 0), 0)

    def nxt(bb, j):
        return (jnp.minimum((cur(bb, j)[0] + 1) * sub, last_blk8), 0)

    kern = functools.partial(_short_conv_kernel, n_seq_tiles=n_seq_tiles)
    rows = bsz * n_pad_tiles * ROW_TILE
    return _pcall(
        kern,
        grid=(bsz, n_pad_tiles),
        in_specs=[pl.BlockSpec((ROW_TILE, w3), cur), pl.BlockSpec((8, w3), prev), pl.BlockSpec((8, w3), nxt),
                  pl.BlockSpec((3, w3), lambda bb, j: (0, 0)), pl.BlockSpec((1, w3), lambda bb, j: (0, 0))],
        out_specs=[pl.BlockSpec((ROW_TILE, HY_WIDTH), lambda bb, j: (bb * n_pad_tiles + j, 0))] * 3,
        out_shape=[jax.ShapeDtypeStruct((rows, HY_WIDTH), F32)] * 3,
        compiler_params=_cparams(("parallel", "arbitrary")),
    )(p_hy, p_hy, p_hy, w, b.reshape(1, w3))


def _filter_kernel(feat_ref, w1_ref, b1_ref, w2_ref, b2_ref, w3_ref, win_ref, h_ref, asum_ref):
    i = pl.program_id(0)
    hp = lax.Precision.HIGHEST
    h = jnp.sin(jnp.dot(feat_ref[...], w1_ref[...], precision=hp, preferred_element_type=F32) + b1_ref[...])
    h = jnp.sin(jnp.dot(h, w2_ref[...], precision=hp, preferred_element_type=F32) + b2_ref[...])
    h = jnp.dot(h, w3_ref[...], precision=hp, preferred_element_type=F32)
    win = win_ref[...]
    h = h * jnp.concatenate([win] * (2 * HY_ORDER), axis=-1)
    h_ref[...] = h
    rows = lax.broadcasted_iota(jnp.int32, h.shape, 0) + i * ROW_TILE
    cols = lax.broadcasted_iota(jnp.int32, h.shape, 1)
    is_bwd = (cols // HY_WIDTH) % 2 == 1
    part = jnp.sum(jnp.where(is_bwd & (rows == 0), 0.0, jnp.abs(h)), axis=0, keepdims=True)

    @pl.when(i == 0)
    def _():
        asum_ref[...] = part

    @pl.when(i > 0)
    def _():
        asum_ref[...] += part


def hyena_filter_taps(length, w1, b1, w2, b2, w3):
    t = jnp.arange(length, dtype=F32)
    t_norm = t / length
    f = jnp.linspace(1e-4, HY_BANDS - 1, HY_BANDS, dtype=F32)
    wt = 2.0 * math.pi * t_norm
    feats = jnp.concatenate([t_norm[:, None], jnp.cos(wt[:, None] * f), -jnp.sin(wt[:, None] * f)], axis=-1)
    feats = jnp.pad(feats, ((0, 0), (0, 128 - HY_EMB_DIM)))
    w1p = jnp.pad(w1, ((0, 128 - HY_EMB_DIM), (0, 0)))
    deltas = jnp.abs(jnp.linspace(HY_MIN_DECAY, HY_MAX_DECAY, HY_WIDTH, dtype=F32))
    window = jnp.exp(-t_norm[:, None] * deltas[None])
    wout = HY_ORDER * 2 * HY_WIDTH
    full = lambda shp: pl.BlockSpec(shp, lambda i: (0, 0))
    return _pcall(
        _filter_kernel,
        grid=(length // ROW_TILE,),
        in_specs=[pl.BlockSpec((ROW_TILE, 128), lambda i: (i, 0)), full((128, HY_FILT_HIDDEN)),
                  full((1, HY_FILT_HIDDEN)), full((HY_FILT_HIDDEN, HY_FILT_HIDDEN)), full((1, HY_FILT_HIDDEN)),
                  full((HY_FILT_HIDDEN, wout)), pl.BlockSpec((ROW_TILE, HY_WIDTH), lambda i: (i, 0))],
        out_specs=[pl.BlockSpec((ROW_TILE, wout), lambda i: (i, 0)), full((1, wout))],
        out_shape=[jax.ShapeDtypeStruct((length, wout), F32), jax.ShapeDtypeStruct((1, wout), F32)],
        compiler_params=_cparams(("arbitrary",)),
    )(feats, w1p, b1.reshape(1, -1), w2, b2.reshape(1, -1), w3, window)


class _Dft:
    def __init__(self, n1):
        n2 = FFT_N2
        assert n1 % 16 == 0
        self.n1, self.n = n1, n1 * n2
        self.half = n1 // 2
        self.nf = n1 // 2 + 1
        self.nfp = -(-self.nf // 16) * 16
        f1 = np.arange(self.nf)[:, None]
        ang = 2.0 * np.pi * f1 * np.arange(n1)[None, :] / n1
        s1 = np.zeros((2 * self.nfp, n1))
        s1[:self.nf] = np.cos(ang)
        s1[self.nfp:self.nfp + self.nf] = -np.sin(ang)
        self.s1_full = s1
        k_data = max(self.half, 16)
        self.s1_data = np.zeros((2 * self.nfp, k_data))
        self.s1_data[:, :self.half] = s1[:, :self.half]
        tw = 2.0 * np.pi * f1 * np.arange(n2)[None, :] / self.n
        self.tw_cos = np.cos(tw)[:, :, None]
        self.tw_sin = np.sin(tw)[:, :, None]
        a2 = 2.0 * np.pi * np.arange(n2)[:, None] * np.arange(n2)[None, :] / n2
        wc, ws = np.cos(a2), np.sin(a2)
        self.m_fwd = np.block([[wc, ws], [-ws, wc]])
        self.m_inv = np.block([[wc, -ws], [ws, wc]])
        wgt = np.full(self.nf, 2.0)
        wgt[0] = 1.0
        wgt[-1] = 1.0
        ango = 2.0 * np.pi * np.arange(self.half)[:, None] * np.arange(self.nf)[None, :] / n1
        self.s4_re = np.zeros((self.half, self.nfp))
        self.s4_im = np.zeros((self.half, self.nfp))
        self.s4_re[:, :self.nf] = np.cos(ango) * wgt / self.n
        self.s4_im[:, :self.nf] = -np.sin(ango) * wgt / self.n

    @staticmethod
    def const(a, dtype=BF16):
        return jnp.asarray(a, F32).astype(dtype)


def _fft_s1_kernel(m_ref, x_ref, o_ref):
    x = x_ref[...]
    k = m_ref.shape[1]
    if x.shape[0] < k:
        x = jnp.concatenate([x, jnp.zeros((k - x.shape[0], x.shape[1]), x.dtype)], axis=0)
    o_ref[...] = jnp.dot(m_ref[...], x.astype(BF16), preferred_element_type=F32)


def fft_stage1(mat, x, *, n_batch, k_rows, tn):
    cols = x.shape[1]
    m, k = mat.shape
    return _pcall(
        _fft_s1_kernel,
        grid=(n_batch, cols // tn),
        in_specs=[pl.BlockSpec((m, k), lambda b, j: (0, 0)), pl.BlockSpec((k_rows, tn), lambda b, j: (b, j))],
        out_specs=pl.BlockSpec((None, m, tn), lambda b, j: (b, 0, j)),
        out_shape=jax.ShapeDtypeStruct((n_batch, m, cols), F32),
        compiler_params=_cparams(("parallel", "parallel")),
    )(mat, x)


def _twiddle(ar, ai, c, s):
    return ar * c + ai * s, ai * c - ar * s


def _fft_filter_mid_kernel(ar_ref, ai_ref, c_ref, s_ref, mf_ref, sc_ref, kr_ref, ki_ref):
    br, bi = _twiddle(ar_ref[...], ai_ref[...], c_ref[...], s_ref[...])
    x = jnp.dot(mf_ref[...], jnp.concatenate([br, bi], axis=0).astype(BF16), preferred_element_type=F32)
    kr_ref[...] = x[:FFT_N2] * sc_ref[...]
    ki_ref[...] = x[FFT_N2:] * sc_ref[...]


def fft_filter_mid(a, dft, kscale):
    cols = a.shape[-1]
    n2 = FFT_N2
    tw = pl.BlockSpec((None, n2, 1), lambda f: (f, 0, 0))
    blk = pl.BlockSpec((None, n2, cols), lambda f: (f, 0, 0))
    return _pcall(
        _fft_filter_mid_kernel,
        grid=(dft.nf,),
        in_specs=[pl.BlockSpec((None, n2, cols), lambda f: (f, 0, 0)),
                  pl.BlockSpec((None, n2, cols), lambda f: (dft.nfp + f, 0, 0)), tw, tw,
                  pl.BlockSpec((2 * n2, 2 * n2), lambda f: (0, 0)), pl.BlockSpec((1, cols), lambda f: (0, 0))],
        out_specs=[blk, blk],
        out_shape=[jax.ShapeDtypeStruct((dft.nf, n2, cols), F32)] * 2,
        compiler_params=_cparams(("parallel",)),
    )(a, a, _Dft.const(dft.tw_cos, F32), _Dft.const(dft.tw_sin, F32), _Dft.const(dft.m_fwd), kscale)


def _fft_mid_kernel(ar_ref, ai_ref, c_ref, s_ref, mf_ref, mi_ref, kr_ref, ki_ref, er_ref, ei_ref, *, nf):
    @pl.when(pl.program_id(1) < nf)
    def _():
        c, s = c_ref[...], s_ref[...]
        br, bi = _twiddle(ar_ref[...], ai_ref[...], c, s)
        x = jnp.dot(mf_ref[...], jnp.concatenate([br, bi], axis=0).astype(BF16), preferred_element_type=F32)
        xr, xi = x[:FFT_N2], x[FFT_N2:]
        kr, ki = kr_ref[...], ki_ref[...]
        yr = xr * kr - xi * ki
        yi = xr * ki + xi * kr
        dd = jnp.dot(mi_ref[...], jnp.concatenate([yr, yi], axis=0).astype(BF16), preferred_element_type=F32)
        dr, di = dd[:FFT_N2], dd[FFT_N2:]
        er_ref[...] = dr * c - di * s
        ei_ref[...] = di * c + dr * s

    @pl.when(pl.program_id(1) >= nf)
    def _():
        er_ref[...] = jnp.zeros_like(er_ref)
        ei_ref[...] = jnp.zeros_like(ei_ref)


def fft_mid(a, dft, kr, ki, order):
    nb = a.shape[0]
    n2, w = FFT_N2, HY_WIDTH
    nf, nfp = dft.nf, dft.nfp
    fc = lambda f: jnp.minimum(f, nf - 1)
    tw = pl.BlockSpec((None, n2, 1), lambda b, f: (fc(f), 0, 0))
    mat = pl.BlockSpec((2 * n2, 2 * n2), lambda b, f: (0, 0))
    kf = pl.BlockSpec((None, n2, w), lambda b, f: (fc(f), 0, order))
    out = pl.BlockSpec((None, None, n2, w), lambda b, f: (b, f, 0, 0))
    return _pcall(
        functools.partial(_fft_mid_kernel, nf=nf),
        grid=(nb, nfp),
        in_specs=[pl.BlockSpec((None, None, n2, w), lambda b, f: (b, fc(f), 0, 0)),
                  pl.BlockSpec((None, None, n2, w), lambda b, f: (b, nfp + fc(f), 0, 0)),
                  tw, tw, mat, mat, kf, kf],
        out_specs=[out, out],
        out_shape=[jax.ShapeDtypeStruct((nb, nfp, n2, w), F32)] * 2,
        compiler_params=_cparams(("parallel", "arbitrary")),
    )(a, a, _Dft.const(dft.tw_cos, F32), _Dft.const(dft.tw_sin, F32), _Dft.const(dft.m_fwd),
      _Dft.const(dft.m_inv), kr, ki)


def _fft_s4_kernel(mr_ref, mi_ref, er_ref, ei_ref, gate_ref, z_ref, bias_ref, scale_ref, o_ref):
    y = (jnp.dot(mr_ref[...], er_ref[...].astype(BF16), preferred_element_type=F32)
         + jnp.dot(mi_ref[...], ei_ref[...].astype(BF16), preferred_element_type=F32))
    o_ref[...] = (gate_ref[...] * (y + bias_ref[...] * z_ref[...]) * scale_ref[...]).astype(o_ref.dtype)


def fft_stage4(dft, er, ei, gate, z, bias_row, scale_row, *, tn, out_dtype):
    nb = er.shape[0]
    m, nfp = dft.half, dft.nfp
    cols = z.shape[1]
    row = pl.BlockSpec((m, tn), lambda b, j: (b, j))
    vec = pl.BlockSpec((1, tn), lambda b, j: (0, j))
    mat = pl.BlockSpec((m, nfp), lambda b, j: (0, 0))
    spec = pl.BlockSpec((None, nfp, tn), lambda b, j: (b, 0, j))
    return _pcall(
        _fft_s4_kernel,
        grid=(nb, cols // tn),
        in_specs=[mat, mat, spec, spec, row, row, vec, vec],
        out_specs=row,
        out_shape=jax.ShapeDtypeStruct((nb * m, cols), out_dtype),
        compiler_params=_cparams(("parallel", "parallel")),
    )(_Dft.const(dft.s4_re), _Dft.const(dft.s4_im), er, ei, gate, z, bias_row, scale_row)


def hyena_long_conv(z, x1, x2, filt, hbias, beta_hy, *, bsz, length, n1):
    dft = _Dft(n1)
    n2, w = FFT_N2, HY_WIDTH
    half, n = dft.half, dft.n
    assert half >= 8
    taps, asum = filt
    taps = taps.reshape(length, HY_ORDER, 2, w)
    h_fwd = taps[:, :, 0].reshape(length, HY_ORDER * w)
    h_bwd = taps[:, :, 1].reshape(length, HY_ORDER * w)
    k_full = jnp.concatenate([h_fwd, jnp.zeros((n - 2 * length + 1, HY_ORDER * w), F32), h_bwd[1:][::-1]], axis=0)
    asum = asum.reshape(HY_ORDER, 2, w)
    kscale = (1.0 / (asum[:, 0] + asum[:, 1] + NORM_EPS)).reshape(1, HY_ORDER * w)
    cols_f = n2 * HY_ORDER * w
    a_f = fft_stage1(_Dft.const(dft.s1_full), k_full.reshape(n1, cols_f), n_batch=1, k_rows=n1, tn=4096)
    kr, ki = fft_filter_mid(a_f.reshape(2 * dft.nfp, n2, HY_ORDER * w), dft, kscale)
    cols = n2 * w
    s1 = _Dft.const(dft.s1_data)
    ones = jnp.ones((1, cols), F32)
    zc = z.reshape(bsz * half, cols)
    for o, gate in enumerate((x1, x2)):
        a = fft_stage1(s1, zc, n_batch=bsz, k_rows=half, tn=4096)
        er, ei = fft_mid(a.reshape(bsz, 2 * dft.nfp, n2, w), dft, kr, ki, o)
        last = o == HY_ORDER - 1
        zc = fft_stage4(dft, er.reshape(bsz, dft.nfp, cols), ei.reshape(bsz, dft.nfp, cols),
                        gate.reshape(bsz * half, cols), zc, jnp.tile(hbias[o], n2).reshape(1, cols),
                        jnp.tile(beta_hy, n2).reshape(1, cols) if last else ones,
                        tn=4096, out_dtype=BF16 if last else F32)
    return zc.reshape(bsz * half * n2, w)


def _rope_tables(n_ctx, n_lat):
    rows = n_lat // GRID_W
    row = jnp.repeat(jnp.arange(rows, dtype=F32), GRID_W)
    col = jnp.tile(jnp.arange(GRID_W, dtype=F32), rows)
    n_freq = HEAD_DIM // 4
    freqs = ROPE_THETA ** (-jnp.arange(n_freq, dtype=F32) / n_freq)
    ar = row[:, None] * freqs
    ac = col[:, None] * freqs
    cos = jnp.concatenate([jnp.cos(ar), jnp.cos(ar), jnp.cos(ac), jnp.cos(ac)], axis=-1)
    sin = jnp.concatenate([jnp.sin(ar), jnp.sin(ar), jnp.sin(ac), jnp.sin(ac)], axis=-1)
    cos = jnp.concatenate([jnp.ones((n_ctx, HEAD_DIM), F32), cos], axis=0)
    sin = jnp.concatenate([jnp.zeros((n_ctx, HEAD_DIM), F32), sin], axis=0)
    even = (np.arange(HEAD_DIM) // n_freq) % 2 == 0
    sin_a = jnp.where(even, -sin, 0.0)
    sin_b = jnp.where(even, 0.0, sin)
    tile2 = lambda a: jnp.concatenate([a, a], axis=-1)
    return tile2(cos), tile2(sin_a), tile2(sin_b)


def _attn_key_chunk(t):
    for tk in (1408, 1280, 1024, 768, 512, 256):
        if t % tk == 0:
            return tk
    raise ValueError(t)


def kernel(x, c, ctx, c_ctx, ada_w, ada_b, norm_g, w_in, w_out, mix_beta, gqa_qk_g, diff_lambda, diff_subln_g,
           ret_log_decay, ret_gn_g, hy_short_w, hy_short_b, hy_filt_w1, hy_filt_b1, hy_filt_w2, hy_filt_b2,
           hy_filt_w3, hy_bias, ffn_w_gate, ffn_w_up, ffn_w_down, moe_router, moe_w_gate, moe_w_up, moe_w_down):
    bsz, n_lat, d = x.shape
    n_ctx = ctx.shape[1]
    t = n_ctx + n_lat
    r = bsz * t
    n_t, n_c = t // ROW_TILE, n_ctx // ROW_TILE
    n_s = n_lat // ROW_TILE
    assert n_ctx % ROW_TILE == 0 and n_lat % ROW_TILE == 0 and r % 512 == 0
    tk = _attn_key_chunk(t)
    assert tk >= n_ctx

    rope_c, rope_sa, rope_sb = _rope_tables(n_ctx, n_lat)
    xs = jnp.concatenate([ctx, x], axis=1).reshape(r, d)

    cvec = jnp.stack([jnp.broadcast_to(c_ctx, c.shape), c], axis=1).reshape(2 * bsz, d)
    cvec = jax.nn.silu(cvec)
    cvec = jnp.pad(cvec, ((0, 16 - 2 * bsz), (0, 0))).astype(BF16)

    n1_lat = 2 * n_lat // FFT_N2
    ctx_pad_tiles = max(n_c, 1024 // ROW_TILE)
    n1_ctx = 2 * ctx_pad_tiles * ROW_TILE // FFT_N2

    for l in range(DEPTH):
        lambda_init = 0.8 - 0.6 * math.exp(-0.3 * l)
        mod = mm(cvec, ada_w, tm=16, tn=1536, out_dtype=F32, lead=(l,))[:2 * bsz] + ada_b[l]
        mod = mod.reshape(2 * bsz, 6, 1, d)
        sh_m, sc_m, gt_m, sh_f, sc_f, gt_f = (mod[:, k] for k in range(6))
        beta = mix_beta[l].reshape(1, 4 * GROUP_WIDTH)
        beta_a, beta_b, beta_r, beta_d = (beta[:, GROUP_WIDTH * k:GROUP_WIDTH * (k + 1)] for k in range(4))

        (h,) = norm_mod(xs, norm_g[l, 0], sh_m, sc_m, n_t, n_c, (BF16,))
        p_attn = mm(h, w_in, tm=512, tn=768, out_dtype=F32, lead=(l,), n_cols=COL_ATTN)
        p_hy = mm(h, w_in, tm=512, tn=768, out_dtype=F32, lead=(l,), col_off=COL_HY // 768,
                  n_cols=IN_WIDTH - COL_HY)
        aq, ak, av, dq, dk, dv, rq, rk, rv = prep_heads(p_attn, rope_c, rope_sa, rope_sb, gqa_qk_g[l], bsz, n_t)

        a_out = gqa_attention(aq, ak, av, beta_a, n_t=n_t, n_c=n_c, tk=tk)

        lamf = diff_lambda[l]
        lam_full = (jnp.exp(jnp.sum(lamf[0] * lamf[1])) - jnp.exp(jnp.sum(lamf[2] * lamf[3])) + lambda_init)
        b_out = diff_attention(dq, dk, dv, lam_full.reshape(1, 1), diff_subln_g[l].reshape(1, -1), beta_b,
                               1.0 - lambda_init, n_t=n_t, n_c=n_c, tk=tk)

        o2 = retention(rq, rk, rv, ret_log_decay[l], n_t=n_t, n_c=n_c)
        r_out = ret_gate(o2, p_attn, ret_gn_g[l].reshape(1, -1), beta_r)

        filt_args = (hy_filt_w1[l], hy_filt_b1[l], hy_filt_w2[l], hy_filt_b2[l], hy_filt_w3[l])
        z_l, x1_l, x2_l = short_conv(p_hy, hy_short_w[l], hy_short_b[l], bsz=bsz, n_t=n_t, first_tile=n_c,
                                     n_seq_tiles=n_s, n_pad_tiles=n_s)
        d_l = hyena_long_conv(z_l, x1_l, x2_l, hyena_filter_taps(n_lat, *filt_args), hy_bias[l], beta_d[0],
                              bsz=bsz, length=n_lat, n1=n1_lat)
        z_c, x1_c, x2_c = short_conv(p_hy, hy_short_w[l], hy_short_b[l], bsz=bsz, n_t=n_t, first_tile=0,
                                     n_seq_tiles=n_c, n_pad_tiles=ctx_pad_tiles)
        d_c = hyena_long_conv(z_c, x1_c, x2_c, hyena_filter_taps(n_ctx, *filt_args), hy_bias[l], beta_d[0],
                              bsz=bsz, length=n_ctx, n1=n1_ctx)
        d_out = jnp.concatenate([d_c.reshape(bsz, -1, HY_WIDTH)[:, :n_ctx], d_l.reshape(bsz, n_lat, HY_WIDTH)],
                                axis=1).reshape(r, HY_WIDTH)

        mixed = jnp.concatenate([a_out, b_out, r_out, d_out], axis=-1)
        y = mm(mixed, w_out, tm=512, tn=512, out_dtype=F32, lead=(l,))
        xs = gate_res(xs, y, norm_g[l, 1], gt_m, n_t, n_c)

        i = l // 2
        if l % 2 == 0:
            (f,) = norm_mod(xs, norm_g[l, 2], sh_f, sc_f, n_t, n_c, (BF16,))
            hid = swiglu_up(f, ffn_w_gate, ffn_w_up, tm=512, tn=512, lead=(i,))
            f_out = mm(hid, ffn_w_down, tm=512, tn=256, out_dtype=F32, lead=(i,))
        else:
            f, f32_in = norm_mod(xs, norm_g[l, 2], sh_f, sc_f, n_t, n_c, (BF16, F32))
            comb = router(f32_in, moe_router[i].T)
            f_out = jnp.zeros((r, d), F32)
            for e in range(N_EXPERTS):
                hid = swiglu_up(f, moe_w_gate, moe_w_up, tm=512, tn=512, lead=(i, e))
                f_out = mm_acc(hid, moe_w_down, comb[e].reshape(r, 1), f_out, tm=512, tn=512, lead=(i, e))
        xs = gate_res(xs, f_out, norm_g[l, 3], gt_f, n_t, n_c)

    return xs.reshape(bsz, t, d)[:, n_ctx:]
```

```python
import functools
import math

import numpy as np
import jax
import jax.numpy as jnp
from jax import lax
from jax.experimental import pallas as pl
from jax.experimental.pallas import tpu as pltpu

F32 = jnp.float32
BF16 = jnp.bfloat16

D_MODEL = 2048
DEPTH = 4
GRID_W = 64
HEAD_DIM = 64
ROPE_THETA = 10000.0
NORM_EPS = 1e-6
GROUP_WIDTH = 512
GQA_Q_HEADS = 8
GQA_KV_HEADS = 2
GQA_REP = 4
DIFF_HEADS = 4
RET_HEADS = 4
HY_WIDTH = 512
HY_ORDER = 2
HY_EMB_DIM = 33
HY_BANDS = 16
HY_FILT_HIDDEN = 64
HY_MIN_DECAY = math.log(1e-2) / 1.5
HY_MAX_DECAY = math.log(1e-2) / 0.3
N_EXPERTS = 8

COL_ATTN = 3840
COL_DIFF = 768
COL_RET = 2304
COL_RET_GATE = 3328
COL_HY = 3840
IN_WIDTH = 5376

ROW_TILE = 256
MOE_TM = 1024
MOE_TF = 256
FFT_N2 = 128
FFT_SUB = 16
VMEM_LIMIT_MB = 48


def _cparams(sem, vmem_mb=VMEM_LIMIT_MB):
    return pltpu.CompilerParams(dimension_semantics=sem, vmem_limit_bytes=vmem_mb * 1024 * 1024)


def _pcall(kern, **kw):
    fn = kern.func if isinstance(kern, functools.partial) else kern
    return pl.pallas_call(kern, name=fn.__name__.strip("_"), **kw)


def _mm_kernel(a_ref, b_ref, o_ref):
    o_ref[...] = jnp.dot(a_ref[...].astype(BF16), b_ref[...].astype(BF16),
                         preferred_element_type=F32).astype(o_ref.dtype)


def _b_spec(b, lead, k, tn, col_off):
    nlead = len(lead)
    return pl.BlockSpec((None,) * nlead + (k, tn), lambda j, i: tuple(lead) + (0, j + col_off))


def mm(a, b, *, tm, tn, out_dtype, lead=(), col_off=0, n_cols=None):
    m, k = a.shape
    n = b.shape[-1] if n_cols is None else n_cols
    assert m % tm == 0 and n % tn == 0, (m, tm, n, tn)
    return _pcall(
        _mm_kernel,
        grid=(n // tn, m // tm),
        in_specs=[pl.BlockSpec((tm, k), lambda j, i: (i, 0)), _b_spec(b, lead, k, tn, col_off)],
        out_specs=pl.BlockSpec((tm, tn), lambda j, i: (i, j)),
        out_shape=jax.ShapeDtypeStruct((m, n), out_dtype),
        compiler_params=_cparams(("parallel", "parallel")),
    )(a, b)


def _swiglu_up_kernel(a_ref, wg_ref, wu_ref, o_ref):
    a = a_ref[...]
    g = jnp.dot(a, wg_ref[...].astype(BF16), preferred_element_type=F32)
    u = jnp.dot(a, wu_ref[...].astype(BF16), preferred_element_type=F32)
    o_ref[...] = (g * jax.nn.sigmoid(g) * u).astype(o_ref.dtype)


def swiglu_up(a, wg, wu, *, tm, tn, lead):
    m, k = a.shape
    n = wg.shape[-1]
    assert m % tm == 0 and n % tn == 0
    return _pcall(
        _swiglu_up_kernel,
        grid=(n // tn, m // tm),
        in_specs=[pl.BlockSpec((tm, k), lambda j, i: (i, 0)),
                  _b_spec(wg, lead, k, tn, 0), _b_spec(wu, lead, k, tn, 0)],
        out_specs=pl.BlockSpec((tm, tn), lambda j, i: (i, j)),
        out_shape=jax.ShapeDtypeStruct((m, n), BF16),
        compiler_params=_cparams(("parallel", "parallel")),
    )(a, wg, wu)


def _seg_map(n_t, n_c):
    return lambda i: (2 * (i // n_t) + jnp.where(i % n_t >= n_c, 1, 0), 0, 0)


def _rms(x):
    return x * lax.rsqrt(jnp.mean(x * x, axis=-1, keepdims=True) + NORM_EPS)


def _norm_mod_kernel(x_ref, g_ref, sh_ref, sc_ref, *o_refs):
    y = _rms(x_ref[...]) * g_ref[...]
    y = y * (1.0 + sc_ref[...]) + sh_ref[...]
    for o_ref in o_refs:
        o_ref[...] = y.astype(o_ref.dtype)


def norm_mod(x, g, shift, scale, n_t, n_c, out_dtypes):
    r, d = x.shape
    seg = _seg_map(n_t, n_c)
    row = pl.BlockSpec((ROW_TILE, d), lambda i: (i, 0))
    outs = _pcall(
        _norm_mod_kernel,
        grid=(r // ROW_TILE,),
        in_specs=[row, pl.BlockSpec((1, d), lambda i: (0, 0)),
                  pl.BlockSpec((None, 1, d), seg), pl.BlockSpec((None, 1, d), seg)],
        out_specs=[row] * len(out_dtypes),
        out_shape=[jax.ShapeDtypeStruct((r, d), dt) for dt in out_dtypes],
        compiler_params=_cparams(("parallel",)),
    )(x, g.reshape(1, d), shift, scale)
    return outs


def _gate_res_kernel(x_ref, y_ref, g_ref, gt_ref, o_ref):
    o_ref[...] = x_ref[...] + gt_ref[...] * (_rms(y_ref[...]) * g_ref[...])


def gate_res(x, y, g, gate, n_t, n_c):
    r, d = x.shape
    row = pl.BlockSpec((ROW_TILE, d), lambda i: (i, 0))
    return _pcall(
        _gate_res_kernel,
        grid=(r // ROW_TILE,),
        in_specs=[row, row, pl.BlockSpec((1, d), lambda i: (0, 0)),
                  pl.BlockSpec((None, 1, d), _seg_map(n_t, n_c))],
        out_specs=row,
        out_shape=jax.ShapeDtypeStruct((r, d), F32),
        compiler_params=_cparams(("parallel",)),
    )(x, y, g.reshape(1, d), gate)


def _router_kernel(f_ref, rt_ref, idx_ref, w_ref):
    lt = lax.dot_general(rt_ref[...], f_ref[...], (((1,), (1,)), ((), ())),
                         precision=lax.Precision.HIGHEST, preferred_element_type=F32)
    e_idx = lax.broadcasted_iota(jnp.int32, lt.shape, 0)
    m1 = jnp.max(lt, axis=0, keepdims=True)
    i1 = jnp.min(jnp.where(lt == m1, e_idx, N_EXPERTS), axis=0, keepdims=True)
    rest = jnp.where(e_idx == i1, -jnp.inf, lt)
    m2 = jnp.max(rest, axis=0, keepdims=True)
    i2 = jnp.min(jnp.where(rest == m2, e_idx, N_EXPERTS), axis=0, keepdims=True)
    e2 = jnp.exp(m2 - m1)
    idx_ref[...] = jnp.concatenate([i1, i2], axis=0)
    w_ref[...] = jnp.concatenate([1.0 / (1.0 + e2), e2 / (1.0 + e2)], axis=0)


def router(f, router_t):
    r, d = f.shape
    return _pcall(
        _router_kernel,
        grid=(r // ROW_TILE,),
        in_specs=[pl.BlockSpec((ROW_TILE, d), lambda i: (i, 0)),
                  pl.BlockSpec((N_EXPERTS, d), lambda i: (0, 0))],
        out_specs=[pl.BlockSpec((2, ROW_TILE), lambda i: (0, i))] * 2,
        out_shape=[jax.ShapeDtypeStruct((2, r), jnp.int32), jax.ShapeDtypeStruct((2, r), F32)],
        compiler_params=_cparams(("parallel",)),
    )(f, router_t)


def _row_copy(src_hbm, dst, src_row, dst_row, sem):
    return pltpu.make_async_copy(src_hbm.at[pl.ds(src_row, 1), :], dst.at[pl.ds(dst_row, 1), :], sem)


def _gather_rows(src_hbm, dst, idx_ref, base, n_rows, sem):
    def start(r, carry):
        _row_copy(src_hbm, dst, idx_ref[base + r], r, sem).start()
        return carry

    def wait(r, carry):
        _row_copy(src_hbm, dst, idx_ref[base + r], r, sem).wait()
        return carry

    lax.fori_loop(0, n_rows, start, 0)
    lax.fori_loop(0, n_rows, wait, 0)


def route_tokens(idx, w, n_pad_rows):
    r = idx.shape[1]
    n_assign = 2 * r
    n_tiles = n_pad_rows // MOE_TM
    e_flat = idx.reshape(n_assign)
    w_flat = w.reshape(n_assign)
    counts = jnp.sum((e_flat[:, None] == jnp.arange(N_EXPERTS)[None, :]).astype(jnp.int32), axis=0)
    padded = (counts + MOE_TM - 1) // MOE_TM * MOE_TM
    group_end = jnp.cumsum(padded)
    group_start = group_end - padded
    sorted_start = jnp.cumsum(counts) - counts
    order = jnp.argsort(e_flat, stable=True).astype(jnp.int32)
    e_sorted = e_flat[order]
    dest = group_start[e_sorted] + (jnp.arange(n_assign, dtype=jnp.int32) - sorted_start[e_sorted])
    row_token = jnp.zeros((n_pad_rows,), jnp.int32).at[dest].set(order % r)
    row_weight = jnp.zeros((n_pad_rows,), F32).at[dest].set(w_flat[order])
    pos = jnp.zeros((n_assign,), jnp.int32).at[order].set(dest)
    n_used = group_end[-1] // MOE_TM
    tile_start = jnp.arange(n_tiles, dtype=jnp.int32) * MOE_TM
    tile_start = jnp.minimum(tile_start, group_end[-1] - 1)
    tile_expert = jnp.sum((tile_start[:, None] >= group_end[None, :]).astype(jnp.int32), axis=1)
    return (row_token, row_weight.reshape(n_pad_rows, 1), tile_expert.astype(jnp.int32),
            n_used.astype(jnp.int32).reshape(1), pos)


def _moe_expert_kernel(te_ref, nu_ref, tok_ref, f_hbm, wg_ref, wu_ref, wd_ref, rw_ref, o_ref, xb_ref, sem):
    i, f = pl.program_id(0), pl.program_id(1)
    used = i < nu_ref[0]

    @pl.when(jnp.logical_and(used, f == 0))
    def _():
        _gather_rows(f_hbm, o_ref, tok_ref, i * MOE_TM, MOE_TM, sem)
        xb_ref[...] = o_ref[...].astype(BF16)

    @pl.when(f == 0)
    def _():
        o_ref[...] = jnp.zeros_like(o_ref)

    @pl.when(used)
    def _():
        x = xb_ref[...]
        g = jnp.dot(x, wg_ref[...].astype(BF16), preferred_element_type=F32)
        u = jnp.dot(x, wu_ref[...].astype(BF16), preferred_element_type=F32)
        h = (g * jax.nn.sigmoid(g) * u).astype(BF16)
        o_ref[...] += jnp.dot(h, wd_ref[...].astype(BF16), preferred_element_type=F32)

    @pl.when(jnp.logical_and(used, f == pl.num_programs(1) - 1))
    def _():
        o_ref[...] = o_ref[...] * rw_ref[...]


def moe_experts(f_in, wg, wu, wd, layer, row_token, row_weight, tile_expert, n_used):
    r, d = f_in.shape
    ff = wg.shape[-1]
    n_pad_rows = row_token.shape[0]
    nt, nf = n_pad_rows // MOE_TM, ff // MOE_TF

    def fi(i, f, nu):
        return jnp.where(i < nu[0], f, nf - 1)

    grid_spec = pltpu.PrefetchScalarGridSpec(
        num_scalar_prefetch=3,
        grid=(nt, nf),
        in_specs=[pl.BlockSpec(memory_space=pl.ANY),
                  pl.BlockSpec((None, None, d, MOE_TF), lambda i, f, te, nu, tok: (layer, te[i], 0, fi(i, f, nu))),
                  pl.BlockSpec((None, None, d, MOE_TF), lambda i, f, te, nu, tok: (layer, te[i], 0, fi(i, f, nu))),
                  pl.BlockSpec((None, None, MOE_TF, d), lambda i, f, te, nu, tok: (layer, te[i], fi(i, f, nu), 0)),
                  pl.BlockSpec((MOE_TM, 1), lambda i, f, te, nu, tok: (i, 0))],
        out_specs=pl.BlockSpec((MOE_TM, d), lambda i, f, te, nu, tok: (i, 0)),
        scratch_shapes=[pltpu.VMEM((MOE_TM, d), BF16), pltpu.SemaphoreType.DMA(())],
    )
    return _pcall(
        _moe_expert_kernel,
        grid_spec=grid_spec,
        out_shape=jax.ShapeDtypeStruct((n_pad_rows, d), F32),
        compiler_params=_cparams(("arbitrary", "arbitrary")),
    )(tile_expert, n_used, row_token, f_in, wg, wu, wd, row_weight)


def _moe_combine_kernel(pos_ref, y_hbm, x_ref, g_ref, gt_ref, o_ref, buf_ref, sem):
    i = pl.program_id(0)
    for k in range(2):
        _gather_rows(y_hbm, buf_ref.at[k], pos_ref, (2 * i + k) * ROW_TILE, ROW_TILE, sem)
    y = buf_ref[0] + buf_ref[1]
    o_ref[...] = x_ref[...] + gt_ref[...] * (_rms(y) * g_ref[...])


def moe_combine_gate_res(x, y_sorted, pos, g, gate, n_t, n_c):
    r, d = x.shape
    nrt = r // ROW_TILE
    pos_tiles = pos.reshape(2, nrt, ROW_TILE).transpose(1, 0, 2).reshape(2 * r)
    seg = _seg_map(n_t, n_c)
    row = pl.BlockSpec((ROW_TILE, d), lambda i, p: (i, 0))
    grid_spec = pltpu.PrefetchScalarGridSpec(
        num_scalar_prefetch=1,
        grid=(nrt,),
        in_specs=[pl.BlockSpec(memory_space=pl.ANY), row, pl.BlockSpec((1, d), lambda i, p: (0, 0)),
                  pl.BlockSpec((None, 1, d), lambda i, p: seg(i))],
        out_specs=row,
        scratch_shapes=[pltpu.VMEM((2, ROW_TILE, d), F32), pltpu.SemaphoreType.DMA(())],
    )
    return _pcall(
        _moe_combine_kernel,
        grid_spec=grid_spec,
        out_shape=jax.ShapeDtypeStruct((r, d), F32),
        compiler_params=_cparams(("arbitrary",)),
    )(pos_tiles, y_sorted, x, g.reshape(1, d), gate)


def _rope(x, c, sa, sb):
    return x * c + pltpu.roll(x, 112, 1) * sa + pltpu.roll(x, 16, 1) * sb


def _head_rms(x, g, bd):
    sq = x * x
    hi = sq.astype(BF16)
    lo = (sq - hi.astype(F32)).astype(BF16)
    ms = jnp.dot(hi, bd, preferred_element_type=F32) + jnp.dot(lo, bd, preferred_element_type=F32)
    return x * lax.rsqrt(ms + NORM_EPS) * g


def _prep_kernel(p_ref, c_ref, sa_ref, sb_ref, gq_ref, gk_ref, bd_ref,
                 aq_ref, ak_ref, av_ref, dq_ref, dk_ref, dv_ref, rq_ref, rk_ref, rv_ref):
    c, sa, sb = c_ref[...], sa_ref[...], sb_ref[...]
    bd = bd_ref[...]
    scale = HEAD_DIM ** -0.5

    def chunk(j):
        return p_ref[:, 128 * j:128 * (j + 1)]

    def put_heads(ref, first, val):
        ref[first] = val[:, :HEAD_DIM].astype(ref.dtype)
        ref[first + 1] = val[:, HEAD_DIM:].astype(ref.dtype)

    for j in range(4):
        put_heads(aq_ref, 2 * j, _rope(_head_rms(chunk(j), gq_ref[...], bd), c, sa, sb) * scale)
    put_heads(ak_ref, 0, _rope(_head_rms(chunk(4), gk_ref[...], bd), c, sa, sb))
    put_heads(av_ref, 0, chunk(5))
    for j in range(4):
        put_heads(dq_ref, 2 * j, _rope(chunk(6 + j), c, sa, sb) * scale)
        put_heads(dk_ref, 2 * j, _rope(chunk(10 + j), c, sa, sb))
        dv_ref[j] = chunk(14 + j).astype(dv_ref.dtype)
    for j in range(2):
        put_heads(rq_ref, 2 * j, _rope(chunk(18 + j), c, sa, sb))
        put_heads(rk_ref, 2 * j, _rope(chunk(20 + j), c, sa, sb) * scale)
    for j in range(4):
        rv_ref[j] = chunk(22 + j).astype(rv_ref.dtype)


def prep_heads(p_attn, rope_c, rope_sa, rope_sb, qk_g, bsz, n_t):
    t = n_t * ROW_TILE
    bd = jnp.asarray(np.kron(np.eye(2), np.full((HEAD_DIM, HEAD_DIM), 1.0 / HEAD_DIM)), F32).astype(BF16)
    gq = jnp.tile(qk_g[0], 2).reshape(1, 128)
    gk = jnp.tile(qk_g[1], 2).reshape(1, 128)
    tab = pl.BlockSpec((ROW_TILE, 128), lambda i: (i % n_t, 0))
    vec = pl.BlockSpec((1, 128), lambda i: (0, 0))

    def heads(nh, dh):
        return (pl.BlockSpec((None, nh, ROW_TILE, dh), lambda i: (i // n_t, 0, i % n_t, 0)),
                jax.ShapeDtypeStruct((bsz, nh, t, dh), BF16))

    outs = [heads(8, 64), heads(2, 64), heads(2, 64), heads(8, 64), heads(8, 64), heads(4, 128),
            heads(4, 64), heads(4, 64), heads(4, 128)]
    return _pcall(
        _prep_kernel,
        grid=(bsz * n_t,),
        in_specs=[pl.BlockSpec((ROW_TILE, COL_RET_GATE), lambda i: (i, 0)), tab, tab, tab, vec, vec,
                  pl.BlockSpec((128, 128), lambda i: (0, 0))],
        out_specs=[o[0] for o in outs],
        out_shape=[o[1] for o in outs],
        compiler_params=_cparams(("parallel",)),
    )(p_attn, rope_c, rope_sa, rope_sb, gq, gk, bd)


def _nt_dot(a, b):
    return lax.dot_general(a, b, (((1,), (1,)), ((), ())), preferred_element_type=F32)


def _softmax_chunks(score_fn, v_ref, rows, dv, n_chunks, tk, n_ctx_keys, is_ctx):
    def step(kk, carry, mask_keys):
        m, l, acc = carry
        start = pl.multiple_of(kk * tk, tk)
        s = score_fn(start, tk)
        if mask_keys:
            col = lax.broadcasted_iota(jnp.int32, s.shape, 1)
            s = jnp.where(col < n_ctx_keys, s, -1e30)
        m_new = jnp.maximum(m, jnp.max(s, axis=-1, keepdims=True))
        alpha = jnp.exp(m - m_new)
        p = jnp.exp(s - m_new)
        l = alpha * l + jnp.sum(p, axis=-1, keepdims=True)
        acc = alpha * acc + jnp.dot(p.astype(BF16), v_ref[pl.ds(start, tk), :], preferred_element_type=F32)
        return m_new, l, acc

    init = (jnp.full((rows, 1), -1e30, F32), jnp.zeros((rows, 1), F32), jnp.zeros((rows, dv), F32))
    _, l, acc = lax.cond(is_ctx,
                         lambda: step(0, init, True),
                         lambda: lax.fori_loop(0, n_chunks, lambda kk, c: step(kk, c, False), init))
    return acc / l


def _gqa_kernel(q_ref, k_ref, v_ref, beta_ref, o_ref, *, tq, tk, n_chunks, n_ctx_tiles, n_ctx_keys):
    is_ctx = pl.program_id(2) < n_ctx_tiles
    q = q_ref[...].reshape(GQA_REP * tq, HEAD_DIM)

    def scores(start, size):
        return _nt_dot(q, k_ref[pl.ds(start, size), :])

    o = _softmax_chunks(scores, v_ref, GQA_REP * tq, HEAD_DIM, n_chunks, tk, n_ctx_keys, is_ctx)
    for r in range(GQA_REP):
        sl = slice(HEAD_DIM * r, HEAD_DIM * (r + 1))
        o_ref[:, sl] = (o[r * tq:(r + 1) * tq] * beta_ref[:, sl]).astype(o_ref.dtype)


def gqa_attention(q, k, v, beta, *, n_t, n_c, tk):
    bsz, _, t, _ = q.shape
    tq = ROW_TILE
    kern = functools.partial(_gqa_kernel, tq=tq, tk=tk, n_chunks=t // tk, n_ctx_tiles=n_c,
                             n_ctx_keys=n_c * ROW_TILE)
    kv = pl.BlockSpec((None, None, t, HEAD_DIM), lambda b, g, i: (b, g, 0, 0))
    w = GQA_REP * HEAD_DIM
    return _pcall(
        kern,
        grid=(bsz, GQA_KV_HEADS, n_t),
        in_specs=[pl.BlockSpec((None, GQA_REP, tq, HEAD_DIM), lambda b, g, i: (b, g, i, 0)), kv, kv,
                  pl.BlockSpec((1, w), lambda b, g, i: (0, g))],
        out_specs=pl.BlockSpec((tq, w), lambda b, g, i: (b * n_t + i, g)),
        out_shape=jax.ShapeDtypeStruct((bsz * t, GROUP_WIDTH), BF16),
        compiler_params=_cparams(("parallel", "parallel", "arbitrary")),
    )(q, k, v, beta)


def _diff_kernel(q_ref, k_ref, v_ref, lam_ref, g_ref, beta_ref, o_ref, *, tq, tk, n_chunks, n_ctx_tiles,
                 n_ctx_keys, out_scale):
    is_ctx = pl.program_id(2) < n_ctx_tiles
    q0, q1 = q_ref[0], q_ref[1]

    def scores(start, size):
        return jnp.concatenate([_nt_dot(q0, k_ref[0, pl.ds(start, size), :]),
                                _nt_dot(q1, k_ref[1, pl.ds(start, size), :])], axis=0)

    a = _softmax_chunks(scores, v_ref, 2 * tq, 2 * HEAD_DIM, n_chunks, tk, n_ctx_keys, is_ctx)
    o = a[:tq] - lam_ref[...] * a[tq:]
    o = _rms(o) * g_ref[...] * out_scale
    o_ref[...] = (o * beta_ref[...]).astype(o_ref.dtype)


def diff_attention(q, k, v, lam, subln_g, beta, out_scale, *, n_t, n_c, tk):
    bsz, _, t, _ = q.shape
    tq = ROW_TILE
    dv = 2 * HEAD_DIM
    kern = functools.partial(_diff_kernel, tq=tq, tk=tk, n_chunks=t // tk, n_ctx_tiles=n_c,
                             n_ctx_keys=n_c * ROW_TILE, out_scale=out_scale)
    return _pcall(
        kern,
        grid=(bsz, DIFF_HEADS, n_t),
        in_specs=[pl.BlockSpec((None, 2, tq, HEAD_DIM), lambda b, h, i: (b, h, i, 0)),
                  pl.BlockSpec((None, 2, t, HEAD_DIM), lambda b, h, i: (b, h, 0, 0)),
                  pl.BlockSpec((None, None, t, dv), lambda b, h, i: (b, h, 0, 0)),
                  pl.BlockSpec((1, 1), lambda b, h, i: (0, 0)),
                  pl.BlockSpec((1, dv), lambda b, h, i: (0, 0)),
                  pl.BlockSpec((1, dv), lambda b, h, i: (0, h))],
        out_specs=pl.BlockSpec((tq, dv), lambda b, h, i: (b * n_t + i, h)),
        out_shape=jax.ShapeDtypeStruct((bsz * t, GROUP_WIDTH), BF16),
        compiler_params=_cparams(("parallel", "parallel", "arbitrary")),
    )(q, k, v, lam, subln_g, beta)


def _ret_kernel(lg_ref, q_ref, k_ref, v_ref, o_ref, state_ref):
    d = pl.program_id(2)
    c = ROW_TILE

    @pl.when(pl.program_id(3) == 0)
    def _():
        state_ref[...] = jnp.zeros_like(state_ref)

    lg = lg_ref[...]
    fwd = d == 0
    q, k, v = q_ref[...], k_ref[...], v_ref[...]
    ii = lax.broadcasted_iota(jnp.int32, (c, c), 0)
    jj = lax.broadcasted_iota(jnp.int32, (c, c), 1)
    rel = jnp.where(fwd, ii - jj, jj - ii).astype(F32)
    decay = jnp.where(rel >= 0, jnp.exp(jnp.maximum(rel, 0.0) * lg), 0.0)
    pos = lax.broadcasted_iota(jnp.int32, (c, 1), 0).astype(F32)
    xi = jnp.exp(jnp.where(fwd, pos + 1.0, c - pos) * lg)
    zeta = jnp.exp(jnp.where(fwd, c - 1.0 - pos, pos) * lg)
    state = state_ref[...]
    scores = _nt_dot(q, k) * decay
    inner = jnp.dot(scores.astype(BF16), v, preferred_element_type=F32)
    cross = jnp.dot(q, state.astype(BF16), preferred_element_type=F32) * xi
    o_ref[...] = inner + cross
    kz = (k.astype(F32) * zeta).astype(BF16)
    upd = lax.dot_general(kz, v, (((0,), (0,)), ((), ())), preferred_element_type=F32)
    state_ref[...] = jnp.exp(c * lg) * state + upd


def retention(q, k, v, log_decay, *, n_t, n_c):
    bsz, nh, t, dk = q.shape
    dv = v.shape[-1]

    def blk(d, j):
        back = jnp.where(j < n_c, n_c - 1 - j, n_t - 1 - (j - n_c))
        return jnp.where(d == 0, j, back)

    qk = pl.BlockSpec((None, None, ROW_TILE, dk), lambda b, h, d, j: (b, h, blk(d, j), 0))
    return _pcall(
        _ret_kernel,
        grid=(bsz, nh, 2, n_t),
        in_specs=[pl.BlockSpec((None, 1, 1), lambda b, h, d, j: (d * RET_HEADS + h, 0, 0)), qk, qk,
                  pl.BlockSpec((None, None, ROW_TILE, dv), lambda b, h, d, j: (b, h, blk(d, j), 0))],
        out_specs=pl.BlockSpec((None, ROW_TILE, dv), lambda b, h, d, j: (d, b * n_t + blk(d, j), h)),
        out_shape=jax.ShapeDtypeStruct((2, bsz * t, nh * dv), F32),
        scratch_shapes=[pltpu.VMEM((dk, dv), F32)],
        compiler_params=_cparams(("parallel", "parallel", "arbitrary", "arbitrary")),
    )(log_decay.reshape(2 * RET_HEADS, 1, 1), q, k, v)


def _ret_gate_kernel(of_ref, ob_ref, g_ref, gn_ref, beta_ref, o_ref):
    o = of_ref[...] + ob_ref[...]
    mu = jnp.mean(o, axis=-1, keepdims=True)
    var = jnp.mean(jnp.square(o - mu), axis=-1, keepdims=True)
    y = (o - mu) * lax.rsqrt(var + NORM_EPS) * gn_ref[...]
    g = g_ref[...]
    o_ref[...] = (g * jax.nn.sigmoid(g) * y * beta_ref[...]).astype(o_ref.dtype)


def ret_gate(o2, p_attn, gn_g, beta):
    _, r, w = o2.shape
    dv = 2 * HEAD_DIM
    gate_col0 = COL_RET_GATE // dv
    vec = pl.BlockSpec((1, dv), lambda i, h: (0, h))
    return _pcall(
        _ret_gate_kernel,
        grid=(r // ROW_TILE, RET_HEADS),
        in_specs=[pl.BlockSpec((None, ROW_TILE, dv), lambda i, h: (0, i, h)),
                  pl.BlockSpec((None, ROW_TILE, dv), lambda i, h: (1, i, h)),
                  pl.BlockSpec((ROW_TILE, dv), lambda i, h: (i, gate_col0 + h)), vec, vec],
        out_specs=pl.BlockSpec((ROW_TILE, dv), lambda i, h: (i, h)),
        out_shape=jax.ShapeDtypeStruct((r, w), BF16),
        compiler_params=_cparams(("parallel", "parallel")),
    )(o2, o2, p_attn, gn_g, beta)


def _short_conv_kernel(cur_ref, prev_ref, next_ref, w_ref, b_ref, z_ref, x1_ref, x2_ref, *, n_seq_tiles):
    j = pl.program_id(1)

    @pl.when(j < n_seq_tiles)
    def _():
        u = cur_ref[...]
        rows = lax.broadcasted_iota(jnp.int32, u.shape, 0)
        prev_row = jnp.where(j == 0, 0.0, prev_ref[7:8, :])
        next_row = jnp.where(j == n_seq_tiles - 1, 0.0, next_ref[0:1, :])
        up = jnp.where(rows == 0, prev_row, pltpu.roll(u, 1, 0))
        un = jnp.where(rows == ROW_TILE - 1, next_row, pltpu.roll(u, ROW_TILE - 1, 0))
        y = up * w_ref[0:1, :] + u * w_ref[1:2, :] + un * w_ref[2:3, :] + b_ref[...]
        z_ref[...] = y[:, :HY_WIDTH]
        x1_ref[...] = y[:, HY_WIDTH:2 * HY_WIDTH]
        x2_ref[...] = y[:, 2 * HY_WIDTH:]

    @pl.when(j >= n_seq_tiles)
    def _():
        z_ref[...] = jnp.zeros_like(z_ref)
        x1_ref[...] = jnp.zeros_like(x1_ref)
        x2_ref[...] = jnp.zeros_like(x2_ref)


def short_conv(p_hy, w, b, *, bsz, n_t, first_tile, n_seq_tiles, n_pad_tiles):
    w3 = 3 * HY_WIDTH
    sub = ROW_TILE // 8
    last_blk8 = p_hy.shape[0] // 8 - 1

    def cur(bb, j):
        return (bb * n_t + first_tile + jnp.minimum(j, n_seq_tiles - 1), 0)

    def prev(bb, j):
        return (jnp.maximum(cur(bb, j)[0] * sub - 1, 0), 0)

    def nxt(bb, j):
        return (jnp.minimum((cur(bb, j)[0] + 1) * sub, last_blk8), 0)

    kern = functools.partial(_short_conv_kernel, n_seq_tiles=n_seq_tiles)
    rows = bsz * n_pad_tiles * ROW_TILE
    return _pcall(
        kern,
        grid=(bsz, n_pad_tiles),
        in_specs=[pl.BlockSpec((ROW_TILE, w3), cur), pl.BlockSpec((8, w3), prev), pl.BlockSpec((8, w3), nxt),
                  pl.BlockSpec((3, w3), lambda bb, j: (0, 0)), pl.BlockSpec((1, w3), lambda bb, j: (0, 0))],
        out_specs=[pl.BlockSpec((ROW_TILE, HY_WIDTH), lambda bb, j: (bb * n_pad_tiles + j, 0))] * 3,
        out_shape=[jax.ShapeDtypeStruct((rows, HY_WIDTH), F32)] * 3,
        compiler_params=_cparams(("parallel", "arbitrary")),
    )(p_hy, p_hy, p_hy, w, b.reshape(1, w3))


def _filter_kernel(feat_ref, w1_ref, b1_ref, w2_ref, b2_ref, w3_ref, win_ref, h_ref, asum_ref):
    i = pl.program_id(0)
    hp = lax.Precision.HIGHEST
    h = jnp.sin(jnp.dot(feat_ref[...], w1_ref[...], precision=hp, preferred_element_type=F32) + b1_ref[...])
    h = jnp.sin(jnp.dot(h, w2_ref[...], precision=hp, preferred_element_type=F32) + b2_ref[...])
    h = jnp.dot(h, w3_ref[...], precision=hp, preferred_element_type=F32)
    win = win_ref[...]
    h = h * jnp.concatenate([win] * (2 * HY_ORDER), axis=-1)
    h_ref[...] = h
    rows = lax.broadcasted_iota(jnp.int32, h.shape, 0) + i * ROW_TILE
    cols = lax.broadcasted_iota(jnp.int32, h.shape, 1)
    is_bwd = (cols // HY_WIDTH) % 2 == 1
    part = jnp.sum(jnp.where(is_bwd & (rows == 0), 0.0, jnp.abs(h)), axis=0, keepdims=True)

    @pl.when(i == 0)
    def _():
        asum_ref[...] = part

    @pl.when(i > 0)
    def _():
        asum_ref[...] += part


def hyena_filter_taps(length, w1, b1, w2, b2, w3):
    t = jnp.arange(length, dtype=F32)
    t_norm = t / length
    f = jnp.linspace(1e-4, HY_BANDS - 1, HY_BANDS, dtype=F32)
    wt = 2.0 * math.pi * t_norm
    feats = jnp.concatenate([t_norm[:, None], jnp.cos(wt[:, None] * f), -jnp.sin(wt[:, None] * f)], axis=-1)
    feats = jnp.pad(feats, ((0, 0), (0, 128 - HY_EMB_DIM)))
    w1p = jnp.pad(w1, ((0, 128 - HY_EMB_DIM), (0, 0)))
    deltas = jnp.abs(jnp.linspace(HY_MIN_DECAY, HY_MAX_DECAY, HY_WIDTH, dtype=F32))
    window = jnp.exp(-t_norm[:, None] * deltas[None])
    wout = HY_ORDER * 2 * HY_WIDTH
    full = lambda shp: pl.BlockSpec(shp, lambda i: (0, 0))
    return _pcall(
        _filter_kernel,
        grid=(length // ROW_TILE,),
        in_specs=[pl.BlockSpec((ROW_TILE, 128), lambda i: (i, 0)), full((128, HY_FILT_HIDDEN)),
                  full((1, HY_FILT_HIDDEN)), full((HY_FILT_HIDDEN, HY_FILT_HIDDEN)), full((1, HY_FILT_HIDDEN)),
                  full((HY_FILT_HIDDEN, wout)), pl.BlockSpec((ROW_TILE, HY_WIDTH), lambda i: (i, 0))],
        out_specs=[pl.BlockSpec((ROW_TILE, wout), lambda i: (i, 0)), full((1, wout))],
        out_shape=[jax.ShapeDtypeStruct((length, wout), F32), jax.ShapeDtypeStruct((1, wout), F32)],
        compiler_params=_cparams(("arbitrary",)),
    )(feats, w1p, b1.reshape(1, -1), w2, b2.reshape(1, -1), w3, window)


class _Dft:
    def __init__(self, n1):
        n2 = FFT_N2
        assert n1 % 16 == 0
        self.n1, self.n = n1, n1 * n2
        self.half = n1 // 2
        self.nf = n1 // 2 + 1
        self.nfp = -(-self.nf // 16) * 16
        f1 = np.arange(self.nf)[:, None]
        ang = 2.0 * np.pi * f1 * np.arange(n1)[None, :] / n1
        s1 = np.zeros((2 * self.nfp, n1))
        s1[:self.nf] = np.cos(ang)
        s1[self.nfp:self.nfp + self.nf] = -np.sin(ang)
        self.s1_full = s1
        k_data = max(self.half, 16)
        self.s1_data = np.zeros((2 * self.nfp, k_data))
        self.s1_data[:, :self.half] = s1[:, :self.half]
        tw = 2.0 * np.pi * f1 * np.arange(n2)[None, :] / self.n
        self.tw_cos = np.cos(tw)[:, :, None]
        self.tw_sin = np.sin(tw)[:, :, None]
        a2 = 2.0 * np.pi * np.arange(n2)[:, None] * np.arange(n2)[None, :] / n2
        wc, ws = np.cos(a2), np.sin(a2)
        self.m_fwd = np.block([[wc, ws], [-ws, wc]])
        self.m_inv = np.block([[wc, -ws], [ws, wc]])
        wgt = np.full(self.nf, 2.0)
        wgt[0] = 1.0
        wgt[-1] = 1.0
        ango = 2.0 * np.pi * np.arange(self.half)[:, None] * np.arange(self.nf)[None, :] / n1
        self.s4_re = np.zeros((self.half, self.nfp))
        self.s4_im = np.zeros((self.half, self.nfp))
        self.s4_re[:, :self.nf] = np.cos(ango) * wgt / self.n
        self.s4_im[:, :self.nf] = -np.sin(ango) * wgt / self.n

    @staticmethod
    def const(a, dtype=BF16):
        return jnp.asarray(a, F32).astype(dtype)


def _fft_s1_kernel(m_ref, x_ref, o_ref):
    m = m_ref[...]
    k_rows, n_sub = x_ref.shape[0], x_ref.shape[1]
    for s in range(n_sub):
        x = x_ref[:, s, :]
        if k_rows < m.shape[1]:
            x = jnp.concatenate([x, jnp.zeros((m.shape[1] - k_rows, x.shape[1]), x.dtype)], axis=0)
        o_ref[:, s, :] = jnp.dot(m, x.astype(BF16), preferred_element_type=F32)


def fft_stage1(mat, x3, *, n_batch, k_rows, n_sub=FFT_SUB):
    w = x3.shape[-1]
    m, k = mat.shape
    return _pcall(
        _fft_s1_kernel,
        grid=(n_batch, FFT_N2 // n_sub),
        in_specs=[pl.BlockSpec((m, k), lambda b, j: (0, 0)),
                  pl.BlockSpec((k_rows, n_sub, w), lambda b, j: (b, j, 0))],
        out_specs=pl.BlockSpec((None, m, n_sub, w), lambda b, j: (b, 0, j, 0)),
        out_shape=jax.ShapeDtypeStruct((n_batch, m, FFT_N2, w), F32),
        compiler_params=_cparams(("parallel", "parallel")),
    )(mat, x3)


def _twiddle(ar, ai, c, s):
    return ar * c + ai * s, ai * c - ar * s


def _fft_filter_mid_kernel(ar_ref, ai_ref, c_ref, s_ref, mf_ref, sc_ref, hb0_ref, kr_ref, ki_ref):
    br, bi = _twiddle(ar_ref[...], ai_ref[...], c_ref[...], s_ref[...])
    x = jnp.dot(mf_ref[...], jnp.concatenate([br, bi], axis=0).astype(BF16), preferred_element_type=F32)
    xr, xi = x[:FFT_N2], x[FFT_N2:]
    w = HY_WIDTH
    for o in range(HY_ORDER):
        fw = slice(2 * o * w, (2 * o + 1) * w)
        bw = slice((2 * o + 1) * w, (2 * o + 2) * w)
        oc = slice(o * w, (o + 1) * w)
        kr_ref[:, oc] = (xr[:, fw] + xr[:, bw] - hb0_ref[:, oc]) * sc_ref[:, oc]
        ki_ref[:, oc] = (xi[:, fw] - xi[:, bw]) * sc_ref[:, oc]


def fft_filter_mid(a, dft, kscale, hb0):
    cols = a.shape[-1]
    n2 = FFT_N2
    tw = pl.BlockSpec((None, n2, 1), lambda f: (f, 0, 0))
    blk = pl.BlockSpec((None, n2, cols // 2), lambda f: (f, 0, 0))
    vec = pl.BlockSpec((1, cols // 2), lambda f: (0, 0))
    return _pcall(
        _fft_filter_mid_kernel,
        grid=(dft.nf,),
        in_specs=[pl.BlockSpec((None, n2, cols), lambda f: (f, 0, 0)),
                  pl.BlockSpec((None, n2, cols), lambda f: (dft.nfp + f, 0, 0)), tw, tw,
                  pl.BlockSpec((2 * n2, 2 * n2), lambda f: (0, 0)), vec, vec],
        out_specs=[blk, blk],
        out_shape=[jax.ShapeDtypeStruct((dft.nf, n2, cols // 2), F32)] * 2,
        compiler_params=_cparams(("parallel",)),
    )(a, a, _Dft.const(dft.tw_cos, F32), _Dft.const(dft.tw_sin, F32), _Dft.const(dft.m_fwd), kscale, hb0)


def _fft_mid_kernel(ar_ref, ai_ref, c_ref, s_ref, mf_ref, mi_ref, kr_ref, ki_ref, er_ref, ei_ref, *, nf):
    @pl.when(pl.program_id(1) < nf)
    def _():
        c, s = c_ref[...], s_ref[...]
        br, bi = _twiddle(ar_ref[...], ai_ref[...], c, s)
        x = jnp.dot(mf_ref[...], jnp.concatenate([br, bi], axis=0).astype(BF16), preferred_element_type=F32)
        xr, xi = x[:FFT_N2], x[FFT_N2:]
        kr, ki = kr_ref[...], ki_ref[...]
        yr = xr * kr - xi * ki
        yi = xr * ki + xi * kr
        dd = jnp.dot(mi_ref[...], jnp.concatenate([yr, yi], axis=0).astype(BF16), preferred_element_type=F32)
        dr, di = dd[:FFT_N2], dd[FFT_N2:]
        er_ref[...] = dr * c - di * s
        ei_ref[...] = di * c + dr * s

    @pl.when(pl.program_id(1) >= nf)
    def _():
        er_ref[...] = jnp.zeros_like(er_ref)
        ei_ref[...] = jnp.zeros_like(ei_ref)


def fft_mid(a, dft, kr, ki, order):
    nb = a.shape[0]
    n2, w = FFT_N2, HY_WIDTH
    nf, nfp = dft.nf, dft.nfp
    fc = lambda f: jnp.minimum(f, nf - 1)
    tw = pl.BlockSpec((None, n2, 1), lambda b, f: (fc(f), 0, 0))
    mat = pl.BlockSpec((2 * n2, 2 * n2), lambda b, f: (0, 0))
    kf = pl.BlockSpec((None, n2, w), lambda b, f: (fc(f), 0, order))
    out = pl.BlockSpec((None, None, n2, w), lambda b, f: (b, f, 0, 0))
    return _pcall(
        functools.partial(_fft_mid_kernel, nf=nf),
        grid=(nb, nfp),
        in_specs=[pl.BlockSpec((None, None, n2, w), lambda b, f: (b, fc(f), 0, 0)),
                  pl.BlockSpec((None, None, n2, w), lambda b, f: (b, nfp + fc(f), 0, 0)),
                  tw, tw, mat, mat, kf, kf],
        out_specs=[out, out],
        out_shape=[jax.ShapeDtypeStruct((nb, nfp, n2, w), F32)] * 2,
        compiler_params=_cparams(("parallel", "arbitrary")),
    )(a, a, _Dft.const(dft.tw_cos, F32), _Dft.const(dft.tw_sin, F32), _Dft.const(dft.m_fwd),
      _Dft.const(dft.m_inv), kr, ki)


def _fft_s4_kernel(mr_ref, mi_ref, er_ref, ei_ref, gate_ref, z_ref, bias_ref, scale_ref, o_ref):
    mr, mi = mr_ref[...], mi_ref[...]
    bias, scale = bias_ref[...], scale_ref[...]
    for s in range(FFT_SUB):
        y = (jnp.dot(mr, er_ref[:, s, :].astype(BF16), preferred_element_type=F32)
             + jnp.dot(mi, ei_ref[:, s, :].astype(BF16), preferred_element_type=F32))
        o_ref[:, s, :] = (gate_ref[:, s, :] * (y + bias * z_ref[:, s, :]) * scale).astype(o_ref.dtype)


def fft_stage4(dft, er, ei, gate, z, bias_row, scale_row, *, out_dtype):
    nb = er.shape[0]
    m, nfp = dft.half, dft.nfp
    w = z.shape[-1]
    row = pl.BlockSpec((m, FFT_SUB, w), lambda b, j: (b, j, 0))
    vec = pl.BlockSpec((1, w), lambda b, j: (0, 0))
    mat = pl.BlockSpec((m, nfp), lambda b, j: (0, 0))
    spec = pl.BlockSpec((None, nfp, FFT_SUB, w), lambda b, j: (b, 0, j, 0))
    return _pcall(
        _fft_s4_kernel,
        grid=(nb, FFT_N2 // FFT_SUB),
        in_specs=[mat, mat, spec, spec, row, row, vec, vec],
        out_specs=row,
        out_shape=jax.ShapeDtypeStruct((nb * m, FFT_N2, w), out_dtype),
        compiler_params=_cparams(("parallel", "parallel")),
    )(_Dft.const(dft.s4_re), _Dft.const(dft.s4_im), er, ei, gate, z, bias_row, scale_row)


def hyena_long_conv(z, x1, x2, filt, hbias, beta_hy, *, bsz, length, n1):
    dft = _Dft(n1)
    n2, w = FFT_N2, HY_WIDTH
    half = dft.half
    assert half >= 8 and length % n2 == 0
    taps, asum = filt
    asum = asum.reshape(HY_ORDER, 2, w)
    kscale = (1.0 / (asum[:, 0] + asum[:, 1] + NORM_EPS)).reshape(1, HY_ORDER * w)
    hb0 = taps[0].reshape(HY_ORDER, 2, w)[:, 1].reshape(1, HY_ORDER * w)
    k_taps = length // n2
    s1_taps = np.zeros((2 * dft.nfp, max(k_taps, 16)))
    s1_taps[:, :k_taps] = dft.s1_full[:, :k_taps]
    a_f = fft_stage1(_Dft.const(s1_taps), taps.reshape(k_taps, n2, 2 * HY_ORDER * w), n_batch=1, k_rows=k_taps,
                     n_sub=8)
    kr, ki = fft_filter_mid(a_f[0], dft, kscale, hb0)
    s1 = _Dft.const(dft.s1_data)
    ones = jnp.ones((1, w), F32)
    shape3 = (bsz * half, n2, w)
    zc = z.reshape(shape3)
    for o, gate in enumerate((x1, x2)):
        a = fft_stage1(s1, zc, n_batch=bsz, k_rows=half)
        er, ei = fft_mid(a, dft, kr, ki, o)
        last = o == HY_ORDER - 1
        zc = fft_stage4(dft, er, ei, gate.reshape(shape3), zc, hbias[o].reshape(1, w),
                        beta_hy.reshape(1, w) if last else ones, out_dtype=BF16 if last else F32)
    return zc.reshape(bsz * half * n2, w)


def _rope_tables(n_ctx, n_lat):
    rows = n_lat // GRID_W
    row = jnp.repeat(jnp.arange(rows, dtype=F32), GRID_W)
    col = jnp.tile(jnp.arange(GRID_W, dtype=F32), rows)
    n_freq = HEAD_DIM // 4
    freqs = ROPE_THETA ** (-jnp.arange(n_freq, dtype=F32) / n_freq)
    ar = row[:, None] * freqs
    ac = col[:, None] * freqs
    cos = jnp.concatenate([jnp.cos(ar), jnp.cos(ar), jnp.cos(ac), jnp.cos(ac)], axis=-1)
    sin = jnp.concatenate([jnp.sin(ar), jnp.sin(ar), jnp.sin(ac), jnp.sin(ac)], axis=-1)
    cos = jnp.concatenate([jnp.ones((n_ctx, HEAD_DIM), F32), cos], axis=0)
    sin = jnp.concatenate([jnp.zeros((n_ctx, HEAD_DIM), F32), sin], axis=0)
    even = (np.arange(HEAD_DIM) // n_freq) % 2 == 0
    sin_a = jnp.where(even, -sin, 0.0)
    sin_b = jnp.where(even, 0.0, sin)
    tile2 = lambda a: jnp.concatenate([a, a], axis=-1)
    return tile2(cos), tile2(sin_a), tile2(sin_b)


def _attn_key_chunk(t):
    for tk in (1408, 1280, 1024, 768, 512, 256):
        if t % tk == 0:
            return tk
    raise ValueError(t)


def kernel(x, c, ctx, c_ctx, ada_w, ada_b, norm_g, w_in, w_out, mix_beta, gqa_qk_g, diff_lambda, diff_subln_g,
           ret_log_decay, ret_gn_g, hy_short_w, hy_short_b, hy_filt_w1, hy_filt_b1, hy_filt_w2, hy_filt_b2,
           hy_filt_w3, hy_bias, ffn_w_gate, ffn_w_up, ffn_w_down, moe_router, moe_w_gate, moe_w_up, moe_w_down):
    bsz, n_lat, d = x.shape
    n_ctx = ctx.shape[1]
    t = n_ctx + n_lat
    r = bsz * t
    n_t, n_c = t // ROW_TILE, n_ctx // ROW_TILE
    n_s = n_lat // ROW_TILE
    assert n_ctx % ROW_TILE == 0 and n_lat % ROW_TILE == 0 and r % 512 == 0
    tk = _attn_key_chunk(t)
    assert tk >= n_ctx

    rope_c, rope_sa, rope_sb = _rope_tables(n_ctx, n_lat)
    xs = jnp.concatenate([ctx, x], axis=1).reshape(r, d)

    cvec = jnp.stack([jnp.broadcast_to(c_ctx, c.shape), c], axis=1).reshape(2 * bsz, d)
    cvec = jax.nn.silu(cvec)
    cvec = jnp.pad(cvec, ((0, 16 - 2 * bsz), (0, 0))).astype(BF16)

    n1_lat = 2 * n_lat // FFT_N2
    ctx_pad_tiles = max(n_c, 1024 // ROW_TILE)
    n1_ctx = 2 * ctx_pad_tiles * ROW_TILE // FFT_N2

    for l in range(DEPTH):
        lambda_init = 0.8 - 0.6 * math.exp(-0.3 * l)
        mod = mm(cvec, ada_w, tm=16, tn=1536, out_dtype=F32, lead=(l,))[:2 * bsz] + ada_b[l]
        mod = mod.reshape(2 * bsz, 6, 1, d)
        sh_m, sc_m, gt_m, sh_f, sc_f, gt_f = (mod[:, k] for k in range(6))
        beta = mix_beta[l].reshape(1, 4 * GROUP_WIDTH)
        beta_a, beta_b, beta_r, beta_d = (beta[:, GROUP_WIDTH * k:GROUP_WIDTH * (k + 1)] for k in range(4))

        (h,) = norm_mod(xs, norm_g[l, 0], sh_m, sc_m, n_t, n_c, (BF16,))
        p_attn = mm(h, w_in, tm=512, tn=768, out_dtype=F32, lead=(l,), n_cols=COL_ATTN)
        p_hy = mm(h, w_in, tm=512, tn=768, out_dtype=F32, lead=(l,), col_off=COL_HY // 768,
                  n_cols=IN_WIDTH - COL_HY)
        aq, ak, av, dq, dk, dv, rq, rk, rv = prep_heads(p_attn, rope_c, rope_sa, rope_sb, gqa_qk_g[l], bsz, n_t)

        a_out = gqa_attention(aq, ak, av, beta_a, n_t=n_t, n_c=n_c, tk=tk)

        lamf = diff_lambda[l]
        lam_full = (jnp.exp(jnp.sum(lamf[0] * lamf[1])) - jnp.exp(jnp.sum(lamf[2] * lamf[3])) + lambda_init)
        b_out = diff_attention(dq, dk, dv, lam_full.reshape(1, 1), diff_subln_g[l].reshape(1, -1), beta_b,
                               1.0 - lambda_init, n_t=n_t, n_c=n_c, tk=tk)

        o2 = retention(rq, rk, rv, ret_log_decay[l], n_t=n_t, n_c=n_c)
        r_out = ret_gate(o2, p_attn, ret_gn_g[l].reshape(1, -1), beta_r)

        filt_args = (hy_filt_w1[l], hy_filt_b1[l], hy_filt_w2[l], hy_filt_b2[l], hy_filt_w3[l])
        z_l, x1_l, x2_l = short_conv(p_hy, hy_short_w[l], hy_short_b[l], bsz=bsz, n_t=n_t, first_tile=n_c,
                                     n_seq_tiles=n_s, n_pad_tiles=n_s)
        d_l = hyena_long_conv(z_l, x1_l, x2_l, hyena_filter_taps(n_lat, *filt_args), hy_bias[l], beta_d[0],
                              bsz=bsz, length=n_lat, n1=n1_lat)
        z_c, x1_c, x2_c = short_conv(p_hy, hy_short_w[l], hy_short_b[l], bsz=bsz, n_t=n_t, first_tile=0,
                                     n_seq_tiles=n_c, n_pad_tiles=ctx_pad_tiles)
        d_c = hyena_long_conv(z_c, x1_c, x2_c, hyena_filter_taps(n_ctx, *filt_args), hy_bias[l], beta_d[0],
                              bsz=bsz, length=n_ctx, n1=n1_ctx)
        d_out = jnp.concatenate([d_c.reshape(bsz, -1, HY_WIDTH)[:, :n_ctx], d_l.reshape(bsz, n_lat, HY_WIDTH)],
                                axis=1).reshape(r, HY_WIDTH)

        mixed = jnp.concatenate([a_out, b_out, r_out, d_out], axis=-1)
        y = mm(mixed, w_out, tm=512, tn=512, out_dtype=F32, lead=(l,))
        xs = gate_res(xs, y, norm_g[l, 1], gt_m, n_t, n_c)

        i = l // 2
        if l % 2 == 0:
            (f,) = norm_mod(xs, norm_g[l, 2], sh_f, sc_f, n_t, n_c, (BF16,))
            hid = swiglu_up(f, ffn_w_gate, ffn_w_up, tm=512, tn=512, lead=(i,))
            f_out = mm(hid, ffn_w_down, tm=512, tn=256, out_dtype=F32, lead=(i,))
            xs = gate_res(xs, f_out, norm_g[l, 3], gt_f, n_t, n_c)
        else:
            (f_in,) = norm_mod(xs, norm_g[l, 2], sh_f, sc_f, n_t, n_c, (F32,))
            top_i, top_w = router(f_in, moe_router[i].T)
            n_pad_rows = (2 * r // MOE_TM + N_EXPERTS) * MOE_TM
            row_token, row_weight, tile_expert, n_used, pos = route_tokens(top_i, top_w, n_pad_rows)
            y_sorted = moe_experts(f_in, moe_w_gate, moe_w_up, moe_w_down, i, row_token, row_weight,
                                   tile_expert, n_used)
            xs = moe_combine_gate_res(xs, y_sorted, pos, norm_g[l, 3], gt_f, n_t, n_c)

    return xs.reshape(bsz, t, d)[:, n_ctx:]
```

```python
import functools
import math

import numpy as np
import jax
import jax.numpy as jnp
from jax import lax
from jax.experimental import pallas as pl
from jax.experimental.pallas import tpu as pltpu

F32 = jnp.float32
BF16 = jnp.bfloat16

D_MODEL = 2048
DEPTH = 4
GRID_W = 64
HEAD_DIM = 64
ROPE_THETA = 10000.0
NORM_EPS = 1e-6
GROUP_WIDTH = 512
GQA_Q_HEADS = 8
GQA_KV_HEADS = 2
GQA_REP = 4
DIFF_HEADS = 4
RET_HEADS = 4
HY_WIDTH = 512
HY_ORDER = 2
HY_EMB_DIM = 33
HY_BANDS = 16
HY_FILT_HIDDEN = 64
HY_MIN_DECAY = math.log(1e-2) / 1.5
HY_MAX_DECAY = math.log(1e-2) / 0.3
N_EXPERTS = 8

COL_ATTN = 3840
COL_DIFF = 768
COL_RET = 2304
COL_RET_GATE = 3328
COL_HY = 3840
IN_WIDTH = 5376

ROW_TILE = 256
MOE_TM = 1024
MOE_TF = 256
FFT_N2 = 128
FFT_SUB = 8
VMEM_LIMIT_MB = 48


def _cparams(sem, vmem_mb=VMEM_LIMIT_MB):
    return pltpu.CompilerParams(dimension_semantics=sem, vmem_limit_bytes=vmem_mb * 1024 * 1024)


def _pcall(kern, **kw):
    fn = kern.func if isinstance(kern, functools.partial) else kern
    return pl.pallas_call(kern, name=fn.__name__.strip("_"), **kw)


def _mm_kernel(a_ref, b_ref, o_ref):
    o_ref[...] = jnp.dot(a_ref[...].astype(BF16), b_ref[...].astype(BF16),
                         preferred_element_type=F32).astype(o_ref.dtype)


def _b_spec(b, lead, k, tn, col_off):
    nlead = len(lead)
    return pl.BlockSpec((None,) * nlead + (k, tn), lambda j, i: tuple(lead) + (0, j + col_off))


def mm(a, b, *, tm, tn, out_dtype, lead=(), col_off=0, n_cols=None):
    m, k = a.shape
    n = b.shape[-1] if n_cols is None else n_cols
    assert m % tm == 0 and n % tn == 0, (m, tm, n, tn)
    return _pcall(
        _mm_kernel,
        grid=(n // tn, m // tm),
        in_specs=[pl.BlockSpec((tm, k), lambda j, i: (i, 0)), _b_spec(b, lead, k, tn, col_off)],
        out_specs=pl.BlockSpec((tm, tn), lambda j, i: (i, j)),
        out_shape=jax.ShapeDtypeStruct((m, n), out_dtype),
        compiler_params=_cparams(("parallel", "parallel")),
    )(a, b)


def _swiglu_up_kernel(a_ref, wg_ref, wu_ref, o_ref):
    a = a_ref[...]
    g = jnp.dot(a, wg_ref[...].astype(BF16), preferred_element_type=F32)
    u = jnp.dot(a, wu_ref[...].astype(BF16), preferred_element_type=F32)
    o_ref[...] = (g * jax.nn.sigmoid(g) * u).astype(o_ref.dtype)


def swiglu_up(a, wg, wu, *, tm, tn, lead):
    m, k = a.shape
    n = wg.shape[-1]
    assert m % tm == 0 and n % tn == 0
    return _pcall(
        _swiglu_up_kernel,
        grid=(n // tn, m // tm),
        in_specs=[pl.BlockSpec((tm, k), lambda j, i: (i, 0)),
                  _b_spec(wg, lead, k, tn, 0), _b_spec(wu, lead, k, tn, 0)],
        out_specs=pl.BlockSpec((tm, tn), lambda j, i: (i, j)),
        out_shape=jax.ShapeDtypeStruct((m, n), BF16),
        compiler_params=_cparams(("parallel", "parallel")),
    )(a, wg, wu)


def _seg_map(n_t, n_c):
    return lambda i: (2 * (i // n_t) + jnp.where(i % n_t >= n_c, 1, 0), 0, 0)


def _rms(x):
    return x * lax.rsqrt(jnp.mean(x * x, axis=-1, keepdims=True) + NORM_EPS)


def _norm_mod_kernel(x_ref, g_ref, sh_ref, sc_ref, *o_refs):
    y = _rms(x_ref[...]) * g_ref[...]
    y = y * (1.0 + sc_ref[...]) + sh_ref[...]
    for o_ref in o_refs:
        o_ref[...] = y.astype(o_ref.dtype)


def norm_mod(x, g, shift, scale, n_t, n_c, out_dtypes):
    r, d = x.shape
    seg = _seg_map(n_t, n_c)
    row = pl.BlockSpec((ROW_TILE, d), lambda i: (i, 0))
    outs = _pcall(
        _norm_mod_kernel,
        grid=(r // ROW_TILE,),
        in_specs=[row, pl.BlockSpec((1, d), lambda i: (0, 0)),
                  pl.BlockSpec((None, 1, d), seg), pl.BlockSpec((None, 1, d), seg)],
        out_specs=[row] * len(out_dtypes),
        out_shape=[jax.ShapeDtypeStruct((r, d), dt) for dt in out_dtypes],
        compiler_params=_cparams(("parallel",)),
    )(x, g.reshape(1, d), shift, scale)
    return outs


def _gate_res_kernel(x_ref, y_ref, g_ref, gt_ref, o_ref):
    o_ref[...] = x_ref[...] + gt_ref[...] * (_rms(y_ref[...]) * g_ref[...])


def gate_res(x, y, g, gate, n_t, n_c):
    r, d = x.shape
    row = pl.BlockSpec((ROW_TILE, d), lambda i: (i, 0))
    return _pcall(
        _gate_res_kernel,
        grid=(r // ROW_TILE,),
        in_specs=[row, row, pl.BlockSpec((1, d), lambda i: (0, 0)),
                  pl.BlockSpec((None, 1, d), _seg_map(n_t, n_c))],
        out_specs=row,
        out_shape=jax.ShapeDtypeStruct((r, d), F32),
        compiler_params=_cparams(("parallel",)),
    )(x, y, g.reshape(1, d), gate)


def _router_kernel(f_ref, rt_ref, idx_ref, w_ref):
    lt = lax.dot_general(rt_ref[...], f_ref[...], (((1,), (1,)), ((), ())),
                         precision=lax.Precision.HIGHEST, preferred_element_type=F32)
    e_idx = lax.broadcasted_iota(jnp.int32, lt.shape, 0)
    m1 = jnp.max(lt, axis=0, keepdims=True)
    i1 = jnp.min(jnp.where(lt == m1, e_idx, N_EXPERTS), axis=0, keepdims=True)
    rest = jnp.where(e_idx == i1, -jnp.inf, lt)
    m2 = jnp.max(rest, axis=0, keepdims=True)
    i2 = jnp.min(jnp.where(rest == m2, e_idx, N_EXPERTS), axis=0, keepdims=True)
    e2 = jnp.exp(m2 - m1)
    idx_ref[...] = jnp.concatenate([i1, i2], axis=0)
    w_ref[...] = jnp.concatenate([1.0 / (1.0 + e2), e2 / (1.0 + e2)], axis=0)


def router(f, router_t):
    r, d = f.shape
    return _pcall(
        _router_kernel,
        grid=(r // ROW_TILE,),
        in_specs=[pl.BlockSpec((ROW_TILE, d), lambda i: (i, 0)),
                  pl.BlockSpec((N_EXPERTS, d), lambda i: (0, 0))],
        out_specs=[pl.BlockSpec((2, ROW_TILE), lambda i: (0, i))] * 2,
        out_shape=[jax.ShapeDtypeStruct((2, r), jnp.int32), jax.ShapeDtypeStruct((2, r), F32)],
        compiler_params=_cparams(("parallel",)),
    )(f, router_t)


def _row_copy(src_hbm, dst, src_row, dst_row, sem):
    return pltpu.make_async_copy(src_hbm.at[pl.ds(src_row, 1), :], dst.at[pl.ds(dst_row, 1), :], sem)


def _gather_rows(src_hbm, dst, idx_ref, base, n_rows, sem):
    def start(r, carry):
        _row_copy(src_hbm, dst, idx_ref[base + r], r, sem).start()
        return carry

    def wait(r, carry):
        _row_copy(src_hbm, dst, idx_ref[base + r], r, sem).wait()
        return carry

    lax.fori_loop(0, n_rows, start, 0)
    lax.fori_loop(0, n_rows, wait, 0)


def route_tokens(idx, w, n_pad_rows):
    r = idx.shape[1]
    n_assign = 2 * r
    n_tiles = n_pad_rows // MOE_TM
    e_flat = idx.reshape(n_assign)
    w_flat = w.reshape(n_assign)
    counts = jnp.sum((e_flat[:, None] == jnp.arange(N_EXPERTS)[None, :]).astype(jnp.int32), axis=0)
    padded = (counts + MOE_TM - 1) // MOE_TM * MOE_TM
    group_end = jnp.cumsum(padded)
    group_start = group_end - padded
    sorted_start = jnp.cumsum(counts) - counts
    order = jnp.argsort(e_flat, stable=True).astype(jnp.int32)
    inv = jnp.argsort(order).astype(jnp.int32)
    pos = group_start[e_flat] + (inv - sorted_start[e_flat])
    rows = jnp.arange(n_pad_rows, dtype=jnp.int32)
    row_e = jnp.minimum(jnp.sum((rows[:, None] >= group_end[None, :]).astype(jnp.int32), axis=1), N_EXPERTS - 1)
    rank = rows - group_start[row_e]
    valid = rank < counts[row_e]
    src = order[jnp.clip(sorted_start[row_e] + rank, 0, n_assign - 1)]
    row_token = jnp.where(valid, src % r, 0)
    row_weight = jnp.where(valid, w_flat[src], 0.0)
    n_used = group_end[-1] // MOE_TM
    tile_start = jnp.arange(n_tiles, dtype=jnp.int32) * MOE_TM
    tile_start = jnp.minimum(tile_start, group_end[-1] - 1)
    tile_expert = jnp.sum((tile_start[:, None] >= group_end[None, :]).astype(jnp.int32), axis=1)
    return (row_token, row_weight.reshape(n_pad_rows, 1), tile_expert.astype(jnp.int32),
            n_used.astype(jnp.int32).reshape(1), pos)


def _moe_expert_kernel(te_ref, nu_ref, tok_ref, f_hbm, wg_ref, wu_ref, wd_ref, rw_ref, o_ref, xb_ref, sem):
    i, f = pl.program_id(0), pl.program_id(1)
    used = i < nu_ref[0]

    @pl.when(jnp.logical_and(used, f == 0))
    def _():
        _gather_rows(f_hbm, o_ref, tok_ref, i * MOE_TM, MOE_TM, sem)
        xb_ref[...] = o_ref[...].astype(BF16)

    @pl.when(f == 0)
    def _():
        o_ref[...] = jnp.zeros_like(o_ref)

    @pl.when(used)
    def _():
        x = xb_ref[...]
        g = jnp.dot(x, wg_ref[...].astype(BF16), preferred_element_type=F32)
        u = jnp.dot(x, wu_ref[...].astype(BF16), preferred_element_type=F32)
        h = (g * jax.nn.sigmoid(g) * u).astype(BF16)
        o_ref[...] += jnp.dot(h, wd_ref[...].astype(BF16), preferred_element_type=F32)

    @pl.when(jnp.logical_and(used, f == pl.num_programs(1) - 1))
    def _():
        o_ref[...] = o_ref[...] * rw_ref[...]


def moe_experts(f_in, wg, wu, wd, layer, row_token, row_weight, tile_expert, n_used):
    r, d = f_in.shape
    ff = wg.shape[-1]
    n_pad_rows = row_token.shape[0]
    nt, nf = n_pad_rows // MOE_TM, ff // MOE_TF

    def fi(i, f, nu):
        return jnp.where(i < nu[0], f, nf - 1)

    grid_spec = pltpu.PrefetchScalarGridSpec(
        num_scalar_prefetch=3,
        grid=(nt, nf),
        in_specs=[pl.BlockSpec(memory_space=pl.ANY),
                  pl.BlockSpec((None, None, d, MOE_TF), lambda i, f, te, nu, tok: (layer, te[i], 0, fi(i, f, nu))),
                  pl.BlockSpec((None, None, d, MOE_TF), lambda i, f, te, nu, tok: (layer, te[i], 0, fi(i, f, nu))),
                  pl.BlockSpec((None, None, MOE_TF, d), lambda i, f, te, nu, tok: (layer, te[i], fi(i, f, nu), 0)),
                  pl.BlockSpec((MOE_TM, 1), lambda i, f, te, nu, tok: (i, 0))],
        out_specs=pl.BlockSpec((MOE_TM, d), lambda i, f, te, nu, tok: (i, 0)),
        scratch_shapes=[pltpu.VMEM((MOE_TM, d), BF16), pltpu.SemaphoreType.DMA(())],
    )
    return _pcall(
        _moe_expert_kernel,
        grid_spec=grid_spec,
        out_shape=jax.ShapeDtypeStruct((n_pad_rows, d), F32),
        compiler_params=_cparams(("arbitrary", "arbitrary")),
    )(tile_expert, n_used, row_token, f_in, wg, wu, wd, row_weight)


def _moe_combine_kernel(pos_ref, y_hbm, x_ref, g_ref, gt_ref, o_ref, buf_ref, sem):
    i = pl.program_id(0)
    for k in range(2):
        _gather_rows(y_hbm, buf_ref.at[k], pos_ref, (2 * i + k) * ROW_TILE, ROW_TILE, sem)
    y = buf_ref[0] + buf_ref[1]
    o_ref[...] = x_ref[...] + gt_ref[...] * (_rms(y) * g_ref[...])


def moe_combine_gate_res(x, y_sorted, pos, g, gate, n_t, n_c):
    r, d = x.shape
    nrt = r // ROW_TILE
    pos_tiles = pos.reshape(2, nrt, ROW_TILE).transpose(1, 0, 2).reshape(2 * r)
    seg = _seg_map(n_t, n_c)
    row = pl.BlockSpec((ROW_TILE, d), lambda i, p: (i, 0))
    grid_spec = pltpu.PrefetchScalarGridSpec(
        num_scalar_prefetch=1,
        grid=(nrt,),
        in_specs=[pl.BlockSpec(memory_space=pl.ANY), row, pl.BlockSpec((1, d), lambda i, p: (0, 0)),
                  pl.BlockSpec((None, 1, d), lambda i, p: seg(i))],
        out_specs=row,
        scratch_shapes=[pltpu.VMEM((2, ROW_TILE, d), F32), pltpu.SemaphoreType.DMA(())],
    )
    return _pcall(
        _moe_combine_kernel,
        grid_spec=grid_spec,
        out_shape=jax.ShapeDtypeStruct((r, d), F32),
        compiler_params=_cparams(("arbitrary",)),
    )(pos_tiles, y_sorted, x, g.reshape(1, d), gate)


def _rope(x, c, sa, sb):
    return x * c + pltpu.roll(x, 112, 1) * sa + pltpu.roll(x, 16, 1) * sb


def _head_rms(x, g, bd):
    sq = x * x
    hi = sq.astype(BF16)
    lo = (sq - hi.astype(F32)).astype(BF16)
    ms = jnp.dot(hi, bd, preferred_element_type=F32) + jnp.dot(lo, bd, preferred_element_type=F32)
    return x * lax.rsqrt(ms + NORM_EPS) * g


def _prep_kernel(p_ref, c_ref, sa_ref, sb_ref, gq_ref, gk_ref, bd_ref,
                 aq_ref, ak_ref, av_ref, dq_ref, dk_ref, dv_ref, rq_ref, rk_ref, rv_ref):
    c, sa, sb = c_ref[...], sa_ref[...], sb_ref[...]
    bd = bd_ref[...]
    scale = HEAD_DIM ** -0.5
    qscale = scale * math.log2(math.e)

    def chunk(j):
        return p_ref[:, 128 * j:128 * (j + 1)]

    def put_heads(ref, first, val):
        ref[first] = val[:, :HEAD_DIM].astype(ref.dtype)
        ref[first + 1] = val[:, HEAD_DIM:].astype(ref.dtype)

    for j in range(4):
        put_heads(aq_ref, 2 * j, _rope(_head_rms(chunk(j), gq_ref[...], bd), c, sa, sb) * qscale)
    put_heads(ak_ref, 0, _rope(_head_rms(chunk(4), gk_ref[...], bd), c, sa, sb))
    one64 = (lax.broadcasted_iota(jnp.int32, (ROW_TILE, HEAD_DIM), 1) == 0).astype(av_ref.dtype)
    one128 = (lax.broadcasted_iota(jnp.int32, (ROW_TILE, 2 * HEAD_DIM), 1) == 0).astype(dv_ref.dtype)
    v_gqa = chunk(5)
    for h in range(GQA_KV_HEADS):
        av_ref[h, :, :HEAD_DIM] = v_gqa[:, HEAD_DIM * h:HEAD_DIM * (h + 1)].astype(av_ref.dtype)
        av_ref[h, :, HEAD_DIM:] = one64
    for j in range(4):
        put_heads(dq_ref, 2 * j, _rope(chunk(6 + j), c, sa, sb) * qscale)
        put_heads(dk_ref, 2 * j, _rope(chunk(10 + j), c, sa, sb))
        dv_ref[j, :, :2 * HEAD_DIM] = chunk(14 + j).astype(dv_ref.dtype)
        dv_ref[j, :, 2 * HEAD_DIM:] = one128
    for j in range(2):
        put_heads(rq_ref, 2 * j, _rope(chunk(18 + j), c, sa, sb))
        put_heads(rk_ref, 2 * j, _rope(chunk(20 + j), c, sa, sb) * scale)
    for j in range(4):
        rv_ref[j] = chunk(22 + j).astype(rv_ref.dtype)


def prep_heads(p_attn, rope_c, rope_sa, rope_sb, qk_g, bsz, n_t):
    t = n_t * ROW_TILE
    bd = jnp.asarray(np.kron(np.eye(2), np.full((HEAD_DIM, HEAD_DIM), 1.0 / HEAD_DIM)), F32).astype(BF16)
    gq = jnp.tile(qk_g[0], 2).reshape(1, 128)
    gk = jnp.tile(qk_g[1], 2).reshape(1, 128)
    tab = pl.BlockSpec((ROW_TILE, 128), lambda i: (i % n_t, 0))
    vec = pl.BlockSpec((1, 128), lambda i: (0, 0))

    def heads(nh, dh):
        return (pl.BlockSpec((None, nh, ROW_TILE, dh), lambda i: (i // n_t, 0, i % n_t, 0)),
                jax.ShapeDtypeStruct((bsz, nh, t, dh), BF16))

    outs = [heads(8, 64), heads(2, 64), heads(2, 128), heads(8, 64), heads(8, 64), heads(4, 256),
            heads(4, 64), heads(4, 64), heads(4, 128)]
    return _pcall(
        _prep_kernel,
        grid=(bsz * n_t,),
        in_specs=[pl.BlockSpec((ROW_TILE, COL_RET_GATE), lambda i: (i, 0)), tab, tab, tab, vec, vec,
                  pl.BlockSpec((128, 128), lambda i: (0, 0))],
        out_specs=[o[0] for o in outs],
        out_shape=[o[1] for o in outs],
        compiler_params=_cparams(("parallel",)),
    )(p_attn, rope_c, rope_sa, rope_sb, gq, gk, bd)


def _nt_dot(a, b):
    return lax.dot_general(a, b, (((1,), (1,)), ((), ())), preferred_element_type=F32)


def _softmax_chunks(score_fn, v_ref, s_ref, rows, dv, n_chunks, tk, n_ctx_keys, is_ctx):
    def update(s, start, carry):
        m, acc = carry
        m_new = jnp.maximum(m, jnp.max(s, axis=-1, keepdims=True))
        p = jnp.exp2(s - m_new)
        acc = jnp.exp2(m - m_new) * acc + jnp.dot(p.astype(BF16), v_ref[pl.ds(start, tk), :],
                                                  preferred_element_type=F32)
        return m_new, acc

    init = (jnp.full((rows, 1), -1e30, F32), jnp.zeros((rows, v_ref.shape[-1]), F32))

    def ctx_tile():
        s = score_fn(0, tk)
        col = lax.broadcasted_iota(jnp.int32, s.shape, 1)
        return update(jnp.where(col < n_ctx_keys, s, -1e30), 0, init)

    def lat_tile():
        stats = init
        s_ref[0] = score_fn(0, tk)
        for j in range(n_chunks):
            if j + 1 < n_chunks:
                s_ref[(j + 1) % 2] = score_fn((j + 1) * tk, tk)
            stats = update(s_ref[j % 2], j * tk, stats)
        return stats

    _, acc = lax.cond(is_ctx, ctx_tile, lat_tile)
    return acc[:, :dv] / acc[:, dv:dv + 1]


def _gqa_kernel(q_ref, k_ref, v_ref, beta_ref, o_ref, s_ref, *, tq, tk, n_chunks, n_ctx_tiles, n_ctx_keys):
    is_ctx = pl.program_id(2) < n_ctx_tiles
    q = q_ref[...].reshape(GQA_REP * tq, HEAD_DIM)

    def scores(start, size):
        return _nt_dot(q, k_ref[pl.ds(start, size), :])

    o = _softmax_chunks(scores, v_ref, s_ref, GQA_REP * tq, HEAD_DIM, n_chunks, tk, n_ctx_keys, is_ctx)
    for r in range(GQA_REP):
        sl = slice(HEAD_DIM * r, HEAD_DIM * (r + 1))
        o_ref[:, sl] = (o[r * tq:(r + 1) * tq] * beta_ref[:, sl]).astype(o_ref.dtype)


def gqa_attention(q, k, v, beta, *, n_t, n_c, tk):
    bsz, _, t, _ = q.shape
    tq = ROW_TILE
    kern = functools.partial(_gqa_kernel, tq=tq, tk=tk, n_chunks=t // tk, n_ctx_tiles=n_c,
                             n_ctx_keys=n_c * ROW_TILE)
    kspec = pl.BlockSpec((None, None, t, HEAD_DIM), lambda b, g, i: (b, g, 0, 0))
    vspec = pl.BlockSpec((None, None, t, v.shape[-1]), lambda b, g, i: (b, g, 0, 0))
    w = GQA_REP * HEAD_DIM
    return _pcall(
        kern,
        grid=(bsz, GQA_KV_HEADS, n_t),
        in_specs=[pl.BlockSpec((None, GQA_REP, tq, HEAD_DIM), lambda b, g, i: (b, g, i, 0)), kspec, vspec,
                  pl.BlockSpec((1, w), lambda b, g, i: (0, g))],
        out_specs=pl.BlockSpec((tq, w), lambda b, g, i: (b * n_t + i, g)),
        out_shape=jax.ShapeDtypeStruct((bsz * t, GROUP_WIDTH), BF16),
        scratch_shapes=[pltpu.VMEM((2, GQA_REP * tq, tk), F32)],
        compiler_params=_cparams(("parallel", "parallel", "arbitrary")),
    )(q, k, v, beta)


def _diff_kernel(q_ref, k_ref, v_ref, lam_ref, g_ref, beta_ref, o_ref, s_ref, *, tq, tk, n_chunks, n_ctx_tiles,
                 n_ctx_keys, out_scale):
    is_ctx = pl.program_id(2) < n_ctx_tiles
    q0, q1 = q_ref[0], q_ref[1]

    def scores(start, size):
        return jnp.concatenate([_nt_dot(q0, k_ref[0, pl.ds(start, size), :]),
                                _nt_dot(q1, k_ref[1, pl.ds(start, size), :])], axis=0)

    a = _softmax_chunks(scores, v_ref, s_ref, 2 * tq, 2 * HEAD_DIM, n_chunks, tk, n_ctx_keys, is_ctx)
    o = a[:tq] - lam_ref[...] * a[tq:]
    o = _rms(o) * g_ref[...] * out_scale
    o_ref[...] = (o * beta_ref[...]).astype(o_ref.dtype)


def diff_attention(q, k, v, lam, subln_g, beta, out_scale, *, n_t, n_c, tk):
    bsz, _, t, _ = q.shape
    tq = ROW_TILE
    dv = 2 * HEAD_DIM
    kern = functools.partial(_diff_kernel, tq=tq, tk=tk, n_chunks=t // tk, n_ctx_tiles=n_c,
                             n_ctx_keys=n_c * ROW_TILE, out_scale=out_scale)
    return _pcall(
        kern,
        grid=(bsz, DIFF_HEADS, n_t),
        in_specs=[pl.BlockSpec((None, 2, tq, HEAD_DIM), lambda b, h, i: (b, h, i, 0)),
                  pl.BlockSpec((None, 2, t, HEAD_DIM), lambda b, h, i: (b, h, 0, 0)),
                  pl.BlockSpec((None, None, t, v.shape[-1]), lambda b, h, i: (b, h, 0, 0)),
                  pl.BlockSpec((1, 1), lambda b, h, i: (0, 0)),
                  pl.BlockSpec((1, dv), lambda b, h, i: (0, 0)),
                  pl.BlockSpec((1, dv), lambda b, h, i: (0, h))],
        out_specs=pl.BlockSpec((tq, dv), lambda b, h, i: (b * n_t + i, h)),
        out_shape=jax.ShapeDtypeStruct((bsz * t, GROUP_WIDTH), BF16),
        scratch_shapes=[pltpu.VMEM((2, 2 * tq, tk), F32)],
        compiler_params=_cparams(("parallel", "parallel", "arbitrary")),
    )(q, k, v, lam, subln_g, beta)


def _ret_kernel(lg_ref, q_ref, k_ref, v_ref, o_ref, state_ref):
    d = pl.program_id(2)
    c = ROW_TILE

    @pl.when(pl.program_id(3) == 0)
    def _():
        state_ref[...] = jnp.zeros_like(state_ref)

    lg = lg_ref[...]
    fwd = d == 0
    q, k, v = q_ref[...], k_ref[...], v_ref[...]
    ii = lax.broadcasted_iota(jnp.int32, (c, c), 0)
    jj = lax.broadcasted_iota(jnp.int32, (c, c), 1)
    rel = jnp.where(fwd, ii - jj, jj - ii).astype(F32)
    decay = jnp.where(rel >= 0, jnp.exp(jnp.maximum(rel, 0.0) * lg), 0.0)
    pos = lax.broadcasted_iota(jnp.int32, (c, 1), 0).astype(F32)
    xi = jnp.exp(jnp.where(fwd, pos + 1.0, c - pos) * lg)
    zeta = jnp.exp(jnp.where(fwd, c - 1.0 - pos, pos) * lg)
    state = state_ref[...]
    scores = _nt_dot(q, k) * decay
    inner = jnp.dot(scores.astype(BF16), v, preferred_element_type=F32)
    cross = jnp.dot(q, state.astype(BF16), preferred_element_type=F32) * xi
    o_ref[...] = inner + cross
    kz = (k.astype(F32) * zeta).astype(BF16)
    upd = lax.dot_general(kz, v, (((0,), (0,)), ((), ())), preferred_element_type=F32)
    state_ref[...] = jnp.exp(c * lg) * state + upd


def retention(q, k, v, log_decay, *, n_t, n_c):
    bsz, nh, t, dk = q.shape
    dv = v.shape[-1]

    def blk(d, j):
        back = jnp.where(j < n_c, n_c - 1 - j, n_t - 1 - (j - n_c))
        return jnp.where(d == 0, j, back)

    qk = pl.BlockSpec((None, None, ROW_TILE, dk), lambda b, h, d, j: (b, h, blk(d, j), 0))
    return _pcall(
        _ret_kernel,
        grid=(bsz, nh, 2, n_t),
        in_specs=[pl.BlockSpec((None, 1, 1), lambda b, h, d, j: (d * RET_HEADS + h, 0, 0)), qk, qk,
                  pl.BlockSpec((None, None, ROW_TILE, dv), lambda b, h, d, j: (b, h, blk(d, j), 0))],
        out_specs=pl.BlockSpec((None, ROW_TILE, dv), lambda b, h, d, j: (d, b * n_t + blk(d, j), h)),
        out_shape=jax.ShapeDtypeStruct((2, bsz * t, nh * dv), F32),
        scratch_shapes=[pltpu.VMEM((dk, dv), F32)],
        compiler_params=_cparams(("parallel", "parallel", "arbitrary", "arbitrary")),
    )(log_decay.reshape(2 * RET_HEADS, 1, 1), q, k, v)


def _ret_gate_kernel(of_ref, ob_ref, g_ref, gn_ref, beta_ref, o_ref):
    o = of_ref[...] + ob_ref[...]
    mu = jnp.mean(o, axis=-1, keepdims=True)
    var = jnp.mean(jnp.square(o - mu), axis=-1, keepdims=True)
    y = (o - mu) * lax.rsqrt(var + NORM_EPS) * gn_ref[...]
    g = g_ref[...]
    o_ref[...] = (g * jax.nn.sigmoid(g) * y * beta_ref[...]).astype(o_ref.dtype)


def ret_gate(o2, p_attn, gn_g, beta):
    _, r, w = o2.shape
    dv = 2 * HEAD_DIM
    gate_col0 = COL_RET_GATE // dv
    vec = pl.BlockSpec((1, dv), lambda i, h: (0, h))
    return _pcall(
        _ret_gate_kernel,
        grid=(r // ROW_TILE, RET_HEADS),
        in_specs=[pl.BlockSpec((None, ROW_TILE, dv), lambda i, h: (0, i, h)),
                  pl.BlockSpec((None, ROW_TILE, dv), lambda i, h: (1, i, h)),
                  pl.BlockSpec((ROW_TILE, dv), lambda i, h: (i, gate_col0 + h)), vec, vec],
        out_specs=pl.BlockSpec((ROW_TILE, dv), lambda i, h: (i, h)),
        out_shape=jax.ShapeDtypeStruct((r, w), BF16),
        compiler_params=_cparams(("parallel", "parallel")),
    )(o2, o2, p_attn, gn_g, beta)


def _short_conv_kernel(cur_ref, prev_ref, next_ref, w_ref, b_ref, z_ref, x1_ref, x2_ref, *, n_seq_tiles):
    j = pl.program_id(1)

    @pl.when(j < n_seq_tiles)
    def _():
        u = cur_ref[...]
        rows = lax.broadcasted_iota(jnp.int32, u.shape, 0)
        prev_row = jnp.where(j == 0, 0.0, prev_ref[7:8, :])
        next_row = jnp.where(j == n_seq_tiles - 1, 0.0, next_ref[0:1, :])
        up = jnp.where(rows == 0, prev_row, pltpu.roll(u, 1, 0))
        un = jnp.where(rows == ROW_TILE - 1, next_row, pltpu.roll(u, ROW_TILE - 1, 0))
        y = up * w_ref[0:1, :] + u * w_ref[1:2, :] + un * w_ref[2:3, :] + b_ref[...]
        z_ref[...] = y[:, :HY_WIDTH]
        x1_ref[...] = y[:, HY_WIDTH:2 * HY_WIDTH]
        x2_ref[...] = y[:, 2 * HY_WIDTH:]

    @pl.when(j >= n_seq_tiles)
    def _():
        z_ref[...] = jnp.zeros_like(z_ref)
        x1_ref[...] = jnp.zeros_like(x1_ref)
        x2_ref[...] = jnp.zeros_like(x2_ref)


def short_conv(p_hy, w, b, *, bsz, n_t, first_tile, n_seq_tiles, n_pad_tiles):
    w3 = 3 * HY_WIDTH
    sub = ROW_TILE // 8
    last_blk8 = p_hy.shape[0] // 8 - 1

    def cur(bb, j):
        return (bb * n_t + first_tile + jnp.minimum(j, n_seq_tiles - 1), 0)

    def prev(bb, j):
        return (jnp.maximum(cur(bb, j)[0] * sub - 1, 0), 0)

    def nxt(bb, j):
        return (jnp.minimum((cur(bb, j)[0] + 1) * sub, last_blk8), 0)

    kern = functools.partial(_short_conv_kernel, n_seq_tiles=n_seq_tiles)
    rows = bsz * n_pad_tiles * ROW_TILE
    return _pcall(
        kern,
        grid=(bsz, n_pad_tiles),
        in_specs=[pl.BlockSpec((ROW_TILE, w3), cur), pl.BlockSpec((8, w3), prev), pl.BlockSpec((8, w3), nxt),
                  pl.BlockSpec((3, w3), lambda bb, j: (0, 0)), pl.BlockSpec((1, w3), lambda bb, j: (0, 0))],
        out_specs=[pl.BlockSpec((ROW_TILE, HY_WIDTH), lambda bb, j: (bb * n_pad_tiles + j, 0))] * 3,
        out_shape=[jax.ShapeDtypeStruct((rows, HY_WIDTH), F32)] * 3,
        compiler_params=_cparams(("parallel", "arbitrary")),
    )(p_hy, p_hy, p_hy, w, b.reshape(1, w3))


def _filter_kernel(feat_ref, w1_ref, b1_ref, w2_ref, b2_ref, w3_ref, win_ref, h_ref, asum_ref):
    i = pl.program_id(0)
    hp = lax.Precision.HIGHEST
    h = jnp.sin(jnp.dot(feat_ref[...], w1_ref[...], precision=hp, preferred_element_type=F32) + b1_ref[...])
    h = jnp.sin(jnp.dot(h, w2_ref[...], precision=hp, preferred_element_type=F32) + b2_ref[...])
    h = jnp.dot(h, w3_ref[...], precision=hp, preferred_element_type=F32)
    win = win_ref[...]
    h = h * jnp.concatenate([win] * (2 * HY_ORDER), axis=-1)
    h_ref[...] = h
    rows = lax.broadcasted_iota(jnp.int32, h.shape, 0) + i * ROW_TILE
    cols = lax.broadcasted_iota(jnp.int32, h.shape, 1)
    is_bwd = (cols // HY_WIDTH) % 2 == 1
    part = jnp.sum(jnp.where(is_bwd & (rows == 0), 0.0, jnp.abs(h)), axis=0, keepdims=True)

    @pl.when(i == 0)
    def _():
        asum_ref[...] = part

    @pl.when(i > 0)
    def _():
        asum_ref[...] += part


def hyena_filter_taps(length, w1, b1, w2, b2, w3):
    t = jnp.arange(length, dtype=F32)
    t_norm = t / length
    f = jnp.linspace(1e-4, HY_BANDS - 1, HY_BANDS, dtype=F32)
    wt = 2.0 * math.pi * t_norm
    feats = jnp.concatenate([t_norm[:, None], jnp.cos(wt[:, None] * f), -jnp.sin(wt[:, None] * f)], axis=-1)
    feats = jnp.pad(feats, ((0, 0), (0, 128 - HY_EMB_DIM)))
    w1p = jnp.pad(w1, ((0, 128 - HY_EMB_DIM), (0, 0)))
    deltas = jnp.abs(jnp.linspace(HY_MIN_DECAY, HY_MAX_DECAY, HY_WIDTH, dtype=F32))
    window = jnp.exp(-t_norm[:, None] * deltas[None])
    wout = HY_ORDER * 2 * HY_WIDTH
    full = lambda shp: pl.BlockSpec(shp, lambda i: (0, 0))
    return _pcall(
        _filter_kernel,
        grid=(length // ROW_TILE,),
        in_specs=[pl.BlockSpec((ROW_TILE, 128), lambda i: (i, 0)), full((128, HY_FILT_HIDDEN)),
                  full((1, HY_FILT_HIDDEN)), full((HY_FILT_HIDDEN, HY_FILT_HIDDEN)), full((1, HY_FILT_HIDDEN)),
                  full((HY_FILT_HIDDEN, wout)), pl.BlockSpec((ROW_TILE, HY_WIDTH), lambda i: (i, 0))],
        out_specs=[pl.BlockSpec((ROW_TILE, wout), lambda i: (i, 0)), full((1, wout))],
        out_shape=[jax.ShapeDtypeStruct((length, wout), F32), jax.ShapeDtypeStruct((1, wout), F32)],
        compiler_params=_cparams(("arbitrary",)),
    )(feats, w1p, b1.reshape(1, -1), w2, b2.reshape(1, -1), w3, window)


class _Dft:
    def __init__(self, n1):
        n2 = FFT_N2
        assert n1 % 16 == 0
        self.n1, self.n = n1, n1 * n2
        self.half = n1 // 2
        self.nf = n1 // 2 + 1
        self.nfp = -(-self.nf // 16) * 16
        f1 = np.arange(self.nf)[:, None]
        ang = 2.0 * np.pi * f1 * np.arange(n1)[None, :] / n1
        s1 = np.zeros((2 * self.nfp, n1))
        s1[:self.nf] = np.cos(ang)
        s1[self.nfp:self.nfp + self.nf] = -np.sin(ang)
        self.s1_full = s1
        tw = 2.0 * np.pi * f1 * np.arange(n2)[None, :] / self.n
        self.tw_cos = np.cos(tw)[:, :, None]
        self.tw_sin = np.sin(tw)[:, :, None]
        a2 = 2.0 * np.pi * np.arange(n2)[:, None] * np.arange(n2)[None, :] / n2
        wc, ws = np.cos(a2), np.sin(a2)
        self.m_fwd = np.block([[wc, ws], [-ws, wc]])
        self.m_inv = np.block([[wc, -ws], [ws, wc]])
        wgt = np.full(self.nf, 2.0)
        wgt[0] = 1.0
        wgt[-1] = 1.0
        ango = 2.0 * np.pi * np.arange(self.half)[:, None] * np.arange(self.nf)[None, :] / n1
        self.s4_re = np.zeros((self.half, self.nfp))
        self.s4_im = np.zeros((self.half, self.nfp))
        self.s4_re[:, :self.nf] = np.cos(ango) * wgt / self.n
        self.s4_im[:, :self.nf] = -np.sin(ango) * wgt / self.n

    @staticmethod
    def const(a, dtype=BF16):
        return jnp.asarray(a, F32).astype(dtype)


def _fft_s1_kernel(m_ref, x_ref, o_ref):
    k_rows, n_sub, w = x_ref.shape
    x = x_ref[...].reshape(k_rows * n_sub, w)
    y = jnp.dot(m_ref[...], x.astype(BF16), preferred_element_type=F32)
    o_ref[...] = y.reshape(o_ref.shape)


def fft_stage1(mat, x3, *, n_batch, k_rows):
    w = x3.shape[-1]
    m = mat.shape[0]
    big = _Dft.const(np.kron(mat, np.eye(FFT_SUB)))
    return _pcall(
        _fft_s1_kernel,
        grid=(n_batch, FFT_N2 // FFT_SUB),
        in_specs=[pl.BlockSpec((m * FFT_SUB, k_rows * FFT_SUB), lambda b, j: (0, 0)),
                  pl.BlockSpec((k_rows, FFT_SUB, w), lambda b, j: (b, j, 0))],
        out_specs=pl.BlockSpec((None, m, FFT_SUB, w), lambda b, j: (b, 0, j, 0)),
        out_shape=jax.ShapeDtypeStruct((n_batch, m, FFT_N2, w), F32),
        compiler_params=_cparams(("parallel", "parallel")),
    )(big, x3)


def _twiddle(ar, ai, c, s):
    return ar * c + ai * s, ai * c - ar * s


def _fft_filter_mid_kernel(ar_ref, ai_ref, c_ref, s_ref, mf_ref, sc_ref, hb0_ref, kr_ref, ki_ref):
    br, bi = _twiddle(ar_ref[...], ai_ref[...], c_ref[...], s_ref[...])
    x = jnp.dot(mf_ref[...], jnp.concatenate([br, bi], axis=0).astype(BF16), preferred_element_type=F32)
    xr, xi = x[:FFT_N2], x[FFT_N2:]
    w = HY_WIDTH
    for o in range(HY_ORDER):
        fw = slice(2 * o * w, (2 * o + 1) * w)
        bw = slice((2 * o + 1) * w, (2 * o + 2) * w)
        oc = slice(o * w, (o + 1) * w)
        kr_ref[:, oc] = (xr[:, fw] + xr[:, bw] - hb0_ref[:, oc]) * sc_ref[:, oc]
        ki_ref[:, oc] = (xi[:, fw] - xi[:, bw]) * sc_ref[:, oc]


def fft_filter_mid(a, dft, kscale, hb0):
    cols = a.shape[-1]
    n2 = FFT_N2
    tw = pl.BlockSpec((None, n2, 1), lambda f: (f, 0, 0))
    blk = pl.BlockSpec((None, n2, cols // 2), lambda f: (f, 0, 0))
    vec = pl.BlockSpec((1, cols // 2), lambda f: (0, 0))
    return _pcall(
        _fft_filter_mid_kernel,
        grid=(dft.nf,),
        in_specs=[pl.BlockSpec((None, n2, cols), lambda f: (f, 0, 0)),
                  pl.BlockSpec((None, n2, cols), lambda f: (dft.nfp + f, 0, 0)), tw, tw,
                  pl.BlockSpec((2 * n2, 2 * n2), lambda f: (0, 0)), vec, vec],
        out_specs=[blk, blk],
        out_shape=[jax.ShapeDtypeStruct((dft.nf, n2, cols // 2), F32)] * 2,
        compiler_params=_cparams(("parallel",)),
    )(a, a, _Dft.const(dft.tw_cos, F32), _Dft.const(dft.tw_sin, F32), _Dft.const(dft.m_fwd), kscale, hb0)


def _fft_mid_kernel(ar_ref, ai_ref, c_ref, s_ref, mf_ref, mi_ref, kr_ref, ki_ref, er_ref, ei_ref, *, nf):
    @pl.when(pl.program_id(1) < nf)
    def _():
        c, s = c_ref[...], s_ref[...]
        br, bi = _twiddle(ar_ref[...], ai_ref[...], c, s)
        x = jnp.dot(mf_ref[...], jnp.concatenate([br, bi], axis=0).astype(BF16), preferred_element_type=F32)
        xr, xi = x[:FFT_N2], x[FFT_N2:]
        kr, ki = kr_ref[...], ki_ref[...]
        yr = xr * kr - xi * ki
        yi = xr * ki + xi * kr
        dd = jnp.dot(mi_ref[...], jnp.concatenate([yr, yi], axis=0).astype(BF16), preferred_element_type=F32)
        dr, di = dd[:FFT_N2], dd[FFT_N2:]
        er_ref[...] = dr * c - di * s
        ei_ref[...] = di * c + dr * s

    @pl.when(pl.program_id(1) >= nf)
    def _():
        er_ref[...] = jnp.zeros_like(er_ref)
        ei_ref[...] = jnp.zeros_like(ei_ref)


def fft_mid(a, dft, kr, ki, order):
    nb = a.shape[0]
    n2, w = FFT_N2, HY_WIDTH
    nf, nfp = dft.nf, dft.nfp
    fc = lambda f: jnp.minimum(f, nf - 1)
    tw = pl.BlockSpec((None, n2, 1), lambda b, f: (fc(f), 0, 0))
    mat = pl.BlockSpec((2 * n2, 2 * n2), lambda b, f: (0, 0))
    kf = pl.BlockSpec((None, n2, w), lambda b, f: (fc(f), 0, order))
    out = pl.BlockSpec((None, None, n2, w), lambda b, f: (b, f, 0, 0))
    return _pcall(
        functools.partial(_fft_mid_kernel, nf=nf),
        grid=(nb, nfp),
        in_specs=[pl.BlockSpec((None, None, n2, w), lambda b, f: (b, fc(f), 0, 0)),
                  pl.BlockSpec((None, None, n2, w), lambda b, f: (b, nfp + fc(f), 0, 0)),
                  tw, tw, mat, mat, kf, kf],
        out_specs=[out, out],
        out_shape=[jax.ShapeDtypeStruct((nb, nfp, n2, w), F32)] * 2,
        compiler_params=_cparams(("parallel", "arbitrary")),
    )(a, a, _Dft.const(dft.tw_cos, F32), _Dft.const(dft.tw_sin, F32), _Dft.const(dft.m_fwd),
      _Dft.const(dft.m_inv), kr, ki)


def _fft_s4_kernel(mr_ref, mi_ref, er_ref, ei_ref, gate_ref, z_ref, bias_ref, scale_ref, o_ref):
    nfp, n_sub, w = er_ref.shape
    er = er_ref[...].reshape(nfp * n_sub, w).astype(BF16)
    ei = ei_ref[...].reshape(nfp * n_sub, w).astype(BF16)
    y = (jnp.dot(mr_ref[...], er, preferred_element_type=F32)
         + jnp.dot(mi_ref[...], ei, preferred_element_type=F32)).reshape(o_ref.shape)
    o_ref[...] = gate_ref[...] * (y + bias_ref[...] * z_ref[...]) * scale_ref[...]


def fft_stage4(dft, er, ei, gate, z, bias_row, scale_row):
    nb = er.shape[0]
    m, nfp = dft.half, dft.nfp
    w = z.shape[-1]
    eye = np.eye(FFT_SUB)
    row = pl.BlockSpec((m, FFT_SUB, w), lambda b, j: (b, j, 0))
    vec = pl.BlockSpec((1, 1, w), lambda b, j: (0, 0, 0))
    mat = pl.BlockSpec((m * FFT_SUB, nfp * FFT_SUB), lambda b, j: (0, 0))
    spec = pl.BlockSpec((None, nfp, FFT_SUB, w), lambda b, j: (b, 0, j, 0))
    return _pcall(
        _fft_s4_kernel,
        grid=(nb, FFT_N2 // FFT_SUB),
        in_specs=[mat, mat, spec, spec, row, row, vec, vec],
        out_specs=row,
        out_shape=jax.ShapeDtypeStruct((nb * m, FFT_N2, w), F32),
        compiler_params=_cparams(("parallel", "parallel")),
    )(_Dft.const(np.kron(dft.s4_re, eye)), _Dft.const(np.kron(dft.s4_im, eye)), er, ei, gate, z,
      bias_row.reshape(1, 1, w), scale_row.reshape(1, 1, w))


def hyena_long_conv(z, x1, x2, filt, hbias, beta_hy, *, bsz, length, n1):
    dft = _Dft(n1)
    n2, w = FFT_N2, HY_WIDTH
    half = dft.half
    assert half >= 8 and length % n2 == 0
    taps, asum = filt
    asum = asum.reshape(HY_ORDER, 2, w)
    kscale = (1.0 / (asum[:, 0] + asum[:, 1] + NORM_EPS)).reshape(1, HY_ORDER * w)
    hb0 = taps[0].reshape(HY_ORDER, 2, w)[:, 1].reshape(1, HY_ORDER * w)
    k_taps = length // n2
    a_f = fft_stage1(dft.s1_full[:, :k_taps], taps.reshape(k_taps, n2, 2 * HY_ORDER * w), n_batch=1, k_rows=k_taps)
    kr, ki = fft_filter_mid(a_f[0], dft, kscale, hb0)
    s1 = dft.s1_full[:, :half]
    ones = jnp.ones((1, w), F32)
    shape3 = (bsz * half, n2, w)
    zc = z.reshape(shape3)
    for o, gate in enumerate((x1, x2)):
        a = fft_stage1(s1, zc, n_batch=bsz, k_rows=half)
        er, ei = fft_mid(a, dft, kr, ki, o)
        last = o == HY_ORDER - 1
        zc = fft_stage4(dft, er, ei, gate.reshape(shape3), zc, hbias[o].reshape(1, w),
                        beta_hy.reshape(1, w) if last else ones)
    return zc.reshape(bsz * half * n2, w)


def _rope_tables(n_ctx, n_lat):
    rows = n_lat // GRID_W
    row = jnp.repeat(jnp.arange(rows, dtype=F32), GRID_W)
    col = jnp.tile(jnp.arange(GRID_W, dtype=F32), rows)
    n_freq = HEAD_DIM // 4
    freqs = ROPE_THETA ** (-jnp.arange(n_freq, dtype=F32) / n_freq)
    ar = row[:, None] * freqs
    ac = col[:, None] * freqs
    cos = jnp.concatenate([jnp.cos(ar), jnp.cos(ar), jnp.cos(ac), jnp.cos(ac)], axis=-1)
    sin = jnp.concatenate([jnp.sin(ar), jnp.sin(ar), jnp.sin(ac), jnp.sin(ac)], axis=-1)
    cos = jnp.concatenate([jnp.ones((n_ctx, HEAD_DIM), F32), cos], axis=0)
    sin = jnp.concatenate([jnp.zeros((n_ctx, HEAD_DIM), F32), sin], axis=0)
    even = (np.arange(HEAD_DIM) // n_freq) % 2 == 0
    sin_a = jnp.where(even, -sin, 0.0)
    sin_b = jnp.where(even, 0.0, sin)
    tile2 = lambda a: jnp.concatenate([a, a], axis=-1)
    return tile2(cos), tile2(sin_a), tile2(sin_b)


def _attn_key_chunk(t):
    for tk in (1408, 1280, 1024, 768, 512, 256):
        if t % tk == 0:
            return tk
    raise ValueError(t)


def kernel(x, c, ctx, c_ctx, ada_w, ada_b, norm_g, w_in, w_out, mix_beta, gqa_qk_g, diff_lambda, diff_subln_g,
           ret_log_decay, ret_gn_g, hy_short_w, hy_short_b, hy_filt_w1, hy_filt_b1, hy_filt_w2, hy_filt_b2,
           hy_filt_w3, hy_bias, ffn_w_gate, ffn_w_up, ffn_w_down, moe_router, moe_w_gate, moe_w_up, moe_w_down):
    bsz, n_lat, d = x.shape
    n_ctx = ctx.shape[1]
    t = n_ctx + n_lat
    r = bsz * t
    n_t, n_c = t // ROW_TILE, n_ctx // ROW_TILE
    n_s = n_lat // ROW_TILE
    assert n_ctx % ROW_TILE == 0 and n_lat % ROW_TILE == 0 and r % 512 == 0
    tk = _attn_key_chunk(t)
    assert tk >= n_ctx

    rope_c, rope_sa, rope_sb = _rope_tables(n_ctx, n_lat)
    xs = jnp.concatenate([ctx, x], axis=1).reshape(r, d)

    cvec = jnp.stack([jnp.broadcast_to(c_ctx, c.shape), c], axis=1).reshape(2 * bsz, d)
    cvec = jax.nn.silu(cvec)
    cvec = jnp.pad(cvec, ((0, 16 - 2 * bsz), (0, 0))).astype(BF16)

    n1_lat = 2 * n_lat // FFT_N2
    ctx_pad_tiles = max(n_c, 1024 // ROW_TILE)
    n1_ctx = 2 * ctx_pad_tiles * ROW_TILE // FFT_N2

    for l in range(DEPTH):
        lambda_init = 0.8 - 0.6 * math.exp(-0.3 * l)
        mod = mm(cvec, ada_w, tm=16, tn=1536, out_dtype=F32, lead=(l,))[:2 * bsz] + ada_b[l]
        mod = mod.reshape(2 * bsz, 6, 1, d)
        sh_m, sc_m, gt_m, sh_f, sc_f, gt_f = (mod[:, k] for k in range(6))
        beta = mix_beta[l].reshape(1, 4 * GROUP_WIDTH)
        beta_a, beta_b, beta_r, beta_d = (beta[:, GROUP_WIDTH * k:GROUP_WIDTH * (k + 1)] for k in range(4))

        (h,) = norm_mod(xs, norm_g[l, 0], sh_m, sc_m, n_t, n_c, (BF16,))
        p_attn = mm(h, w_in, tm=512, tn=768, out_dtype=F32, lead=(l,), n_cols=COL_ATTN)
        p_hy = mm(h, w_in, tm=512, tn=768, out_dtype=F32, lead=(l,), col_off=COL_HY // 768,
                  n_cols=IN_WIDTH - COL_HY)
        aq, ak, av, dq, dk, dv, rq, rk, rv = prep_heads(p_attn, rope_c, rope_sa, rope_sb, gqa_qk_g[l], bsz, n_t)

        a_out = gqa_attention(aq, ak, av, beta_a, n_t=n_t, n_c=n_c, tk=tk)

        lamf = diff_lambda[l]
        lam_full = (jnp.exp(jnp.sum(lamf[0] * lamf[1])) - jnp.exp(jnp.sum(lamf[2] * lamf[3])) + lambda_init)
        b_out = diff_attention(dq, dk, dv, lam_full.reshape(1, 1), diff_subln_g[l].reshape(1, -1), beta_b,
                               1.0 - lambda_init, n_t=n_t, n_c=n_c, tk=tk)

        o2 = retention(rq, rk, rv, ret_log_decay[l], n_t=n_t, n_c=n_c)
        r_out = ret_gate(o2, p_attn, ret_gn_g[l].reshape(1, -1), beta_r)

        filt_args = (hy_filt_w1[l], hy_filt_b1[l], hy_filt_w2[l], hy_filt_b2[l], hy_filt_w3[l])
        z_l, x1_l, x2_l = short_conv(p_hy, hy_short_w[l], hy_short_b[l], bsz=bsz, n_t=n_t, first_tile=n_c,
                                     n_seq_tiles=n_s, n_pad_tiles=n_s)
        d_l = hyena_long_conv(z_l, x1_l, x2_l, hyena_filter_taps(n_lat, *filt_args), hy_bias[l], beta_d[0],
                              bsz=bsz, length=n_lat, n1=n1_lat)
        z_c, x1_c, x2_c = short_conv(p_hy, hy_short_w[l], hy_short_b[l], bsz=bsz, n_t=n_t, first_tile=0,
                                     n_seq_tiles=n_c, n_pad_tiles=ctx_pad_tiles)
        d_c = hyena_long_conv(z_c, x1_c, x2_c, hyena_filter_taps(n_ctx, *filt_args), hy_bias[l], beta_d[0],
                              bsz=bsz, length=n_ctx, n1=n1_ctx)
        d_out = jnp.concatenate([d_c.reshape(bsz, -1, HY_WIDTH)[:, :n_ctx], d_l.reshape(bsz, n_lat, HY_WIDTH)],
                                axis=1).reshape(r, HY_WIDTH).astype(BF16)

        mixed = jnp.concatenate([a_out, b_out, r_out, d_out], axis=-1)
        y = mm(mixed, w_out, tm=512, tn=512, out_dtype=F32, lead=(l,))
        xs = gate_res(xs, y, norm_g[l, 1], gt_m, n_t, n_c)

        i = l // 2
        if l % 2 == 0:
            (f,) = norm_mod(xs, norm_g[l, 2], sh_f, sc_f, n_t, n_c, (BF16,))
            hid = swiglu_up(f, ffn_w_gate, ffn_w_up, tm=512, tn=512, lead=(i,))
            f_out = mm(hid, ffn_w_down, tm=512, tn=256, out_dtype=F32, lead=(i,))
            xs = gate_res(xs, f_out, norm_g[l, 3], gt_f, n_t, n_c)
        else:
            (f_in,) = norm_mod(xs, norm_g[l, 2], sh_f, sc_f, n_t, n_c, (F32,))
            top_i, top_w = router(f_in, moe_router[i].T)
            n_pad_rows = (2 * r // MOE_TM + N_EXPERTS) * MOE_TM
            row_token, row_weight, tile_expert, n_used, pos = route_tokens(top_i, top_w, n_pad_rows)
            y_sorted = moe_experts(f_in, moe_w_gate, moe_w_up, moe_w_down, i, row_token, row_weight,
                                   tile_expert, n_used)
            xs = moe_combine_gate_res(xs, y_sorted, pos, norm_g[l, 3], gt_f, n_t, n_c)

    return xs.reshape(bsz, t, d)[:, n_ctx:]
```

```python
import functools
import math

import numpy as np
import jax
import jax.numpy as jnp
from jax import lax
from jax.experimental import pallas as pl
from jax.experimental.pallas import tpu as pltpu

F32 = jnp.float32
BF16 = jnp.bfloat16

D_MODEL = 2048
DEPTH = 4
GRID_W = 64
HEAD_DIM = 64
ROPE_THETA = 10000.0
NORM_EPS = 1e-6
GROUP_WIDTH = 512
GQA_Q_HEADS = 8
GQA_KV_HEADS = 2
GQA_REP = 4
DIFF_HEADS = 4
RET_HEADS = 4
HY_WIDTH = 512
HY_ORDER = 2
HY_EMB_DIM = 33
HY_BANDS = 16
HY_FILT_HIDDEN = 64
HY_MIN_DECAY = math.log(1e-2) / 1.5
HY_MAX_DECAY = math.log(1e-2) / 0.3
N_EXPERTS = 8

COL_ATTN = 3840
COL_DIFF = 768
COL_RET = 2304
COL_RET_GATE = 3328
COL_HY = 3840
IN_WIDTH = 5376

ROW_TILE = 256
MOE_TM = 1024
MOE_TF = 512
FFT_N2 = 128
FFT_MID_F = 2
FFT_SUB = 8
VMEM_LIMIT_MB = 48
MOE_VMEM_LIMIT_MB = 56


def _cparams(sem, vmem_mb=VMEM_LIMIT_MB):
    return pltpu.CompilerParams(dimension_semantics=sem, vmem_limit_bytes=vmem_mb * 1024 * 1024)


def _pcall(kern, **kw):
    fn = kern.func if isinstance(kern, functools.partial) else kern
    return pl.pallas_call(kern, name=fn.__name__.strip("_"), **kw)


def _pick_tm(m, k, max_tile_mb):
    for tm in (1536, 1280, 1024, 768, 512, 256, 128, 16):
        if m % tm == 0 and tm * k * 2 <= max_tile_mb * 2 ** 20:
            return tm
    raise ValueError((m, k))


def _mm_kernel(a_ref, b_ref, o_ref, bb_ref):
    @pl.when(pl.program_id(1) == 0)
    def _():
        bb_ref[...] = b_ref[...].astype(BF16)

    o_ref[...] = jnp.dot(a_ref[...], bb_ref[...], preferred_element_type=F32).astype(o_ref.dtype)


def _b_spec(b, lead, k, tn, col_off):
    nlead = len(lead)
    return pl.BlockSpec((None,) * nlead + (k, tn), lambda j, i: tuple(lead) + (0, j + col_off))


def mm(a, b, *, tn, out_dtype, lead=(), col_off=0, n_cols=None, max_tile_mb=9):
    m, k = a.shape
    n = b.shape[-1] if n_cols is None else n_cols
    tm = _pick_tm(m, k, max_tile_mb)
    assert n % tn == 0 and a.dtype == BF16, (n, tn, a.dtype)
    return _pcall(
        _mm_kernel,
        grid=(n // tn, m // tm),
        in_specs=[pl.BlockSpec((tm, k), lambda j, i: (i, 0)), _b_spec(b, lead, k, tn, col_off)],
        out_specs=pl.BlockSpec((tm, tn), lambda j, i: (i, j)),
        out_shape=jax.ShapeDtypeStruct((m, n), out_dtype),
        scratch_shapes=[pltpu.VMEM((k, tn), BF16)],
        compiler_params=_cparams(("parallel", "arbitrary")),
    )(a, b)


def _swiglu_up_kernel(a_ref, wg_ref, wu_ref, o_ref, bg_ref, bu_ref):
    @pl.when(pl.program_id(1) == 0)
    def _():
        bg_ref[...] = wg_ref[...].astype(BF16)
        bu_ref[...] = wu_ref[...].astype(BF16)

    a = a_ref[...]
    g = jnp.dot(a, bg_ref[...], preferred_element_type=F32)
    u = jnp.dot(a, bu_ref[...], preferred_element_type=F32)
    o_ref[...] = (g * jax.nn.sigmoid(g) * u).astype(o_ref.dtype)


def swiglu_up(a, wg, wu, *, tn, lead):
    m, k = a.shape
    n = wg.shape[-1]
    tm = _pick_tm(m, k, 4)
    assert n % tn == 0
    return _pcall(
        _swiglu_up_kernel,
        grid=(n // tn, m // tm),
        in_specs=[pl.BlockSpec((tm, k), lambda j, i: (i, 0)),
                  _b_spec(wg, lead, k, tn, 0), _b_spec(wu, lead, k, tn, 0)],
        out_specs=pl.BlockSpec((tm, tn), lambda j, i: (i, j)),
        out_shape=jax.ShapeDtypeStruct((m, n), BF16),
        scratch_shapes=[pltpu.VMEM((k, tn), BF16)] * 2,
        compiler_params=_cparams(("parallel", "arbitrary")),
    )(a, wg, wu)


def _seg_map(n_t, n_c):
    return lambda i: (2 * (i // n_t) + jnp.where(i % n_t >= n_c, 1, 0), 0, 0)


def _rms(x):
    return x * lax.rsqrt(jnp.mean(x * x, axis=-1, keepdims=True) + NORM_EPS)


def _norm_mod_kernel(x_ref, g_ref, sh_ref, sc_ref, *o_refs):
    y = _rms(x_ref[...]) * g_ref[...]
    y = y * (1.0 + sc_ref[...]) + sh_ref[...]
    for o_ref in o_refs:
        o_ref[...] = y.astype(o_ref.dtype)


def norm_mod(x, g, shift, scale, n_t, n_c, out_dtypes):
    r, d = x.shape
    seg = _seg_map(n_t, n_c)
    row = pl.BlockSpec((ROW_TILE, d), lambda i: (i, 0))
    outs = _pcall(
        _norm_mod_kernel,
        grid=(r // ROW_TILE,),
        in_specs=[row, pl.BlockSpec((1, d), lambda i: (0, 0)),
                  pl.BlockSpec((None, 1, d), seg), pl.BlockSpec((None, 1, d), seg)],
        out_specs=[row] * len(out_dtypes),
        out_shape=[jax.ShapeDtypeStruct((r, d), dt) for dt in out_dtypes],
        compiler_params=_cparams(("parallel",)),
    )(x, g.reshape(1, d), shift, scale)
    return outs


def _gate_res_kernel(x_ref, y_ref, g_ref, gt_ref, o_ref):
    o_ref[...] = x_ref[...] + gt_ref[...] * (_rms(y_ref[...]) * g_ref[...])


def gate_res(x, y, g, gate, n_t, n_c):
    r, d = x.shape
    row = pl.BlockSpec((ROW_TILE, d), lambda i: (i, 0))
    return _pcall(
        _gate_res_kernel,
        grid=(r // ROW_TILE,),
        in_specs=[row, row, pl.BlockSpec((1, d), lambda i: (0, 0)),
                  pl.BlockSpec((None, 1, d), _seg_map(n_t, n_c))],
        out_specs=row,
        out_shape=jax.ShapeDtypeStruct((r, d), F32),
        compiler_params=_cparams(("parallel",)),
    )(x, y, g.reshape(1, d), gate)


def _router_kernel(f_ref, rt_ref, idx_ref, w_ref):
    lt = lax.dot_general(rt_ref[...], f_ref[...], (((1,), (1,)), ((), ())),
                         precision=lax.Precision.HIGHEST, preferred_element_type=F32)
    e_idx = lax.broadcasted_iota(jnp.int32, lt.shape, 0)
    m1 = jnp.max(lt, axis=0, keepdims=True)
    i1 = jnp.min(jnp.where(lt == m1, e_idx, N_EXPERTS), axis=0, keepdims=True)
    rest = jnp.where(e_idx == i1, -jnp.inf, lt)
    m2 = jnp.max(rest, axis=0, keepdims=True)
    i2 = jnp.min(jnp.where(rest == m2, e_idx, N_EXPERTS), axis=0, keepdims=True)
    e2 = jnp.exp(m2 - m1)
    idx_ref[...] = jnp.concatenate([i1, i2], axis=0)
    w_ref[...] = jnp.concatenate([1.0 / (1.0 + e2), e2 / (1.0 + e2)], axis=0)


def router(f, router_t):
    r, d = f.shape
    return _pcall(
        _router_kernel,
        grid=(r // ROW_TILE,),
        in_specs=[pl.BlockSpec((ROW_TILE, d), lambda i: (i, 0)),
                  pl.BlockSpec((N_EXPERTS, d), lambda i: (0, 0))],
        out_specs=[pl.BlockSpec((2, ROW_TILE), lambda i: (0, i))] * 2,
        out_shape=[jax.ShapeDtypeStruct((2, r), jnp.int32), jax.ShapeDtypeStruct((2, r), F32)],
        compiler_params=_cparams(("parallel",)),
    )(f, router_t)


def _row_copy(src_hbm, dst, src_row, dst_row, sem):
    return pltpu.make_async_copy(src_hbm.at[pl.ds(src_row, 1), :], dst.at[pl.ds(dst_row, 1), :], sem)


def _gather_rows(src_hbm, dst, idx_ref, base, n_rows, sem):
    def start(r, carry):
        _row_copy(src_hbm, dst, idx_ref[base + r], r, sem).start()
        return carry

    def wait(r, carry):
        _row_copy(src_hbm, dst, idx_ref[base + r], r, sem).wait()
        return carry

    lax.fori_loop(0, n_rows, start, 0, unroll=8)
    lax.fori_loop(0, n_rows, wait, 0, unroll=8)


def route_tokens(idx, w, n_pad_rows):
    r = idx.shape[1]
    n_assign = 2 * r
    n_tiles = n_pad_rows // MOE_TM
    e_flat = idx.reshape(n_assign)
    w_flat = w.reshape(n_assign)
    counts = jnp.sum((e_flat[:, None] == jnp.arange(N_EXPERTS)[None, :]).astype(jnp.int32), axis=0)
    padded = (counts + MOE_TM - 1) // MOE_TM * MOE_TM
    group_end = jnp.cumsum(padded)
    group_start = group_end - padded
    sorted_start = jnp.cumsum(counts) - counts
    order = jnp.argsort(e_flat, stable=True).astype(jnp.int32)
    inv = jnp.argsort(order).astype(jnp.int32)
    pos = group_start[e_flat] + (inv - sorted_start[e_flat])
    rows = jnp.arange(n_pad_rows, dtype=jnp.int32)
    row_e = jnp.minimum(jnp.sum((rows[:, None] >= group_end[None, :]).astype(jnp.int32), axis=1), N_EXPERTS - 1)
    rank = rows - group_start[row_e]
    valid = rank < counts[row_e]
    src = order[jnp.clip(sorted_start[row_e] + rank, 0, n_assign - 1)]
    row_token = jnp.where(valid, src % r, 0)
    row_weight = jnp.where(valid, w_flat[src], 0.0)
    n_used = group_end[-1] // MOE_TM
    tile_start = jnp.arange(n_tiles, dtype=jnp.int32) * MOE_TM
    tile_start = jnp.minimum(tile_start, group_end[-1] - 1)
    tile_expert = jnp.sum((tile_start[:, None] >= group_end[None, :]).astype(jnp.int32), axis=1)
    return (row_token, row_weight.reshape(n_pad_rows, 1), tile_expert.astype(jnp.int32),
            n_used.astype(jnp.int32).reshape(1), pos)


def _moe_expert_kernel(te_ref, nu_ref, tok_ref, f_hbm, wg_ref, wu_ref, wd_ref, rw_ref, o_ref, xb_ref, sem):
    i, f = pl.program_id(0), pl.program_id(1)
    used = i < nu_ref[0]

    @pl.when(jnp.logical_and(used, f == 0))
    def _():
        _gather_rows(f_hbm, o_ref, tok_ref, i * MOE_TM, MOE_TM, sem)
        xb_ref[...] = o_ref[...].astype(BF16)

    @pl.when(f == 0)
    def _():
        o_ref[...] = jnp.zeros_like(o_ref)

    @pl.when(used)
    def _():
        x = xb_ref[...]
        g = jnp.dot(x, wg_ref[...].astype(BF16), preferred_element_type=F32)
        u = jnp.dot(x, wu_ref[...].astype(BF16), preferred_element_type=F32)
        h = (g * jax.nn.sigmoid(g) * u).astype(BF16)
        o_ref[...] += jnp.dot(h, wd_ref[...].astype(BF16), preferred_element_type=F32)

    @pl.when(jnp.logical_and(used, f == pl.num_programs(1) - 1))
    def _():
        o_ref[...] = o_ref[...] * rw_ref[...]


def moe_experts(f_in, wg, wu, wd, layer, row_token, row_weight, tile_expert, n_used):
    r, d = f_in.shape
    ff = wg.shape[-1]
    n_pad_rows = row_token.shape[0]
    nt, nf = n_pad_rows // MOE_TM, ff // MOE_TF

    def fi(i, f, nu):
        return jnp.where(i < nu[0], f, nf - 1)

    grid_spec = pltpu.PrefetchScalarGridSpec(
        num_scalar_prefetch=3,
        grid=(nt, nf),
        in_specs=[pl.BlockSpec(memory_space=pl.ANY),
                  pl.BlockSpec((None, None, d, MOE_TF), lambda i, f, te, nu, tok: (layer, te[i], 0, fi(i, f, nu))),
                  pl.BlockSpec((None, None, d, MOE_TF), lambda i, f, te, nu, tok: (layer, te[i], 0, fi(i, f, nu))),
                  pl.BlockSpec((None, None, MOE_TF, d), lambda i, f, te, nu, tok: (layer, te[i], fi(i, f, nu), 0)),
                  pl.BlockSpec((MOE_TM, 1), lambda i, f, te, nu, tok: (i, 0))],
        out_specs=pl.BlockSpec((MOE_TM, d), lambda i, f, te, nu, tok: (i, 0)),
        scratch_shapes=[pltpu.VMEM((MOE_TM, d), BF16), pltpu.SemaphoreType.DMA(())],
    )
    return _pcall(
        _moe_expert_kernel,
        grid_spec=grid_spec,
        out_shape=jax.ShapeDtypeStruct((n_pad_rows, d), F32),
        compiler_params=_cparams(("arbitrary", "arbitrary"), MOE_VMEM_LIMIT_MB),
    )(tile_expert, n_used, row_token, f_in, wg, wu, wd, row_weight)


def _moe_combine_kernel(pos_ref, y_hbm, x_ref, g_ref, gt_ref, o_ref, buf_ref, sem):
    i = pl.program_id(0)
    for k in range(2):
        _gather_rows(y_hbm, buf_ref.at[k], pos_ref, (2 * i + k) * ROW_TILE, ROW_TILE, sem)
    y = buf_ref[0] + buf_ref[1]
    o_ref[...] = x_ref[...] + gt_ref[...] * (_rms(y) * g_ref[...])


def moe_combine_gate_res(x, y_sorted, pos, g, gate, n_t, n_c):
    r, d = x.shape
    nrt = r // ROW_TILE
    pos_tiles = pos.reshape(2, nrt, ROW_TILE).transpose(1, 0, 2).reshape(2 * r)
    seg = _seg_map(n_t, n_c)
    row = pl.BlockSpec((ROW_TILE, d), lambda i, p: (i, 0))
    grid_spec = pltpu.PrefetchScalarGridSpec(
        num_scalar_prefetch=1,
        grid=(nrt,),
        in_specs=[pl.BlockSpec(memory_space=pl.ANY), row, pl.BlockSpec((1, d), lambda i, p: (0, 0)),
                  pl.BlockSpec((None, 1, d), lambda i, p: seg(i))],
        out_specs=row,
        scratch_shapes=[pltpu.VMEM((2, ROW_TILE, d), F32), pltpu.SemaphoreType.DMA(())],
    )
    return _pcall(
        _moe_combine_kernel,
        grid_spec=grid_spec,
        out_shape=jax.ShapeDtypeStruct((r, d), F32),
        compiler_params=_cparams(("arbitrary",)),
    )(pos_tiles, y_sorted, x, g.reshape(1, d), gate)


def _rope(x, c, sa, sb):
    return x * c + pltpu.roll(x, 112, 1) * sa + pltpu.roll(x, 16, 1) * sb


def _head_rms(x, g, bd):
    sq = x * x
    hi = sq.astype(BF16)
    lo = (sq - hi.astype(F32)).astype(BF16)
    ms = jnp.dot(hi, bd, preferred_element_type=F32) + jnp.dot(lo, bd, preferred_element_type=F32)
    return x * lax.rsqrt(ms + NORM_EPS) * g


def _prep_kernel(p_ref, c_ref, sa_ref, sb_ref, gq_ref, gk_ref, bd_ref,
                 aq_ref, ak_ref, av_ref, dq_ref, dk_ref, dv_ref, rq_ref, rk_ref, rv_ref, rg_ref):
    c, sa, sb = c_ref[...], sa_ref[...], sb_ref[...]
    bd = bd_ref[...]
    scale = HEAD_DIM ** -0.5
    qscale = scale * math.log2(math.e)

    def chunk(j):
        return p_ref[:, 128 * j:128 * (j + 1)]

    def put_heads(ref, first, val):
        ref[first] = val[:, :HEAD_DIM].astype(ref.dtype)
        ref[first + 1] = val[:, HEAD_DIM:].astype(ref.dtype)

    for j in range(4):
        put_heads(aq_ref, 2 * j, _rope(_head_rms(chunk(j), gq_ref[...], bd), c, sa, sb) * qscale)
    put_heads(ak_ref, 0, _rope(_head_rms(chunk(4), gk_ref[...], bd), c, sa, sb))
    one64 = (lax.broadcasted_iota(jnp.int32, (ROW_TILE, HEAD_DIM), 1) == 0).astype(av_ref.dtype)
    one128 = (lax.broadcasted_iota(jnp.int32, (ROW_TILE, 2 * HEAD_DIM), 1) == 0).astype(dv_ref.dtype)
    v_gqa = chunk(5)
    for h in range(GQA_KV_HEADS):
        av_ref[h, :, :HEAD_DIM] = v_gqa[:, HEAD_DIM * h:HEAD_DIM * (h + 1)].astype(av_ref.dtype)
        av_ref[h, :, HEAD_DIM:] = one64
    for j in range(4):
        put_heads(dq_ref, 2 * j, _rope(chunk(6 + j), c, sa, sb) * qscale)
        put_heads(dk_ref, 2 * j, _rope(chunk(10 + j), c, sa, sb))
        dv_ref[j, :, :2 * HEAD_DIM] = chunk(14 + j).astype(dv_ref.dtype)
        dv_ref[j, :, 2 * HEAD_DIM:] = one128
    for j in range(2):
        put_heads(rq_ref, 2 * j, _rope(chunk(18 + j), c, sa, sb))
        put_heads(rk_ref, 2 * j, _rope(chunk(20 + j), c, sa, sb) * scale)
    for j in range(4):
        rv_ref[j] = chunk(22 + j).astype(rv_ref.dtype)
    rg_ref[...] = p_ref[:, COL_RET_GATE:COL_ATTN]


def prep_heads(p_attn, rope_c, rope_sa, rope_sb, qk_g, bsz, n_t):
    t = n_t * ROW_TILE
    bd = jnp.asarray(np.kron(np.eye(2), np.full((HEAD_DIM, HEAD_DIM), 1.0 / HEAD_DIM)), F32).astype(BF16)
    gq = jnp.tile(qk_g[0], 2).reshape(1, 128)
    gk = jnp.tile(qk_g[1], 2).reshape(1, 128)
    tab = pl.BlockSpec((ROW_TILE, 128), lambda i: (i % n_t, 0))
    vec = pl.BlockSpec((1, 128), lambda i: (0, 0))

    def heads(nh, dh):
        return (pl.BlockSpec((None, nh, ROW_TILE, dh), lambda i: (i // n_t, 0, i % n_t, 0)),
                jax.ShapeDtypeStruct((bsz, nh, t, dh), BF16))

    outs = [heads(8, 64), heads(2, 64), heads(2, 128), heads(8, 64), heads(8, 64), heads(4, 256),
            heads(4, 64), heads(4, 64), heads(4, 128)]
    return _pcall(
        _prep_kernel,
        grid=(bsz * n_t,),
        in_specs=[pl.BlockSpec((ROW_TILE, COL_ATTN), lambda i: (i, 0)), tab, tab, tab, vec, vec,
                  pl.BlockSpec((128, 128), lambda i: (0, 0))],
        out_specs=[o[0] for o in outs] + [pl.BlockSpec((ROW_TILE, GROUP_WIDTH), lambda i: (i, 0))],
        out_shape=[o[1] for o in outs] + [jax.ShapeDtypeStruct((bsz * t, GROUP_WIDTH), F32)],
        compiler_params=_cparams(("parallel",)),
    )(p_attn, rope_c, rope_sa, rope_sb, gq, gk, bd)


def _nt_dot(a, b):
    return lax.dot_general(a, b, (((1,), (1,)), ((), ())), preferred_element_type=F32)


def _softmax_chunks(score_fn, v_ref, s_ref, rows, dv, n_chunks, tk, n_ctx_keys, is_ctx):
    def update(s, start, carry):
        m, acc = carry
        m_new = jnp.maximum(m, jnp.max(s, axis=-1, keepdims=True))
        p = jnp.exp2(s - m_new)
        acc = jnp.exp2(m - m_new) * acc + jnp.dot(p.astype(BF16), v_ref[pl.ds(start, tk), :],
                                                  preferred_element_type=F32)
        return m_new, acc

    init = (jnp.full((rows, 1), -1e30, F32), jnp.zeros((rows, v_ref.shape[-1]), F32))

    def ctx_tile():
        s = score_fn(0, tk)
        col = lax.broadcasted_iota(jnp.int32, s.shape, 1)
        return update(jnp.where(col < n_ctx_keys, s, -1e30), 0, init)

    def lat_tile():
        stats = init
        s_ref[0] = score_fn(0, tk)
        for j in range(n_chunks):
            if j + 1 < n_chunks:
                s_ref[(j + 1) % 2] = score_fn((j + 1) * tk, tk)
            stats = update(s_ref[j % 2], j * tk, stats)
        return stats

    _, acc = lax.cond(is_ctx, ctx_tile, lat_tile)
    return acc[:, :dv] / acc[:, dv:dv + 1]


def _gqa_kernel(q_ref, k_ref, v_ref, beta_ref, o_ref, s_ref, *, tq, tk, n_chunks, n_ctx_tiles, n_ctx_keys):
    is_ctx = pl.program_id(2) < n_ctx_tiles
    q = q_ref[...].reshape(GQA_REP * tq, HEAD_DIM)

    def scores(start, size):
        return _nt_dot(q, k_ref[pl.ds(start, size), :])

    o = _softmax_chunks(scores, v_ref, s_ref, GQA_REP * tq, HEAD_DIM, n_chunks, tk, n_ctx_keys, is_ctx)
    for r in range(GQA_REP):
        sl = slice(HEAD_DIM * r, HEAD_DIM * (r + 1))
        o_ref[:, sl] = (o[r * tq:(r + 1) * tq] * beta_ref[:, sl]).astype(o_ref.dtype)


def gqa_attention(q, k, v, beta, *, n_t, n_c, tk):
    bsz, _, t, _ = q.shape
    tq = ROW_TILE
    kern = functools.partial(_gqa_kernel, tq=tq, tk=tk, n_chunks=t // tk, n_ctx_tiles=n_c,
                             n_ctx_keys=n_c * ROW_TILE)
    kspec = pl.BlockSpec((None, None, t, HEAD_DIM), lambda b, g, i: (b, g, 0, 0))
    vspec = pl.BlockSpec((None, None, t, v.shape[-1]), lambda b, g, i: (b, g, 0, 0))
    w = GQA_REP * HEAD_DIM
    return _pcall(
        kern,
        grid=(bsz, GQA_KV_HEADS, n_t),
        in_specs=[pl.BlockSpec((None, GQA_REP, tq, HEAD_DIM), lambda b, g, i: (b, g, i, 0)), kspec, vspec,
                  pl.BlockSpec((1, w), lambda b, g, i: (0, g))],
        out_specs=pl.BlockSpec((tq, w), lambda b, g, i: (b * n_t + i, g)),
        out_shape=jax.ShapeDtypeStruct((bsz * t, GROUP_WIDTH), BF16),
        scratch_shapes=[pltpu.VMEM((2, GQA_REP * tq, tk), F32)],
        compiler_params=_cparams(("parallel", "parallel", "arbitrary")),
    )(q, k, v, beta)


def _diff_kernel(q_ref, k_ref, v_ref, lam_ref, g_ref, beta_ref, o_ref, s_ref, *, tq, tk, n_chunks, n_ctx_tiles,
                 n_ctx_keys, out_scale):
    is_ctx = pl.program_id(2) < n_ctx_tiles
    q0, q1 = q_ref[0], q_ref[1]

    def scores(start, size):
        return jnp.concatenate([_nt_dot(q0, k_ref[0, pl.ds(start, size), :]),
                                _nt_dot(q1, k_ref[1, pl.ds(start, size), :])], axis=0)

    a = _softmax_chunks(scores, v_ref, s_ref, 2 * tq, 2 * HEAD_DIM, n_chunks, tk, n_ctx_keys, is_ctx)
    o = a[:tq] - lam_ref[...] * a[tq:]
    o = _rms(o) * g_ref[...] * out_scale
    o_ref[...] = (o * beta_ref[...]).astype(o_ref.dtype)


def diff_attention(q, k, v, lam, subln_g, beta, out_scale, *, n_t, n_c, tk):
    bsz, _, t, _ = q.shape
    tq = ROW_TILE
    dv = 2 * HEAD_DIM
    kern = functools.partial(_diff_kernel, tq=tq, tk=tk, n_chunks=t // tk, n_ctx_tiles=n_c,
                             n_ctx_keys=n_c * ROW_TILE, out_scale=out_scale)
    return _pcall(
        kern,
        grid=(bsz, DIFF_HEADS, n_t),
        in_specs=[pl.BlockSpec((None, 2, tq, HEAD_DIM), lambda b, h, i: (b, h, i, 0)),
                  pl.BlockSpec((None, 2, t, HEAD_DIM), lambda b, h, i: (b, h, 0, 0)),
                  pl.BlockSpec((None, None, t, v.shape[-1]), lambda b, h, i: (b, h, 0, 0)),
                  pl.BlockSpec((1, 1), lambda b, h, i: (0, 0)),
                  pl.BlockSpec((1, dv), lambda b, h, i: (0, 0)),
                  pl.BlockSpec((1, dv), lambda b, h, i: (0, h))],
        out_specs=pl.BlockSpec((tq, dv), lambda b, h, i: (b * n_t + i, h)),
        out_shape=jax.ShapeDtypeStruct((bsz * t, GROUP_WIDTH), BF16),
        scratch_shapes=[pltpu.VMEM((2, 2 * tq, tk), F32)],
        compiler_params=_cparams(("parallel", "parallel", "arbitrary")),
    )(q, k, v, lam, subln_g, beta)


def _ret_kernel(lg_ref, q_ref, k_ref, v_ref, o_ref, state_ref, decay_ref, xi_ref, zeta_ref):
    d = pl.program_id(1)
    c = ROW_TILE
    dv = v_ref.shape[-1]

    @pl.when(pl.program_id(2) == 0)
    def _():
        state_ref[...] = jnp.zeros_like(state_ref)
        fwd = d == 0
        ii = lax.broadcasted_iota(jnp.int32, (c, c), 0)
        jj = lax.broadcasted_iota(jnp.int32, (c, c), 1)
        rel = jnp.where(fwd, ii - jj, jj - ii).astype(F32)
        pos = lax.broadcasted_iota(jnp.int32, (c, 1), 0).astype(F32)
        for h in range(RET_HEADS):
            lg = lg_ref[h]
            decay_ref[h] = jnp.where(rel >= 0, jnp.exp(jnp.maximum(rel, 0.0) * lg), 0.0)
            xi_ref[h] = jnp.exp(jnp.where(fwd, pos + 1.0, c - pos) * lg)
            zeta_ref[h] = jnp.exp(jnp.where(fwd, c - 1.0 - pos, pos) * lg)

    for h in range(RET_HEADS):
        q, k, v = q_ref[h], k_ref[h], v_ref[h]
        state = state_ref[h]
        scores = _nt_dot(q, k) * decay_ref[h]
        inner = jnp.dot(scores.astype(BF16), v, preferred_element_type=F32)
        cross = jnp.dot(q, state.astype(BF16), preferred_element_type=F32) * xi_ref[h]
        o_ref[:, dv * h:dv * (h + 1)] = inner + cross
        kz = (k.astype(F32) * zeta_ref[h]).astype(BF16)
        upd = lax.dot_general(kz, v, (((0,), (0,)), ((), ())), preferred_element_type=F32)
        state_ref[h] = jnp.exp(c * lg_ref[h]) * state + upd


def retention(q, k, v, log_decay, *, n_t, n_c):
    bsz, nh, t, dk = q.shape
    dv = v.shape[-1]

    def blk(d, j):
        back = jnp.where(j < n_c, n_c - 1 - j, n_t - 1 - (j - n_c))
        return jnp.where(d == 0, j, back)

    qk = pl.BlockSpec((None, nh, ROW_TILE, dk), lambda b, d, j: (b, 0, blk(d, j), 0))
    return _pcall(
        _ret_kernel,
        grid=(bsz, 2, n_t),
        in_specs=[pl.BlockSpec((None, nh, 1, 1), lambda b, d, j: (d, 0, 0, 0)), qk, qk,
                  pl.BlockSpec((None, nh, ROW_TILE, dv), lambda b, d, j: (b, 0, blk(d, j), 0))],
        out_specs=pl.BlockSpec((None, ROW_TILE, nh * dv), lambda b, d, j: (d, b * n_t + blk(d, j), 0)),
        out_shape=jax.ShapeDtypeStruct((2, bsz * t, nh * dv), F32),
        scratch_shapes=[pltpu.VMEM((nh, dk, dv), F32), pltpu.VMEM((nh, ROW_TILE, ROW_TILE), F32),
                        pltpu.VMEM((nh, ROW_TILE, 1), F32), pltpu.VMEM((nh, ROW_TILE, 1), F32)],
        compiler_params=_cparams(("parallel", "arbitrary", "arbitrary")),
    )(log_decay.reshape(2, RET_HEADS, 1, 1), q, k, v)


def _ret_gate_kernel(of_ref, ob_ref, g_ref, gn_ref, beta_ref, o_ref):
    dv = 2 * HEAD_DIM
    for h in range(RET_HEADS):
        sl = slice(dv * h, dv * (h + 1))
        o = of_ref[:, sl] + ob_ref[:, sl]
        mu = jnp.mean(o, axis=-1, keepdims=True)
        var = jnp.mean(jnp.square(o - mu), axis=-1, keepdims=True)
        y = (o - mu) * lax.rsqrt(var + NORM_EPS) * gn_ref[:, sl]
        g = g_ref[:, sl]
        o_ref[:, sl] = (g * jax.nn.sigmoid(g) * y * beta_ref[:, sl]).astype(o_ref.dtype)


def ret_gate(o2, gate, gn_g, beta):
    _, r, w = o2.shape
    vec = pl.BlockSpec((1, w), lambda i: (0, 0))
    row = pl.BlockSpec((ROW_TILE, w), lambda i: (i, 0))
    return _pcall(
        _ret_gate_kernel,
        grid=(r // ROW_TILE,),
        in_specs=[pl.BlockSpec((None, ROW_TILE, w), lambda i: (0, i, 0)),
                  pl.BlockSpec((None, ROW_TILE, w), lambda i: (1, i, 0)), row, vec, vec],
        out_specs=row,
        out_shape=jax.ShapeDtypeStruct((r, w), BF16),
        compiler_params=_cparams(("parallel",)),
    )(o2, o2, gate, gn_g, beta)


def _short_conv_kernel(cur_ref, prev_ref, next_ref, w_ref, b_ref, z_ref, x1_ref, x2_ref, *, n_seq_tiles):
    j = pl.program_id(1)

    @pl.when(j < n_seq_tiles)
    def _():
        u = cur_ref[...]
        rows = lax.broadcasted_iota(jnp.int32, u.shape, 0)
        prev_row = jnp.where(j == 0, 0.0, prev_ref[7:8, :])
        next_row = jnp.where(j == n_seq_tiles - 1, 0.0, next_ref[0:1, :])
        up = jnp.where(rows == 0, prev_row, pltpu.roll(u, 1, 0))
        un = jnp.where(rows == ROW_TILE - 1, next_row, pltpu.roll(u, ROW_TILE - 1, 0))
        y = up * w_ref[0:1, :] + u * w_ref[1:2, :] + un * w_ref[2:3, :] + b_ref[...]
        z_ref[...] = y[:, :HY_WIDTH]
        x1_ref[...] = y[:, HY_WIDTH:2 * HY_WIDTH]
        x2_ref[...] = y[:, 2 * HY_WIDTH:]

    @pl.when(j >= n_seq_tiles)
    def _():
        z_ref[...] = jnp.zeros_like(z_ref)
        x1_ref[...] = jnp.zeros_like(x1_ref)
        x2_ref[...] = jnp.zeros_like(x2_ref)


def short_conv(p_hy, w, b, *, bsz, n_t, first_tile, n_seq_tiles, n_pad_tiles):
    w3 = 3 * HY_WIDTH
    sub = ROW_TILE // 8
    last_blk8 = p_hy.shape[0] // 8 - 1

    def cur(bb, j):
        return (bb * n_t + first_tile + jnp.minimum(j, n_seq_tiles - 1), 0)

    def prev(bb, j):
        return (jnp.maximum(cur(bb, j)[0] * sub - 1, 0), 0)

    def nxt(bb, j):
        return (jnp.minimum((cur(bb, j)[0] + 1) * sub, last_blk8), 0)

    kern = functools.partial(_short_conv_kernel, n_seq_tiles=n_seq_tiles)
    rows = bsz * n_pad_tiles * ROW_TILE
    return _pcall(
        kern,
        grid=(bsz, n_pad_tiles),
        in_specs=[pl.BlockSpec((ROW_TILE, w3), cur), pl.BlockSpec((8, w3), prev), pl.BlockSpec((8, w3), nxt),
                  pl.BlockSpec((3, w3), lambda bb, j: (0, 0)), pl.BlockSpec((1, w3), lambda bb, j: (0, 0))],
        out_specs=[pl.BlockSpec((ROW_TILE, HY_WIDTH), lambda bb, j: (bb * n_pad_tiles + j, 0))] * 3,
        out_shape=[jax.ShapeDtypeStruct((rows, HY_WIDTH), F32)] * 3,
        compiler_params=_cparams(("parallel", "arbitrary")),
    )(p_hy, p_hy, p_hy, w, b.reshape(1, w3))


def _filter_kernel(feat_ref, w1_ref, b1_ref, w2_ref, b2_ref, w3_ref, win_ref, h_ref, asum_ref):
    i = pl.program_id(0)
    hp = lax.Precision.HIGHEST
    h = jnp.sin(jnp.dot(feat_ref[...], w1_ref[...], precision=hp, preferred_element_type=F32) + b1_ref[...])
    h = jnp.sin(jnp.dot(h, w2_ref[...], precision=hp, preferred_element_type=F32) + b2_ref[...])
    h = jnp.dot(h, w3_ref[...], precision=hp, preferred_element_type=F32)
    win = win_ref[...]
    h = h * jnp.concatenate([win] * (2 * HY_ORDER), axis=-1)
    h_ref[...] = h
    rows = lax.broadcasted_iota(jnp.int32, h.shape, 0) + i * ROW_TILE
    cols = lax.broadcasted_iota(jnp.int32, h.shape, 1)
    is_bwd = (cols // HY_WIDTH) % 2 == 1
    part = jnp.sum(jnp.where(is_bwd & (rows == 0), 0.0, jnp.abs(h)), axis=0, keepdims=True)

    @pl.when(i == 0)
    def _():
        asum_ref[...] = part

    @pl.when(i > 0)
    def _():
        asum_ref[...] += part


def hyena_filter_taps(length, w1, b1, w2, b2, w3):
    t = jnp.arange(length, dtype=F32)
    t_norm = t / length
    f = jnp.linspace(1e-4, HY_BANDS - 1, HY_BANDS, dtype=F32)
    wt = 2.0 * math.pi * t_norm
    feats = jnp.concatenate([t_norm[:, None], jnp.cos(wt[:, None] * f), -jnp.sin(wt[:, None] * f)], axis=-1)
    feats = jnp.pad(feats, ((0, 0), (0, 128 - HY_EMB_DIM)))
    w1p = jnp.pad(w1, ((0, 128 - HY_EMB_DIM), (0, 0)))
    deltas = jnp.abs(jnp.linspace(HY_MIN_DECAY, HY_MAX_DECAY, HY_WIDTH, dtype=F32))
    window = jnp.exp(-t_norm[:, None] * deltas[None])
    wout = HY_ORDER * 2 * HY_WIDTH
    full = lambda shp: pl.BlockSpec(shp, lambda i: (0, 0))
    return _pcall(
        _filter_kernel,
        grid=(length // ROW_TILE,),
        in_specs=[pl.BlockSpec((ROW_TILE, 128), lambda i: (i, 0)), full((128, HY_FILT_HIDDEN)),
                  full((1, HY_FILT_HIDDEN)), full((HY_FILT_HIDDEN, HY_FILT_HIDDEN)), full((1, HY_FILT_HIDDEN)),
                  full((HY_FILT_HIDDEN, wout)), pl.BlockSpec((ROW_TILE, HY_WIDTH), lambda i: (i, 0))],
        out_specs=[pl.BlockSpec((ROW_TILE, wout), lambda i: (i, 0)), full((1, wout))],
        out_shape=[jax.ShapeDtypeStruct((length, wout), F32), jax.ShapeDtypeStruct((1, wout), F32)],
        compiler_params=_cparams(("arbitrary",)),
    )(feats, w1p, b1.reshape(1, -1), w2, b2.reshape(1, -1), w3, window)


class _Dft:
    def __init__(self, n1):
        n2 = FFT_N2
        assert n1 % 16 == 0
        self.n1, self.n = n1, n1 * n2
        self.half = n1 // 2
        self.nf = n1 // 2 + 1
        self.nfp = -(-self.nf // 16) * 16
        f1 = np.arange(self.nf)[:, None]
        ang = 2.0 * np.pi * f1 * np.arange(n1)[None, :] / n1
        s1 = np.zeros((2 * self.nfp, n1))
        s1[:self.nf] = np.cos(ang)
        s1[self.nfp:self.nfp + self.nf] = -np.sin(ang)
        self.s1_full = s1
        tw = 2.0 * np.pi * f1 * np.arange(n2)[None, :] / self.n
        self.tw_cos = np.cos(tw)[:, :, None]
        self.tw_sin = np.sin(tw)[:, :, None]
        a2 = 2.0 * np.pi * np.arange(n2)[:, None] * np.arange(n2)[None, :] / n2
        wc, ws = np.cos(a2), np.sin(a2)
        self.m_fwd = np.block([[wc, ws], [-ws, wc]])
        self.m_inv = np.block([[wc, -ws], [ws, wc]])
        wgt = np.full(self.nf, 2.0)
        wgt[0] = 1.0
        wgt[-1] = 1.0
        ango = 2.0 * np.pi * np.arange(self.half)[:, None] * np.arange(self.nf)[None, :] / n1
        self.s4_re = np.zeros((self.half, self.nfp))
        self.s4_im = np.zeros((self.half, self.nfp))
        self.s4_re[:, :self.nf] = np.cos(ango) * wgt / self.n
        self.s4_im[:, :self.nf] = -np.sin(ango) * wgt / self.n

    @staticmethod
    def const(a, dtype=BF16):
        return jnp.asarray(a, F32).astype(dtype)


def _fft_s1_kernel(m_ref, x_ref, o_ref):
    k_rows, n_sub, w = x_ref.shape
    x = x_ref[...].reshape(k_rows * n_sub, w)
    y = jnp.dot(m_ref[...], x.astype(BF16), preferred_element_type=F32)
    o_ref[...] = y.reshape(o_ref.shape)


def fft_stage1(mat, x3, *, n_batch, k_rows):
    w = x3.shape[-1]
    m = mat.shape[0]
    big = _Dft.const(np.kron(mat, np.eye(FFT_SUB)))
    return _pcall(
        _fft_s1_kernel,
        grid=(n_batch, FFT_N2 // FFT_SUB),
        in_specs=[pl.BlockSpec((m * FFT_SUB, k_rows * FFT_SUB), lambda b, j: (0, 0)),
                  pl.BlockSpec((k_rows, FFT_SUB, w), lambda b, j: (b, j, 0))],
        out_specs=pl.BlockSpec((None, m, FFT_SUB, w), lambda b, j: (b, 0, j, 0)),
        out_shape=jax.ShapeDtypeStruct((n_batch, m, FFT_N2, w), F32),
        compiler_params=_cparams(("parallel", "parallel")),
    )(big, x3)


def _twiddle(ar, ai, c, s):
    return ar * c + ai * s, ai * c - ar * s


def _fft_filter_mid_kernel(ar_ref, ai_ref, c_ref, s_ref, mf_ref, sc_ref, hb0_ref, kr_ref, ki_ref):
    br, bi = _twiddle(ar_ref[...], ai_ref[...], c_ref[...], s_ref[...])
    x = jnp.dot(mf_ref[...], jnp.concatenate([br, bi], axis=0).astype(BF16), preferred_element_type=F32)
    xr, xi = x[:FFT_N2], x[FFT_N2:]
    w = HY_WIDTH
    for o in range(HY_ORDER):
        fw = slice(2 * o * w, (2 * o + 1) * w)
        bw = slice((2 * o + 1) * w, (2 * o + 2) * w)
        oc = slice(o * w, (o + 1) * w)
        kr_ref[:, oc] = (xr[:, fw] + xr[:, bw] - hb0_ref[:, oc]) * sc_ref[:, oc]
        ki_ref[:, oc] = (xi[:, fw] - xi[:, bw]) * sc_ref[:, oc]


def fft_filter_mid(a, dft, kscale, hb0):
    cols = a.shape[-1]
    n2 = FFT_N2
    tw = pl.BlockSpec((None, n2, 1), lambda f: (f, 0, 0))
    blk = pl.BlockSpec((None, n2, cols // 2), lambda f: (f, 0, 0))
    vec = pl.BlockSpec((1, cols // 2), lambda f: (0, 0))
    return _pcall(
        _fft_filter_mid_kernel,
        grid=(dft.nf,),
        in_specs=[pl.BlockSpec((None, n2, cols), lambda f: (f, 0, 0)),
                  pl.BlockSpec((None, n2, cols), lambda f: (dft.nfp + f, 0, 0)), tw, tw,
                  pl.BlockSpec((2 * n2, 2 * n2), lambda f: (0, 0)), vec, vec],
        out_specs=[blk, blk],
        out_shape=[jax.ShapeDtypeStruct((dft.nf, n2, cols // 2), F32)] * 2,
        compiler_params=_cparams(("parallel",)),
    )(a, a, _Dft.const(dft.tw_cos, F32), _Dft.const(dft.tw_sin, F32), _Dft.const(dft.m_fwd), kscale, hb0)


def _fft_mid_kernel(ar_ref, ai_ref, c_ref, s_ref, mf_ref, mi_ref, kr_ref, ki_ref, er_ref, ei_ref, *, nf):
    for u in range(FFT_MID_F):
        f1 = pl.program_id(1) * FFT_MID_F + u

        @pl.when(f1 < nf)
        def _():
            c, s = c_ref[u], s_ref[u]
            br, bi = _twiddle(ar_ref[u], ai_ref[u], c, s)
            x = jnp.dot(mf_ref[...], jnp.concatenate([br, bi], axis=0).astype(BF16), preferred_element_type=F32)
            xr, xi = x[:FFT_N2], x[FFT_N2:]
            kr, ki = kr_ref[u], ki_ref[u]
            yr = xr * kr - xi * ki
            yi = xr * ki + xi * kr
            dd = jnp.dot(mi_ref[...], jnp.concatenate([yr, yi], axis=0).astype(BF16), preferred_element_type=F32)
            dr, di = dd[:FFT_N2], dd[FFT_N2:]
            er_ref[u] = dr * c - di * s
            ei_ref[u] = di * c + dr * s

        @pl.when(f1 >= nf)
        def _():
            er_ref[u] = jnp.zeros((FFT_N2, er_ref.shape[-1]), F32)
            ei_ref[u] = jnp.zeros((FFT_N2, ei_ref.shape[-1]), F32)


def fft_mid(a, dft, kr, ki, order):
    nb = a.shape[0]
    n2, w, g = FFT_N2, HY_WIDTH, FFT_MID_F
    nf, nfp = dft.nf, dft.nfp
    assert nfp % g == 0
    last = (nf - 1) // g
    fc = lambda f: jnp.minimum(f, last)
    tw = pl.BlockSpec((g, n2, 1), lambda b, f: (fc(f), 0, 0))
    mat = pl.BlockSpec((2 * n2, 2 * n2), lambda b, f: (0, 0))
    kf = pl.BlockSpec((g, n2, w), lambda b, f: (fc(f), 0, order))
    out = pl.BlockSpec((None, g, n2, w), lambda b, f: (b, f, 0, 0))
    return _pcall(
        functools.partial(_fft_mid_kernel, nf=nf),
        grid=(nb, nfp // g),
        in_specs=[pl.BlockSpec((None, g, n2, w), lambda b, f: (b, fc(f), 0, 0)),
                  pl.BlockSpec((None, g, n2, w), lambda b, f: (b, nfp // g + fc(f), 0, 0)),
                  tw, tw, mat, mat, kf, kf],
        out_specs=[out, out],
        out_shape=[jax.ShapeDtypeStruct((nb, nfp, n2, w), F32)] * 2,
        compiler_params=_cparams(("parallel", "arbitrary")),
    )(a, a, _Dft.const(dft.tw_cos, F32), _Dft.const(dft.tw_sin, F32), _Dft.const(dft.m_fwd),
      _Dft.const(dft.m_inv), kr, ki)


def _fft_s4_kernel(mr_ref, mi_ref, er_ref, ei_ref, gate_ref, z_ref, bias_ref, scale_ref, o_ref):
    nfp, n_sub, w = er_ref.shape
    er = er_ref[...].reshape(nfp * n_sub, w).astype(BF16)
    ei = ei_ref[...].reshape(nfp * n_sub, w).astype(BF16)
    y = (jnp.dot(mr_ref[...], er, preferred_element_type=F32)
         + jnp.dot(mi_ref[...], ei, preferred_element_type=F32)).reshape(o_ref.shape)
    o_ref[...] = gate_ref[...] * (y + bias_ref[...] * z_ref[...]) * scale_ref[...]


def fft_stage4(dft, er, ei, gate, z, bias_row, scale_row):
    nb = er.shape[0]
    m, nfp = dft.half, dft.nfp
    w = z.shape[-1]
    eye = np.eye(FFT_SUB)
    row = pl.BlockSpec((m, FFT_SUB, w), lambda b, j: (b, j, 0))
    vec = pl.BlockSpec((1, 1, w), lambda b, j: (0, 0, 0))
    mat = pl.BlockSpec((m * FFT_SUB, nfp * FFT_SUB), lambda b, j: (0, 0))
    spec = pl.BlockSpec((None, nfp, FFT_SUB, w), lambda b, j: (b, 0, j, 0))
    return _pcall(
        _fft_s4_kernel,
        grid=(nb, FFT_N2 // FFT_SUB),
        in_specs=[mat, mat, spec, spec, row, row, vec, vec],
        out_specs=row,
        out_shape=jax.ShapeDtypeStruct((nb * m, FFT_N2, w), F32),
        compiler_params=_cparams(("parallel", "parallel")),
    )(_Dft.const(np.kron(dft.s4_re, eye)), _Dft.const(np.kron(dft.s4_im, eye)), er, ei, gate, z,
      bias_row.reshape(1, 1, w), scale_row.reshape(1, 1, w))


def hyena_long_conv(z, x1, x2, filt, hbias, beta_hy, *, bsz, length, n1):
    dft = _Dft(n1)
    n2, w = FFT_N2, HY_WIDTH
    half = dft.half
    assert half >= 8 and length % n2 == 0
    taps, asum = filt
    asum = asum.reshape(HY_ORDER, 2, w)
    kscale = (1.0 / (asum[:, 0] + asum[:, 1] + NORM_EPS)).reshape(1, HY_ORDER * w)
    hb0 = taps[0].reshape(HY_ORDER, 2, w)[:, 1].reshape(1, HY_ORDER * w)
    k_taps = length // n2
    a_f = fft_stage1(dft.s1_full[:, :k_taps], taps.reshape(k_taps, n2, 2 * HY_ORDER * w), n_batch=1, k_rows=k_taps)
    kr, ki = fft_filter_mid(a_f[0], dft, kscale, hb0)
    s1 = dft.s1_full[:, :half]
    ones = jnp.ones((1, w), F32)
    shape3 = (bsz * half, n2, w)
    zc = z.reshape(shape3)
    for o, gate in enumerate((x1, x2)):
        a = fft_stage1(s1, zc, n_batch=bsz, k_rows=half)
        er, ei = fft_mid(a, dft, kr, ki, o)
        last = o == HY_ORDER - 1
        zc = fft_stage4(dft, er, ei, gate.reshape(shape3), zc, hbias[o].reshape(1, w),
                        beta_hy.reshape(1, w) if last else ones)
    return zc.reshape(bsz * half * n2, w)


def _rope_tables(n_ctx, n_lat):
    rows = n_lat // GRID_W
    row = jnp.repeat(jnp.arange(rows, dtype=F32), GRID_W)
    col = jnp.tile(jnp.arange(GRID_W, dtype=F32), rows)
    n_freq = HEAD_DIM // 4
    freqs = ROPE_THETA ** (-jnp.arange(n_freq, dtype=F32) / n_freq)
    ar = row[:, None] * freqs
    ac = col[:, None] * freqs
    cos = jnp.concatenate([jnp.cos(ar), jnp.cos(ar), jnp.cos(ac), jnp.cos(ac)], axis=-1)
    sin = jnp.concatenate([jnp.sin(ar), jnp.sin(ar), jnp.sin(ac), jnp.sin(ac)], axis=-1)
    cos = jnp.concatenate([jnp.ones((n_ctx, HEAD_DIM), F32), cos], axis=0)
    sin = jnp.concatenate([jnp.zeros((n_ctx, HEAD_DIM), F32), sin], axis=0)
    even = (np.arange(HEAD_DIM) // n_freq) % 2 == 0
    sin_a = jnp.where(even, -sin, 0.0)
    sin_b = jnp.where(even, 0.0, sin)
    tile2 = lambda a: jnp.concatenate([a, a], axis=-1)
    return tile2(cos), tile2(sin_a), tile2(sin_b)


def _attn_key_chunk(t):
    for tk in (1408, 1280, 1024, 768, 512, 256):
        if t % tk == 0:
            return tk
    raise ValueError(t)


def kernel(x, c, ctx, c_ctx, ada_w, ada_b, norm_g, w_in, w_out, mix_beta, gqa_qk_g, diff_lambda, diff_subln_g,
           ret_log_decay, ret_gn_g, hy_short_w, hy_short_b, hy_filt_w1, hy_filt_b1, hy_filt_w2, hy_filt_b2,
           hy_filt_w3, hy_bias, ffn_w_gate, ffn_w_up, ffn_w_down, moe_router, moe_w_gate, moe_w_up, moe_w_down):
    bsz, n_lat, d = x.shape
    n_ctx = ctx.shape[1]
    t = n_ctx + n_lat
    r = bsz * t
    n_t, n_c = t // ROW_TILE, n_ctx // ROW_TILE
    n_s = n_lat // ROW_TILE
    assert n_ctx % ROW_TILE == 0 and n_lat % ROW_TILE == 0 and r % 512 == 0
    tk = _attn_key_chunk(t)
    assert tk >= n_ctx

    rope_c, rope_sa, rope_sb = _rope_tables(n_ctx, n_lat)
    xs = jnp.concatenate([ctx, x], axis=1).reshape(r, d)

    cvec = jnp.stack([jnp.broadcast_to(c_ctx, c.shape), c], axis=1).reshape(2 * bsz, d)
    cvec = jax.nn.silu(cvec)
    cvec = jnp.pad(cvec, ((0, 16 - 2 * bsz), (0, 0))).astype(BF16)

    n1_lat = 2 * n_lat // FFT_N2
    ctx_pad_tiles = max(n_c, 1024 // ROW_TILE)
    n1_ctx = 2 * ctx_pad_tiles * ROW_TILE // FFT_N2

    for l in range(DEPTH):
        lambda_init = 0.8 - 0.6 * math.exp(-0.3 * l)
        mod = mm(cvec, ada_w, tn=1536, out_dtype=F32, lead=(l,))[:2 * bsz] + ada_b[l]
        mod = mod.reshape(2 * bsz, 6, 1, d)
        sh_m, sc_m, gt_m, sh_f, sc_f, gt_f = (mod[:, k] for k in range(6))
        beta = mix_beta[l].reshape(1, 4 * GROUP_WIDTH)
        beta_a, beta_b, beta_r, beta_d = (beta[:, GROUP_WIDTH * k:GROUP_WIDTH * (k + 1)] for k in range(4))

        (h,) = norm_mod(xs, norm_g[l, 0], sh_m, sc_m, n_t, n_c, (BF16,))
        p_attn = mm(h, w_in, tn=768, out_dtype=F32, lead=(l,), n_cols=COL_ATTN)
        p_hy = mm(h, w_in, tn=768, out_dtype=F32, lead=(l,), col_off=COL_HY // 768,
                  n_cols=IN_WIDTH - COL_HY)
        aq, ak, av, dq, dk, dv, rq, rk, rv, rg = prep_heads(p_attn, rope_c, rope_sa, rope_sb, gqa_qk_g[l], bsz, n_t)

        a_out = gqa_attention(aq, ak, av, beta_a, n_t=n_t, n_c=n_c, tk=tk)

        lamf = diff_lambda[l]
        lam_full = (jnp.exp(jnp.sum(lamf[0] * lamf[1])) - jnp.exp(jnp.sum(lamf[2] * lamf[3])) + lambda_init)
        b_out = diff_attention(dq, dk, dv, lam_full.reshape(1, 1), diff_subln_g[l].reshape(1, -1), beta_b,
                               1.0 - lambda_init, n_t=n_t, n_c=n_c, tk=tk)

        o2 = retention(rq, rk, rv, ret_log_decay[l], n_t=n_t, n_c=n_c)
        r_out = ret_gate(o2, rg, ret_gn_g[l].reshape(1, -1), beta_r)

        filt_args = (hy_filt_w1[l], hy_filt_b1[l], hy_filt_w2[l], hy_filt_b2[l], hy_filt_w3[l])
        z_l, x1_l, x2_l = short_conv(p_hy, hy_short_w[l], hy_short_b[l], bsz=bsz, n_t=n_t, first_tile=n_c,
                                     n_seq_tiles=n_s, n_pad_tiles=n_s)
        d_l = hyena_long_conv(z_l, x1_l, x2_l, hyena_filter_taps(n_lat, *filt_args), hy_bias[l], beta_d[0],
                              bsz=bsz, length=n_lat, n1=n1_lat)
        z_c, x1_c, x2_c = short_conv(p_hy, hy_short_w[l], hy_short_b[l], bsz=bsz, n_t=n_t, first_tile=0,
                                     n_seq_tiles=n_c, n_pad_tiles=ctx_pad_tiles)
        d_c = hyena_long_conv(z_c, x1_c, x2_c, hyena_filter_taps(n_ctx, *filt_args), hy_bias[l], beta_d[0],
                              bsz=bsz, length=n_ctx, n1=n1_ctx)
        d_out = jnp.concatenate([d_c.reshape(bsz, -1, HY_WIDTH)[:, :n_ctx], d_l.reshape(bsz, n_lat, HY_WIDTH)],
                                axis=1).reshape(r, HY_WIDTH).astype(BF16)

        mixed = jnp.concatenate([a_out, b_out, r_out, d_out], axis=-1)
        y = mm(mixed, w_out, tn=512, out_dtype=F32, lead=(l,))
        xs = gate_res(xs, y, norm_g[l, 1], gt_m, n_t, n_c)

        i = l // 2
        if l % 2 == 0:
            (f,) = norm_mod(xs, norm_g[l, 2], sh_f, sc_f, n_t, n_c, (BF16,))
            hid = swiglu_up(f, ffn_w_gate, ffn_w_up, tn=512, lead=(i,))
            f_out = mm(hid, ffn_w_down, tn=256, out_dtype=F32, lead=(i,))
            xs = gate_res(xs, f_out, norm_g[l, 3], gt_f, n_t, n_c)
        else:
            (f_in,) = norm_mod(xs, norm_g[l, 2], sh_f, sc_f, n_t, n_c, (F32,))
            top_i, top_w = router(f_in, moe_router[i].T)
            n_pad_rows = (2 * r // MOE_TM + N_EXPERTS) * MOE_TM
            row_token, row_weight, tile_expert, n_used, pos = route_tokens(top_i, top_w, n_pad_rows)
            y_sorted = moe_experts(f_in, moe_w_gate, moe_w_up, moe_w_down, i, row_token, row_weight,
                                   tile_expert, n_used)
            xs = moe_combine_gate_res(xs, y_sorted, pos, norm_g[l, 3], gt_f, n_t, n_c)

    return xs.reshape(bsz, t, d)[:, n_ctx:]
```

```python
import functools
import math

import numpy as np
import jax
import jax.numpy as jnp
from jax import lax
from jax.experimental import pallas as pl
from jax.experimental.pallas import tpu as pltpu

F32 = jnp.float32
BF16 = jnp.bfloat16

D_MODEL = 2048
DEPTH = 4
GRID_W = 64
HEAD_DIM = 64
ROPE_THETA = 10000.0
NORM_EPS = 1e-6
GROUP_WIDTH = 512
GQA_Q_HEADS = 8
GQA_KV_HEADS = 2
GQA_REP = 4
DIFF_HEADS = 4
RET_HEADS = 4
HY_WIDTH = 512
HY_ORDER = 2
HY_EMB_DIM = 33
HY_BANDS = 16
HY_FILT_HIDDEN = 64
HY_MIN_DECAY = math.log(1e-2) / 1.5
HY_MAX_DECAY = math.log(1e-2) / 0.3
N_EXPERTS = 8

COL_ATTN = 3840
COL_DIFF = 768
COL_RET = 2304
COL_RET_GATE = 3328
COL_HY = 3840
IN_WIDTH = 5376

ROW_TILE = 256
MOE_TM = 1024
MOE_TF = 512
FFT_N2 = 128
FFT_MID_F = 2
FFT_SUB = 8
VMEM_LIMIT_MB = 48


def _cparams(sem, vmem_mb=VMEM_LIMIT_MB):
    return pltpu.CompilerParams(dimension_semantics=sem, vmem_limit_bytes=vmem_mb * 1024 * 1024)


def _pcall(kern, **kw):
    fn = kern.func if isinstance(kern, functools.partial) else kern
    return pl.pallas_call(kern, name=fn.__name__.strip("_"), **kw)


def _pick_tm(m, k, max_tile_mb):
    for tm in (1536, 1280, 1024, 768, 512, 256, 128, 16):
        if m % tm == 0 and tm * k * 2 <= max_tile_mb * 2 ** 20:
            return tm
    raise ValueError((m, k))


def _mm_kernel(a_ref, b_ref, o_ref, bb_ref):
    @pl.when(pl.program_id(1) == 0)
    def _():
        bb_ref[...] = b_ref[...].astype(BF16)

    o_ref[...] = jnp.dot(a_ref[...], bb_ref[...], preferred_element_type=F32).astype(o_ref.dtype)


def _b_spec(b, lead, k, tn, col_off):
    nlead = len(lead)
    return pl.BlockSpec((None,) * nlead + (k, tn), lambda j, i: tuple(lead) + (0, j + col_off))


def mm(a, b, *, tn, out_dtype, lead=(), col_off=0, n_cols=None, max_tile_mb=9):
    m, k = a.shape
    n = b.shape[-1] if n_cols is None else n_cols
    tm = _pick_tm(m, k, max_tile_mb)
    assert n % tn == 0 and a.dtype == BF16, (n, tn, a.dtype)
    return _pcall(
        _mm_kernel,
        grid=(n // tn, m // tm),
        in_specs=[pl.BlockSpec((tm, k), lambda j, i: (i, 0)), _b_spec(b, lead, k, tn, col_off)],
        out_specs=pl.BlockSpec((tm, tn), lambda j, i: (i, j)),
        out_shape=jax.ShapeDtypeStruct((m, n), out_dtype),
        scratch_shapes=[pltpu.VMEM((k, tn), BF16)],
        compiler_params=_cparams(("parallel", "arbitrary")),
    )(a, b)


def _mm_parts_kernel(*refs, n_parts):
    a_refs, (b_ref, o_ref, bb_ref) = refs[:n_parts], refs[n_parts:]

    @pl.when(pl.program_id(1) == 0)
    def _():
        bb_ref[...] = b_ref[...].astype(BF16)

    kp = a_refs[0].shape[1]
    acc = jnp.dot(a_refs[0][...], bb_ref[0:kp, :], preferred_element_type=F32)
    for g in range(1, n_parts):
        acc += jnp.dot(a_refs[g][...], bb_ref[g * kp:(g + 1) * kp, :], preferred_element_type=F32)
    o_ref[...] = acc


def mm_parts(parts, b, *, tn, lead):
    m, kp = parts[0].shape
    k = kp * len(parts)
    n = b.shape[-1]
    tm = _pick_tm(m, k, 9)
    row = pl.BlockSpec((tm, kp), lambda j, i: (i, 0))
    return _pcall(
        functools.partial(_mm_parts_kernel, n_parts=len(parts)),
        grid=(n // tn, m // tm),
        in_specs=[row] * len(parts) + [_b_spec(b, lead, k, tn, 0)],
        out_specs=pl.BlockSpec((tm, tn), lambda j, i: (i, j)),
        out_shape=jax.ShapeDtypeStruct((m, n), F32),
        scratch_shapes=[pltpu.VMEM((k, tn), BF16)],
        compiler_params=_cparams(("parallel", "arbitrary")),
    )(*parts, b)


def _swiglu_up_kernel(a_ref, wg_ref, wu_ref, o_ref, bg_ref, bu_ref):
    @pl.when(pl.program_id(1) == 0)
    def _():
        bg_ref[...] = wg_ref[...].astype(BF16)
        bu_ref[...] = wu_ref[...].astype(BF16)

    a = a_ref[...]
    g = jnp.dot(a, bg_ref[...], preferred_element_type=F32)
    u = jnp.dot(a, bu_ref[...], preferred_element_type=F32)
    o_ref[...] = (g * jax.nn.sigmoid(g) * u).astype(o_ref.dtype)


def swiglu_up(a, wg, wu, *, tn, lead):
    m, k = a.shape
    n = wg.shape[-1]
    tm = _pick_tm(m, k, 4)
    assert n % tn == 0
    return _pcall(
        _swiglu_up_kernel,
        grid=(n // tn, m // tm),
        in_specs=[pl.BlockSpec((tm, k), lambda j, i: (i, 0)),
                  _b_spec(wg, lead, k, tn, 0), _b_spec(wu, lead, k, tn, 0)],
        out_specs=pl.BlockSpec((tm, tn), lambda j, i: (i, j)),
        out_shape=jax.ShapeDtypeStruct((m, n), BF16),
        scratch_shapes=[pltpu.VMEM((k, tn), BF16)] * 2,
        compiler_params=_cparams(("parallel", "arbitrary")),
    )(a, wg, wu)


def _seg_map(n_t, n_c):
    return lambda i: (2 * (i // n_t) + jnp.where(i % n_t >= n_c, 1, 0), 0, 0)


def _rms(x):
    return x * lax.rsqrt(jnp.mean(x * x, axis=-1, keepdims=True) + NORM_EPS)


def _norm_mod_kernel(x_ref, g_ref, sh_ref, sc_ref, *o_refs):
    y = _rms(x_ref[...]) * g_ref[...]
    y = y * (1.0 + sc_ref[...]) + sh_ref[...]
    for o_ref in o_refs:
        o_ref[...] = y.astype(o_ref.dtype)


def norm_mod(x, g, shift, scale, n_t, n_c, out_dtypes):
    r, d = x.shape
    seg = _seg_map(n_t, n_c)
    row = pl.BlockSpec((ROW_TILE, d), lambda i: (i, 0))
    outs = _pcall(
        _norm_mod_kernel,
        grid=(r // ROW_TILE,),
        in_specs=[row, pl.BlockSpec((1, d), lambda i: (0, 0)),
                  pl.BlockSpec((None, 1, d), seg), pl.BlockSpec((None, 1, d), seg)],
        out_specs=[row] * len(out_dtypes),
        out_shape=[jax.ShapeDtypeStruct((r, d), dt) for dt in out_dtypes],
        compiler_params=_cparams(("parallel",)),
    )(x, g.reshape(1, d), shift, scale)
    return outs


def _gate_res_kernel(x_ref, y_ref, g_ref, gt_ref, *rest):
    x = x_ref[...] + gt_ref[...] * (_rms(y_ref[...]) * g_ref[...])
    if len(rest) == 1:
        rest[0][...] = x
    else:
        g2_ref, sh_ref, sc_ref, o_ref, h_ref = rest
        o_ref[...] = x
        h_ref[...] = (_rms(x) * g2_ref[...] * (1.0 + sc_ref[...]) + sh_ref[...]).astype(h_ref.dtype)


def gate_res(x, y, g, gate, n_t, n_c, nxt=None):
    r, d = x.shape
    seg = _seg_map(n_t, n_c)
    row = pl.BlockSpec((ROW_TILE, d), lambda i: (i, 0))
    vec = pl.BlockSpec((1, d), lambda i: (0, 0))
    mod = pl.BlockSpec((None, 1, d), seg)
    in_specs, args = [row, row, vec, mod], [x, y, g.reshape(1, d), gate]
    out_specs, out_shape = row, jax.ShapeDtypeStruct((r, d), F32)
    if nxt is not None:
        g2, shift, scale, dtype = nxt
        in_specs, args = in_specs + [vec, mod, mod], args + [g2.reshape(1, d), shift, scale]
        out_specs, out_shape = [row, row], [out_shape, jax.ShapeDtypeStruct((r, d), dtype)]
    return _pcall(
        _gate_res_kernel,
        grid=(r // ROW_TILE,),
        in_specs=in_specs,
        out_specs=out_specs,
        out_shape=out_shape,
        compiler_params=_cparams(("parallel",)),
    )(*args)


def _router_kernel(f_ref, rt_ref, idx_ref, w_ref):
    lt = lax.dot_general(rt_ref[...], f_ref[...], (((1,), (1,)), ((), ())),
                         precision=lax.Precision.HIGHEST, preferred_element_type=F32)
    e_idx = lax.broadcasted_iota(jnp.int32, lt.shape, 0)
    m1 = jnp.max(lt, axis=0, keepdims=True)
    i1 = jnp.min(jnp.where(lt == m1, e_idx, N_EXPERTS), axis=0, keepdims=True)
    rest = jnp.where(e_idx == i1, -jnp.inf, lt)
    m2 = jnp.max(rest, axis=0, keepdims=True)
    i2 = jnp.min(jnp.where(rest == m2, e_idx, N_EXPERTS), axis=0, keepdims=True)
    e2 = jnp.exp(m2 - m1)
    idx_ref[...] = jnp.concatenate([i1, i2], axis=0)
    w_ref[...] = jnp.concatenate([1.0 / (1.0 + e2), e2 / (1.0 + e2)], axis=0)


def router(f, router_t):
    r, d = f.shape
    return _pcall(
        _router_kernel,
        grid=(r // ROW_TILE,),
        in_specs=[pl.BlockSpec((ROW_TILE, d), lambda i: (i, 0)),
                  pl.BlockSpec((N_EXPERTS, d), lambda i: (0, 0))],
        out_specs=[pl.BlockSpec((2, ROW_TILE), lambda i: (0, i))] * 2,
        out_shape=[jax.ShapeDtypeStruct((2, r), jnp.int32), jax.ShapeDtypeStruct((2, r), F32)],
        compiler_params=_cparams(("parallel",)),
    )(f, router_t)


def _row_copy(src_hbm, dst, src_row, dst_row, sem):
    return pltpu.make_async_copy(src_hbm.at[pl.ds(src_row, 1), :], dst.at[pl.ds(dst_row, 1), :], sem)


def _gather_rows(src_hbm, dst, idx_ref, base, n_rows, sem):
    def start(r, carry):
        _row_copy(src_hbm, dst, idx_ref[base + r], r, sem).start()
        return carry

    def wait(r, carry):
        _row_copy(src_hbm, dst, idx_ref[base + r], r, sem).wait()
        return carry

    lax.fori_loop(0, n_rows, start, 0, unroll=8)
    lax.fori_loop(0, n_rows, wait, 0, unroll=8)


def route_tokens(idx, w, n_pad_rows):
    r = idx.shape[1]
    n_assign = 2 * r
    n_tiles = n_pad_rows // MOE_TM
    e_flat = idx.reshape(n_assign)
    w_flat = w.reshape(n_assign)
    counts = jnp.sum((e_flat[:, None] == jnp.arange(N_EXPERTS)[None, :]).astype(jnp.int32), axis=0)
    padded = (counts + MOE_TM - 1) // MOE_TM * MOE_TM
    group_end = jnp.cumsum(padded)
    group_start = group_end - padded
    sorted_start = jnp.cumsum(counts) - counts
    order = jnp.argsort(e_flat, stable=True).astype(jnp.int32)
    inv = jnp.argsort(order).astype(jnp.int32)
    pos = group_start[e_flat] + (inv - sorted_start[e_flat])
    rows = jnp.arange(n_pad_rows, dtype=jnp.int32)
    row_e = jnp.minimum(jnp.sum((rows[:, None] >= group_end[None, :]).astype(jnp.int32), axis=1), N_EXPERTS - 1)
    rank = rows - group_start[row_e]
    valid = rank < counts[row_e]
    src = order[jnp.clip(sorted_start[row_e] + rank, 0, n_assign - 1)]
    row_token = jnp.where(valid, src % r, 0)
    row_weight = jnp.where(valid, w_flat[src], 0.0)
    n_used = group_end[-1] // MOE_TM
    tile_start = jnp.arange(n_tiles, dtype=jnp.int32) * MOE_TM
    tile_start = jnp.minimum(tile_start, group_end[-1] - 1)
    tile_expert = jnp.sum((tile_start[:, None] >= group_end[None, :]).astype(jnp.int32), axis=1)
    return (row_token, row_weight.reshape(n_pad_rows, 1), tile_expert.astype(jnp.int32),
            n_used.astype(jnp.int32).reshape(1), pos)


def _moe_expert_kernel(te_ref, nu_ref, tok_ref, f_hbm, wg_ref, wu_ref, wd_ref, rw_ref, o_ref, xb_ref, sem):
    i, f = pl.program_id(0), pl.program_id(1)
    used = i < nu_ref[0]

    @pl.when(jnp.logical_and(used, f == 0))
    def _():
        _gather_rows(f_hbm, o_ref, tok_ref, i * MOE_TM, MOE_TM, sem)
        xb_ref[...] = o_ref[...].astype(BF16)

    @pl.when(f == 0)
    def _():
        o_ref[...] = jnp.zeros_like(o_ref)

    @pl.when(used)
    def _():
        x = xb_ref[...]
        g = jnp.dot(x, wg_ref[...], preferred_element_type=F32)
        u = jnp.dot(x, wu_ref[...], preferred_element_type=F32)
        h = (g * jax.nn.sigmoid(g) * u).astype(BF16)
        o_ref[...] += jnp.dot(h, wd_ref[...], preferred_element_type=F32)

    @pl.when(jnp.logical_and(used, f == pl.num_programs(1) - 1))
    def _():
        o_ref[...] = o_ref[...] * rw_ref[...]


def moe_experts(f_in, wg, wu, wd, row_token, row_weight, tile_expert, n_used):
    r, d = f_in.shape
    nf = wg.shape[1]
    n_pad_rows = row_token.shape[0]
    nt = n_pad_rows // MOE_TM

    def fi(i, f, nu):
        return jnp.where(i < nu[0], f, nf - 1)

    grid_spec = pltpu.PrefetchScalarGridSpec(
        num_scalar_prefetch=3,
        grid=(nt, nf),
        in_specs=[pl.BlockSpec(memory_space=pl.ANY),
                  pl.BlockSpec((None, None, d, MOE_TF), lambda i, f, te, nu, tok: (te[i], fi(i, f, nu), 0, 0)),
                  pl.BlockSpec((None, None, d, MOE_TF), lambda i, f, te, nu, tok: (te[i], fi(i, f, nu), 0, 0)),
                  pl.BlockSpec((None, MOE_TF, d), lambda i, f, te, nu, tok: (te[i], fi(i, f, nu), 0)),
                  pl.BlockSpec((MOE_TM, 1), lambda i, f, te, nu, tok: (i, 0))],
        out_specs=pl.BlockSpec((MOE_TM, d), lambda i, f, te, nu, tok: (i, 0)),
        scratch_shapes=[pltpu.VMEM((MOE_TM, d), BF16), pltpu.SemaphoreType.DMA(())],
    )
    return _pcall(
        _moe_expert_kernel,
        grid_spec=grid_spec,
        out_shape=jax.ShapeDtypeStruct((n_pad_rows, d), F32),
        compiler_params=_cparams(("arbitrary", "arbitrary")),
    )(tile_expert, n_used, row_token, f_in, wg, wu, wd, row_weight)


def _moe_combine_kernel(pos_ref, y_hbm, x_ref, g_ref, gt_ref, *rest):
    buf_ref, sem = rest[-2:]
    i = pl.program_id(0)
    for k in range(2):
        _gather_rows(y_hbm, buf_ref.at[k], pos_ref, (2 * i + k) * ROW_TILE, ROW_TILE, sem)
    y = buf_ref[0] + buf_ref[1]
    x = x_ref[...] + gt_ref[...] * (_rms(y) * g_ref[...])
    if len(rest) == 3:
        rest[0][...] = x
    else:
        g2_ref, sh_ref, sc_ref, o_ref, h_ref = rest[:5]
        o_ref[...] = x
        h_ref[...] = (_rms(x) * g2_ref[...] * (1.0 + sc_ref[...]) + sh_ref[...]).astype(h_ref.dtype)


def moe_combine_gate_res(x, y_sorted, pos, g, gate, n_t, n_c, nxt=None):
    r, d = x.shape
    nrt = r // ROW_TILE
    pos_tiles = pos.reshape(2, nrt, ROW_TILE).transpose(1, 0, 2).reshape(2 * r)
    seg = _seg_map(n_t, n_c)
    row = pl.BlockSpec((ROW_TILE, d), lambda i, p: (i, 0))
    vec = pl.BlockSpec((1, d), lambda i, p: (0, 0))
    mod = pl.BlockSpec((None, 1, d), lambda i, p: seg(i))
    in_specs, args = [pl.BlockSpec(memory_space=pl.ANY), row, vec, mod], [y_sorted, x, g.reshape(1, d), gate]
    out_specs, out_shape = row, jax.ShapeDtypeStruct((r, d), F32)
    if nxt is not None:
        g2, shift, scale, dtype = nxt
        in_specs, args = in_specs + [vec, mod, mod], args + [g2.reshape(1, d), shift, scale]
        out_specs, out_shape = [row, row], [out_shape, jax.ShapeDtypeStruct((r, d), dtype)]
    grid_spec = pltpu.PrefetchScalarGridSpec(
        num_scalar_prefetch=1,
        grid=(nrt,),
        in_specs=in_specs,
        out_specs=out_specs,
        scratch_shapes=[pltpu.VMEM((2, ROW_TILE, d), F32), pltpu.SemaphoreType.DMA(())],
    )
    return _pcall(
        _moe_combine_kernel,
        grid_spec=grid_spec,
        out_shape=out_shape,
        compiler_params=_cparams(("arbitrary",)),
    )(pos_tiles, *args)


def _rope(x, c, sa, sb):
    return x * c + pltpu.roll(x, 112, 1) * sa + pltpu.roll(x, 16, 1) * sb


def _head_rms(x, g, bd):
    sq = x * x
    hi = sq.astype(BF16)
    lo = (sq - hi.astype(F32)).astype(BF16)
    ms = jnp.dot(hi, bd, preferred_element_type=F32) + jnp.dot(lo, bd, preferred_element_type=F32)
    return x * lax.rsqrt(ms + NORM_EPS) * g


def _prep_kernel(p_ref, c_ref, sa_ref, sb_ref, gq_ref, gk_ref, bd_ref,
                 aq_ref, ak_ref, av_ref, dq_ref, dk_ref, dv_ref, rq_ref, rk_ref, rv_ref, rg_ref):
    c, sa, sb = c_ref[...], sa_ref[...], sb_ref[...]
    bd = bd_ref[...]
    scale = HEAD_DIM ** -0.5
    qscale = scale * math.log2(math.e)

    def chunk(j):
        return p_ref[:, 128 * j:128 * (j + 1)]

    def put_heads(ref, first, val):
        ref[first] = val[:, :HEAD_DIM].astype(ref.dtype)
        ref[first + 1] = val[:, HEAD_DIM:].astype(ref.dtype)

    for j in range(4):
        put_heads(aq_ref, 2 * j, _rope(_head_rms(chunk(j), gq_ref[...], bd), c, sa, sb) * qscale)
    put_heads(ak_ref, 0, _rope(_head_rms(chunk(4), gk_ref[...], bd), c, sa, sb))
    one64 = (lax.broadcasted_iota(jnp.int32, (ROW_TILE, HEAD_DIM), 1) == 0).astype(av_ref.dtype)
    one128 = (lax.broadcasted_iota(jnp.int32, (ROW_TILE, 2 * HEAD_DIM), 1) == 0).astype(dv_ref.dtype)
    v_gqa = chunk(5)
    for h in range(GQA_KV_HEADS):
        av_ref[h, :, :HEAD_DIM] = v_gqa[:, HEAD_DIM * h:HEAD_DIM * (h + 1)].astype(av_ref.dtype)
        av_ref[h, :, HEAD_DIM:] = one64
    for j in range(4):
        put_heads(dq_ref, 2 * j, _rope(chunk(6 + j), c, sa, sb) * qscale)
        put_heads(dk_ref, 2 * j, _rope(chunk(10 + j), c, sa, sb))
        dv_ref[j, :, :2 * HEAD_DIM] = chunk(14 + j).astype(dv_ref.dtype)
        dv_ref[j, :, 2 * HEAD_DIM:] = one128
    for j in range(2):
        put_heads(rq_ref, 2 * j, _rope(chunk(18 + j), c, sa, sb))
        put_heads(rk_ref, 2 * j, _rope(chunk(20 + j), c, sa, sb) * scale)
    for j in range(4):
        rv_ref[j] = chunk(22 + j).astype(rv_ref.dtype)
    rg_ref[...] = p_ref[:, COL_RET_GATE:COL_ATTN]


def prep_heads(p_attn, rope_c, rope_sa, rope_sb, qk_g, bsz, n_t):
    t = n_t * ROW_TILE
    bd = jnp.asarray(np.kron(np.eye(2), np.full((HEAD_DIM, HEAD_DIM), 1.0 / HEAD_DIM)), F32).astype(BF16)
    gq = jnp.tile(qk_g[0], 2).reshape(1, 128)
    gk = jnp.tile(qk_g[1], 2).reshape(1, 128)
    tab = pl.BlockSpec((ROW_TILE, 128), lambda i: (i % n_t, 0))
    vec = pl.BlockSpec((1, 128), lambda i: (0, 0))

    def heads(nh, dh):
        return (pl.BlockSpec((None, nh, ROW_TILE, dh), lambda i: (i // n_t, 0, i % n_t, 0)),
                jax.ShapeDtypeStruct((bsz, nh, t, dh), BF16))

    outs = [heads(8, 64), heads(2, 64), heads(2, 128), heads(8, 64), heads(8, 64), heads(4, 256),
            heads(4, 64), heads(4, 64), heads(4, 128)]
    return _pcall(
        _prep_kernel,
        grid=(bsz * n_t,),
        in_specs=[pl.BlockSpec((ROW_TILE, COL_ATTN), lambda i: (i, 0)), tab, tab, tab, vec, vec,
                  pl.BlockSpec((128, 128), lambda i: (0, 0))],
        out_specs=[o[0] for o in outs] + [pl.BlockSpec((ROW_TILE, GROUP_WIDTH), lambda i: (i, 0))],
        out_shape=[o[1] for o in outs] + [jax.ShapeDtypeStruct((bsz * t, GROUP_WIDTH), F32)],
        compiler_params=_cparams(("parallel",)),
    )(p_attn, rope_c, rope_sa, rope_sb, gq, gk, bd)


def _nt_dot(a, b):
    return lax.dot_general(a, b, (((1,), (1,)), ((), ())), preferred_element_type=F32)


def _softmax_chunks(score_fn, v_ref, s_ref, rows, dv, n_chunks, tk, n_ctx_keys, is_ctx):
    def update(s, start, carry):
        m, acc = carry
        m_new = jnp.maximum(m, jnp.max(s, axis=-1, keepdims=True))
        p = jnp.exp2(s - m_new)
        acc = jnp.exp2(m - m_new) * acc + jnp.dot(p.astype(BF16), v_ref[pl.ds(start, tk), :],
                                                  preferred_element_type=F32)
        return m_new, acc

    init = (jnp.full((rows, 1), -1e30, F32), jnp.zeros((rows, v_ref.shape[-1]), F32))

    def ctx_tile():
        s = score_fn(0, tk)
        col = lax.broadcasted_iota(jnp.int32, s.shape, 1)
        return update(jnp.where(col < n_ctx_keys, s, -1e30), 0, init)

    def lat_tile():
        stats = init
        s_ref[0] = score_fn(0, tk)
        for j in range(n_chunks):
            if j + 1 < n_chunks:
                s_ref[(j + 1) % 2] = score_fn((j + 1) * tk, tk)
            stats = update(s_ref[j % 2], j * tk, stats)
        return stats

    _, acc = lax.cond(is_ctx, ctx_tile, lat_tile)
    return acc[:, :dv] / acc[:, dv:dv + 1]


def _gqa_kernel(q_ref, k_ref, v_ref, beta_ref, o_ref, s_ref, *, tq, tk, n_chunks, n_ctx_tiles, n_ctx_keys):
    is_ctx = pl.program_id(2) < n_ctx_tiles
    q = q_ref[...].reshape(GQA_REP * tq, HEAD_DIM)

    def scores(start, size):
        return _nt_dot(q, k_ref[pl.ds(start, size), :])

    o = _softmax_chunks(scores, v_ref, s_ref, GQA_REP * tq, HEAD_DIM, n_chunks, tk, n_ctx_keys, is_ctx)
    for r in range(GQA_REP):
        sl = slice(HEAD_DIM * r, HEAD_DIM * (r + 1))
        o_ref[:, sl] = (o[r * tq:(r + 1) * tq] * beta_ref[:, sl]).astype(o_ref.dtype)


def gqa_attention(q, k, v, beta, *, n_t, n_c, tk):
    bsz, _, t, _ = q.shape
    tq = ROW_TILE
    kern = functools.partial(_gqa_kernel, tq=tq, tk=tk, n_chunks=t // tk, n_ctx_tiles=n_c,
                             n_ctx_keys=n_c * ROW_TILE)
    kspec = pl.BlockSpec((None, None, t, HEAD_DIM), lambda b, g, i: (b, g, 0, 0))
    vspec = pl.BlockSpec((None, None, t, v.shape[-1]), lambda b, g, i: (b, g, 0, 0))
    w = GQA_REP * HEAD_DIM
    return _pcall(
        kern,
        grid=(bsz, GQA_KV_HEADS, n_t),
        in_specs=[pl.BlockSpec((None, GQA_REP, tq, HEAD_DIM), lambda b, g, i: (b, g, i, 0)), kspec, vspec,
                  pl.BlockSpec((1, w), lambda b, g, i: (0, g))],
        out_specs=pl.BlockSpec((tq, w), lambda b, g, i: (b * n_t + i, g)),
        out_shape=jax.ShapeDtypeStruct((bsz * t, GROUP_WIDTH), BF16),
        scratch_shapes=[pltpu.VMEM((2, GQA_REP * tq, tk), F32)],
        compiler_params=_cparams(("parallel", "parallel", "arbitrary")),
    )(q, k, v, beta)


def _diff_kernel(q_ref, k_ref, v_ref, lam_ref, g_ref, beta_ref, o_ref, s_ref, *, tq, tk, n_chunks, n_ctx_tiles,
                 n_ctx_keys, out_scale):
    is_ctx = pl.program_id(2) < n_ctx_tiles
    q0, q1 = q_ref[0], q_ref[1]

    def scores(start, size):
        return jnp.concatenate([_nt_dot(q0, k_ref[0, pl.ds(start, size), :]),
                                _nt_dot(q1, k_ref[1, pl.ds(start, size), :])], axis=0)

    a = _softmax_chunks(scores, v_ref, s_ref, 2 * tq, 2 * HEAD_DIM, n_chunks, tk, n_ctx_keys, is_ctx)
    o = a[:tq] - lam_ref[...] * a[tq:]
    o = _rms(o) * g_ref[...] * out_scale
    o_ref[...] = (o * beta_ref[...]).astype(o_ref.dtype)


def diff_attention(q, k, v, lam, subln_g, beta, out_scale, *, n_t, n_c, tk):
    bsz, _, t, _ = q.shape
    tq = ROW_TILE
    dv = 2 * HEAD_DIM
    kern = functools.partial(_diff_kernel, tq=tq, tk=tk, n_chunks=t // tk, n_ctx_tiles=n_c,
                             n_ctx_keys=n_c * ROW_TILE, out_scale=out_scale)
    return _pcall(
        kern,
        grid=(bsz, DIFF_HEADS, n_t),
        in_specs=[pl.BlockSpec((None, 2, tq, HEAD_DIM), lambda b, h, i: (b, h, i, 0)),
                  pl.BlockSpec((None, 2, t, HEAD_DIM), lambda b, h, i: (b, h, 0, 0)),
                  pl.BlockSpec((None, None, t, v.shape[-1]), lambda b, h, i: (b, h, 0, 0)),
                  pl.BlockSpec((1, 1), lambda b, h, i: (0, 0)),
                  pl.BlockSpec((1, dv), lambda b, h, i: (0, 0)),
                  pl.BlockSpec((1, dv), lambda b, h, i: (0, h))],
        out_specs=pl.BlockSpec((tq, dv), lambda b, h, i: (b * n_t + i, h)),
        out_shape=jax.ShapeDtypeStruct((bsz * t, GROUP_WIDTH), BF16),
        scratch_shapes=[pltpu.VMEM((2, 2 * tq, tk), F32)],
        compiler_params=_cparams(("parallel", "parallel", "arbitrary")),
    )(q, k, v, lam, subln_g, beta)


def _ret_kernel(lg_ref, q_ref, k_ref, v_ref, o_ref, state_ref, decay_ref, xi_ref, zeta_ref):
    d = pl.program_id(1)
    c = ROW_TILE
    dv = v_ref.shape[-1]

    @pl.when(pl.program_id(2) == 0)
    def _():
        state_ref[...] = jnp.zeros_like(state_ref)
        fwd = d == 0
        ii = lax.broadcasted_iota(jnp.int32, (c, c), 0)
        jj = lax.broadcasted_iota(jnp.int32, (c, c), 1)
        rel = jnp.where(fwd, ii - jj, jj - ii).astype(F32)
        pos = lax.broadcasted_iota(jnp.int32, (c, 1), 0).astype(F32)
        for h in range(RET_HEADS):
            lg = lg_ref[h]
            decay_ref[h] = jnp.where(rel >= 0, jnp.exp(jnp.maximum(rel, 0.0) * lg), 0.0)
            xi_ref[h] = jnp.exp(jnp.where(fwd, pos + 1.0, c - pos) * lg)
            zeta_ref[h] = jnp.exp(jnp.where(fwd, c - 1.0 - pos, pos) * lg)

    for h in range(RET_HEADS):
        q, k, v = q_ref[h], k_ref[h], v_ref[h]
        state = state_ref[h]
        scores = _nt_dot(q, k) * decay_ref[h]
        inner = jnp.dot(scores.astype(BF16), v, preferred_element_type=F32)
        cross = jnp.dot(q, state.astype(BF16), preferred_element_type=F32) * xi_ref[h]
        o_ref[:, dv * h:dv * (h + 1)] = inner + cross
        kz = (k.astype(F32) * zeta_ref[h]).astype(BF16)
        upd = lax.dot_general(kz, v, (((0,), (0,)), ((), ())), preferred_element_type=F32)
        state_ref[h] = jnp.exp(c * lg_ref[h]) * state + upd


def retention(q, k, v, log_decay, *, n_t, n_c):
    bsz, nh, t, dk = q.shape
    dv = v.shape[-1]

    def blk(d, j):
        back = jnp.where(j < n_c, n_c - 1 - j, n_t - 1 - (j - n_c))
        return jnp.where(d == 0, j, back)

    qk = pl.BlockSpec((None, nh, ROW_TILE, dk), lambda b, d, j: (b, 0, blk(d, j), 0))
    return _pcall(
        _ret_kernel,
        grid=(bsz, 2, n_t),
        in_specs=[pl.BlockSpec((None, nh, 1, 1), lambda b, d, j: (d, 0, 0, 0)), qk, qk,
                  pl.BlockSpec((None, nh, ROW_TILE, dv), lambda b, d, j: (b, 0, blk(d, j), 0))],
        out_specs=pl.BlockSpec((None, ROW_TILE, nh * dv), lambda b, d, j: (d, b * n_t + blk(d, j), 0)),
        out_shape=jax.ShapeDtypeStruct((2, bsz * t, nh * dv), F32),
        scratch_shapes=[pltpu.VMEM((nh, dk, dv), F32), pltpu.VMEM((nh, ROW_TILE, ROW_TILE), F32),
                        pltpu.VMEM((nh, ROW_TILE, 1), F32), pltpu.VMEM((nh, ROW_TILE, 1), F32)],
        compiler_params=_cparams(("parallel", "arbitrary", "arbitrary")),
    )(log_decay.reshape(2, RET_HEADS, 1, 1), q, k, v)


def _ret_gate_kernel(of_ref, ob_ref, g_ref, gn_ref, beta_ref, o_ref):
    dv = 2 * HEAD_DIM
    for h in range(RET_HEADS):
        sl = slice(dv * h, dv * (h + 1))
        o = of_ref[:, sl] + ob_ref[:, sl]
        mu = jnp.mean(o, axis=-1, keepdims=True)
        var = jnp.mean(jnp.square(o - mu), axis=-1, keepdims=True)
        y = (o - mu) * lax.rsqrt(var + NORM_EPS) * gn_ref[:, sl]
        g = g_ref[:, sl]
        o_ref[:, sl] = (g * jax.nn.sigmoid(g) * y * beta_ref[:, sl]).astype(o_ref.dtype)


def ret_gate(o2, gate, gn_g, beta):
    _, r, w = o2.shape
    vec = pl.BlockSpec((1, w), lambda i: (0, 0))
    row = pl.BlockSpec((ROW_TILE, w), lambda i: (i, 0))
    return _pcall(
        _ret_gate_kernel,
        grid=(r // ROW_TILE,),
        in_specs=[pl.BlockSpec((None, ROW_TILE, w), lambda i: (0, i, 0)),
                  pl.BlockSpec((None, ROW_TILE, w), lambda i: (1, i, 0)), row, vec, vec],
        out_specs=row,
        out_shape=jax.ShapeDtypeStruct((r, w), BF16),
        compiler_params=_cparams(("parallel",)),
    )(o2, o2, gate, gn_g, beta)


def _short_conv_kernel(cur_ref, prev_ref, next_ref, w_ref, b_ref, z_ref, x1_ref, x2_ref, *, n_seq_tiles):
    j = pl.program_id(1)

    @pl.when(j < n_seq_tiles)
    def _():
        u = cur_ref[...]
        rows = lax.broadcasted_iota(jnp.int32, u.shape, 0)
        prev_row = jnp.where(j == 0, 0.0, prev_ref[7:8, :])
        next_row = jnp.where(j == n_seq_tiles - 1, 0.0, next_ref[0:1, :])
        up = jnp.where(rows == 0, prev_row, pltpu.roll(u, 1, 0))
        un = jnp.where(rows == ROW_TILE - 1, next_row, pltpu.roll(u, ROW_TILE - 1, 0))
        y = up * w_ref[0:1, :] + u * w_ref[1:2, :] + un * w_ref[2:3, :] + b_ref[...]
        z_ref[...] = y[:, :HY_WIDTH]
        x1_ref[...] = y[:, HY_WIDTH:2 * HY_WIDTH]
        x2_ref[...] = y[:, 2 * HY_WIDTH:]

    @pl.when(j >= n_seq_tiles)
    def _():
        z_ref[...] = jnp.zeros_like(z_ref)
        x1_ref[...] = jnp.zeros_like(x1_ref)
        x2_ref[...] = jnp.zeros_like(x2_ref)


def short_conv(p_hy, w, b, *, bsz, n_t, first_tile, n_seq_tiles, n_pad_tiles):
    w3 = 3 * HY_WIDTH
    sub = ROW_TILE // 8
    last_blk8 = p_hy.shape[0] // 8 - 1

    def cur(bb, j):
        return (bb * n_t + first_tile + jnp.minimum(j, n_seq_tiles - 1), 0)

    def prev(bb, j):
        return (jnp.maximum(cur(bb, j)[0] * sub - 1, 0), 0)

    def nxt(bb, j):
        return (jnp.minimum((cur(bb, j)[0] + 1) * sub, last_blk8), 0)

    kern = functools.partial(_short_conv_kernel, n_seq_tiles=n_seq_tiles)
    rows = bsz * n_pad_tiles * ROW_TILE
    return _pcall(
        kern,
        grid=(bsz, n_pad_tiles),
        in_specs=[pl.BlockSpec((ROW_TILE, w3), cur), pl.BlockSpec((8, w3), prev), pl.BlockSpec((8, w3), nxt),
                  pl.BlockSpec((3, w3), lambda bb, j: (0, 0)), pl.BlockSpec((1, w3), lambda bb, j: (0, 0))],
        out_specs=[pl.BlockSpec((ROW_TILE, HY_WIDTH), lambda bb, j: (bb * n_pad_tiles + j, 0))] * 3,
        out_shape=[jax.ShapeDtypeStruct((rows, HY_WIDTH), F32)] * 3,
        compiler_params=_cparams(("parallel", "arbitrary")),
    )(p_hy, p_hy, p_hy, w, b.reshape(1, w3))


def _filter_kernel(feat_ref, w1_ref, b1_ref, w2_ref, b2_ref, w3_ref, win_ref, h_ref, asum_ref):
    i = pl.program_id(0)
    hp = lax.Precision.HIGHEST
    h = jnp.sin(jnp.dot(feat_ref[...], w1_ref[...], precision=hp, preferred_element_type=F32) + b1_ref[...])
    h = jnp.sin(jnp.dot(h, w2_ref[...], precision=hp, preferred_element_type=F32) + b2_ref[...])
    h = jnp.dot(h, w3_ref[...], precision=hp, preferred_element_type=F32)
    win = win_ref[...]
    h = h * jnp.concatenate([win] * (2 * HY_ORDER), axis=-1)
    h_ref[...] = h
    rows = lax.broadcasted_iota(jnp.int32, h.shape, 0) + i * ROW_TILE
    cols = lax.broadcasted_iota(jnp.int32, h.shape, 1)
    is_bwd = (cols // HY_WIDTH) % 2 == 1
    part = jnp.sum(jnp.where(is_bwd & (rows == 0), 0.0, jnp.abs(h)), axis=0, keepdims=True)

    @pl.when(i == 0)
    def _():
        asum_ref[...] = part

    @pl.when(i > 0)
    def _():
        asum_ref[...] += part


def hyena_filter_taps(length, w1, b1, w2, b2, w3):
    t = jnp.arange(length, dtype=F32)
    t_norm = t / length
    f = jnp.linspace(1e-4, HY_BANDS - 1, HY_BANDS, dtype=F32)
    wt = 2.0 * math.pi * t_norm
    feats = jnp.concatenate([t_norm[:, None], jnp.cos(wt[:, None] * f), -jnp.sin(wt[:, None] * f)], axis=-1)
    feats = jnp.pad(feats, ((0, 0), (0, 128 - HY_EMB_DIM)))
    w1p = jnp.pad(w1, ((0, 128 - HY_EMB_DIM), (0, 0)))
    deltas = jnp.abs(jnp.linspace(HY_MIN_DECAY, HY_MAX_DECAY, HY_WIDTH, dtype=F32))
    window = jnp.exp(-t_norm[:, None] * deltas[None])
    wout = HY_ORDER * 2 * HY_WIDTH
    full = lambda shp: pl.BlockSpec(shp, lambda i: (0, 0))
    return _pcall(
        _filter_kernel,
        grid=(length // ROW_TILE,),
        in_specs=[pl.BlockSpec((ROW_TILE, 128), lambda i: (i, 0)), full((128, HY_FILT_HIDDEN)),
                  full((1, HY_FILT_HIDDEN)), full((HY_FILT_HIDDEN, HY_FILT_HIDDEN)), full((1, HY_FILT_HIDDEN)),
                  full((HY_FILT_HIDDEN, wout)), pl.BlockSpec((ROW_TILE, HY_WIDTH), lambda i: (i, 0))],
        out_specs=[pl.BlockSpec((ROW_TILE, wout), lambda i: (i, 0)), full((1, wout))],
        out_shape=[jax.ShapeDtypeStruct((length, wout), F32), jax.ShapeDtypeStruct((1, wout), F32)],
        compiler_params=_cparams(("arbitrary",)),
    )(feats, w1p, b1.reshape(1, -1), w2, b2.reshape(1, -1), w3, window)


class _Dft:
    def __init__(self, n1):
        n2 = FFT_N2
        assert n1 % 16 == 0
        self.n1, self.n = n1, n1 * n2
        self.half = n1 // 2
        self.nf = n1 // 2 + 1
        self.nfp = -(-self.nf // 16) * 16
        f1 = np.arange(self.nf)[:, None]
        ang = 2.0 * np.pi * f1 * np.arange(n1)[None, :] / n1
        s1 = np.zeros((2 * self.nfp, n1))
        s1[:self.nf] = np.cos(ang)
        s1[self.nfp:self.nfp + self.nf] = -np.sin(ang)
        self.s1_full = s1
        tw = 2.0 * np.pi * f1 * np.arange(n2)[None, :] / self.n
        self.tw_cos = np.cos(tw)[:, :, None]
        self.tw_sin = np.sin(tw)[:, :, None]
        a2 = 2.0 * np.pi * np.arange(n2)[:, None] * np.arange(n2)[None, :] / n2
        wc, ws = np.cos(a2), np.sin(a2)
        self.m_fwd = np.block([[wc, ws], [-ws, wc]])
        self.m_inv = np.block([[wc, -ws], [ws, wc]])
        wgt = np.full(self.nf, 2.0)
        wgt[0] = 1.0
        wgt[-1] = 1.0
        ango = 2.0 * np.pi * np.arange(self.half)[:, None] * np.arange(self.nf)[None, :] / n1
        self.s4_re = np.zeros((self.half, self.nfp))
        self.s4_im = np.zeros((self.half, self.nfp))
        self.s4_re[:, :self.nf] = np.cos(ango) * wgt / self.n
        self.s4_im[:, :self.nf] = -np.sin(ango) * wgt / self.n

    @staticmethod
    def const(a, dtype=BF16):
        return jnp.asarray(a, F32).astype(dtype)


def _fft_s1_kernel(m_ref, x_ref, o_ref):
    k_rows, n_sub, w = x_ref.shape
    x = x_ref[...].reshape(k_rows * n_sub, w)
    y = jnp.dot(m_ref[...], x.astype(BF16), preferred_element_type=F32)
    o_ref[...] = y.reshape(o_ref.shape)


def fft_stage1(mat, x3, *, n_batch, k_rows):
    w = x3.shape[-1]
    m = mat.shape[0]
    big = _Dft.const(np.kron(mat, np.eye(FFT_SUB)))
    return _pcall(
        _fft_s1_kernel,
        grid=(n_batch, FFT_N2 // FFT_SUB),
        in_specs=[pl.BlockSpec((m * FFT_SUB, k_rows * FFT_SUB), lambda b, j: (0, 0)),
                  pl.BlockSpec((k_rows, FFT_SUB, w), lambda b, j: (b, j, 0))],
        out_specs=pl.BlockSpec((None, m, FFT_SUB, w), lambda b, j: (b, 0, j, 0)),
        out_shape=jax.ShapeDtypeStruct((n_batch, m, FFT_N2, w), F32),
        compiler_params=_cparams(("parallel", "parallel")),
    )(big, x3)


def _twiddle(ar, ai, c, s):
    return ar * c + ai * s, ai * c - ar * s


def _fft_filter_mid_kernel(ar_ref, ai_ref, c_ref, s_ref, mf_ref, sc_ref, hb0_ref, kr_ref, ki_ref):
    br, bi = _twiddle(ar_ref[...], ai_ref[...], c_ref[...], s_ref[...])
    x = jnp.dot(mf_ref[...], jnp.concatenate([br, bi], axis=0).astype(BF16), preferred_element_type=F32)
    xr, xi = x[:FFT_N2], x[FFT_N2:]
    w = HY_WIDTH
    for o in range(HY_ORDER):
        fw = slice(2 * o * w, (2 * o + 1) * w)
        bw = slice((2 * o + 1) * w, (2 * o + 2) * w)
        oc = slice(o * w, (o + 1) * w)
        kr_ref[:, oc] = (xr[:, fw] + xr[:, bw] - hb0_ref[:, oc]) * sc_ref[:, oc]
        ki_ref[:, oc] = (xi[:, fw] - xi[:, bw]) * sc_ref[:, oc]


def fft_filter_mid(a, dft, kscale, hb0):
    cols = a.shape[-1]
    n2 = FFT_N2
    tw = pl.BlockSpec((None, n2, 1), lambda f: (f, 0, 0))
    blk = pl.BlockSpec((None, n2, cols // 2), lambda f: (f, 0, 0))
    vec = pl.BlockSpec((1, cols // 2), lambda f: (0, 0))
    return _pcall(
        _fft_filter_mid_kernel,
        grid=(dft.nf,),
        in_specs=[pl.BlockSpec((None, n2, cols), lambda f: (f, 0, 0)),
                  pl.BlockSpec((None, n2, cols), lambda f: (dft.nfp + f, 0, 0)), tw, tw,
                  pl.BlockSpec((2 * n2, 2 * n2), lambda f: (0, 0)), vec, vec],
        out_specs=[blk, blk],
        out_shape=[jax.ShapeDtypeStruct((dft.nf, n2, cols // 2), F32)] * 2,
        compiler_params=_cparams(("parallel",)),
    )(a, a, _Dft.const(dft.tw_cos, F32), _Dft.const(dft.tw_sin, F32), _Dft.const(dft.m_fwd), kscale, hb0)


def _fft_mid_kernel(ar_ref, ai_ref, c_ref, s_ref, mf_ref, mi_ref, kr_ref, ki_ref, er_ref, ei_ref, *, nf):
    for u in range(FFT_MID_F):
        f1 = pl.program_id(1) * FFT_MID_F + u

        @pl.when(f1 < nf)
        def _():
            c, s = c_ref[u], s_ref[u]
            br, bi = _twiddle(ar_ref[u], ai_ref[u], c, s)
            x = jnp.dot(mf_ref[...], jnp.concatenate([br, bi], axis=0).astype(BF16), preferred_element_type=F32)
            xr, xi = x[:FFT_N2], x[FFT_N2:]
            kr, ki = kr_ref[u], ki_ref[u]
            yr = xr * kr - xi * ki
            yi = xr * ki + xi * kr
            dd = jnp.dot(mi_ref[...], jnp.concatenate([yr, yi], axis=0).astype(BF16), preferred_element_type=F32)
            dr, di = dd[:FFT_N2], dd[FFT_N2:]
            er_ref[u] = dr * c - di * s
            ei_ref[u] = di * c + dr * s

        @pl.when(f1 >= nf)
        def _():
            er_ref[u] = jnp.zeros((FFT_N2, er_ref.shape[-1]), F32)
            ei_ref[u] = jnp.zeros((FFT_N2, ei_ref.shape[-1]), F32)


def fft_mid(a, dft, kr, ki, order):
    nb = a.shape[0]
    n2, w, g = FFT_N2, HY_WIDTH, FFT_MID_F
    nf, nfp = dft.nf, dft.nfp
    assert nfp % g == 0
    last = (nf - 1) // g
    fc = lambda f: jnp.minimum(f, last)
    tw = pl.BlockSpec((g, n2, 1), lambda b, f: (fc(f), 0, 0))
    mat = pl.BlockSpec((2 * n2, 2 * n2), lambda b, f: (0, 0))
    kf = pl.BlockSpec((g, n2, w), lambda b, f: (fc(f), 0, order))
    out = pl.BlockSpec((None, g, n2, w), lambda b, f: (b, f, 0, 0))
    return _pcall(
        functools.partial(_fft_mid_kernel, nf=nf),
        grid=(nb, nfp // g),
        in_specs=[pl.BlockSpec((None, g, n2, w), lambda b, f: (b, fc(f), 0, 0)),
                  pl.BlockSpec((None, g, n2, w), lambda b, f: (b, nfp // g + fc(f), 0, 0)),
                  tw, tw, mat, mat, kf, kf],
        out_specs=[out, out],
        out_shape=[jax.ShapeDtypeStruct((nb, nfp, n2, w), F32)] * 2,
        compiler_params=_cparams(("parallel", "arbitrary")),
    )(a, a, _Dft.const(dft.tw_cos, F32), _Dft.const(dft.tw_sin, F32), _Dft.const(dft.m_fwd),
      _Dft.const(dft.m_inv), kr, ki)


def _fft_s4_kernel(mr_ref, mi_ref, er_ref, ei_ref, gate_ref, z_ref, bias_ref, scale_ref, o_ref):
    nfp, n_sub, w = er_ref.shape
    er = er_ref[...].reshape(nfp * n_sub, w).astype(BF16)
    ei = ei_ref[...].reshape(nfp * n_sub, w).astype(BF16)
    y = (jnp.dot(mr_ref[...], er, preferred_element_type=F32)
         + jnp.dot(mi_ref[...], ei, preferred_element_type=F32)).reshape(o_ref.shape)
    o_ref[...] = gate_ref[...] * (y + bias_ref[...] * z_ref[...]) * scale_ref[...]


def fft_stage4(dft, er, ei, gate, z, bias_row, scale_row):
    nb = er.shape[0]
    m, nfp = dft.half, dft.nfp
    w = z.shape[-1]
    eye = np.eye(FFT_SUB)
    row = pl.BlockSpec((m, FFT_SUB, w), lambda b, j: (b, j, 0))
    vec = pl.BlockSpec((1, 1, w), lambda b, j: (0, 0, 0))
    mat = pl.BlockSpec((m * FFT_SUB, nfp * FFT_SUB), lambda b, j: (0, 0))
    spec = pl.BlockSpec((None, nfp, FFT_SUB, w), lambda b, j: (b, 0, j, 0))
    return _pcall(
        _fft_s4_kernel,
        grid=(nb, FFT_N2 // FFT_SUB),
        in_specs=[mat, mat, spec, spec, row, row, vec, vec],
        out_specs=row,
        out_shape=jax.ShapeDtypeStruct((nb * m, FFT_N2, w), F32),
        compiler_params=_cparams(("parallel", "parallel")),
    )(_Dft.const(np.kron(dft.s4_re, eye)), _Dft.const(np.kron(dft.s4_im, eye)), er, ei, gate, z,
      bias_row.reshape(1, 1, w), scale_row.reshape(1, 1, w))


def hyena_long_conv(z, x1, x2, filt, hbias, beta_hy, *, bsz, length, n1):
    dft = _Dft(n1)
    n2, w = FFT_N2, HY_WIDTH
    half = dft.half
    assert half >= 8 and length % n2 == 0
    taps, asum = filt
    asum = asum.reshape(HY_ORDER, 2, w)
    kscale = (1.0 / (asum[:, 0] + asum[:, 1] + NORM_EPS)).reshape(1, HY_ORDER * w)
    hb0 = taps[0].reshape(HY_ORDER, 2, w)[:, 1].reshape(1, HY_ORDER * w)
    k_taps = length // n2
    a_f = fft_stage1(dft.s1_full[:, :k_taps], taps.reshape(k_taps, n2, 2 * HY_ORDER * w), n_batch=1, k_rows=k_taps)
    kr, ki = fft_filter_mid(a_f[0], dft, kscale, hb0)
    s1 = dft.s1_full[:, :half]
    ones = jnp.ones((1, w), F32)
    shape3 = (bsz * half, n2, w)
    zc = z.reshape(shape3)
    for o, gate in enumerate((x1, x2)):
        a = fft_stage1(s1, zc, n_batch=bsz, k_rows=half)
        er, ei = fft_mid(a, dft, kr, ki, o)
        last = o == HY_ORDER - 1
        zc = fft_stage4(dft, er, ei, gate.reshape(shape3), zc, hbias[o].reshape(1, w),
                        beta_hy.reshape(1, w) if last else ones)
    return zc.reshape(bsz * half * n2, w)


def _rope_tables(n_ctx, n_lat):
    rows = n_lat // GRID_W
    row = jnp.repeat(jnp.arange(rows, dtype=F32), GRID_W)
    col = jnp.tile(jnp.arange(GRID_W, dtype=F32), rows)
    n_freq = HEAD_DIM // 4
    freqs = ROPE_THETA ** (-jnp.arange(n_freq, dtype=F32) / n_freq)
    ar = row[:, None] * freqs
    ac = col[:, None] * freqs
    cos = jnp.concatenate([jnp.cos(ar), jnp.cos(ar), jnp.cos(ac), jnp.cos(ac)], axis=-1)
    sin = jnp.concatenate([jnp.sin(ar), jnp.sin(ar), jnp.sin(ac), jnp.sin(ac)], axis=-1)
    cos = jnp.concatenate([jnp.ones((n_ctx, HEAD_DIM), F32), cos], axis=0)
    sin = jnp.concatenate([jnp.zeros((n_ctx, HEAD_DIM), F32), sin], axis=0)
    even = (np.arange(HEAD_DIM) // n_freq) % 2 == 0
    sin_a = jnp.where(even, -sin, 0.0)
    sin_b = jnp.where(even, 0.0, sin)
    tile2 = lambda a: jnp.concatenate([a, a], axis=-1)
    return tile2(cos), tile2(sin_a), tile2(sin_b)


def _attn_key_chunk(t):
    for tk in (1408, 1280, 1024, 768, 512, 256):
        if t % tk == 0:
            return tk
    raise ValueError(t)


def kernel(x, c, ctx, c_ctx, ada_w, ada_b, norm_g, w_in, w_out, mix_beta, gqa_qk_g, diff_lambda, diff_subln_g,
           ret_log_decay, ret_gn_g, hy_short_w, hy_short_b, hy_filt_w1, hy_filt_b1, hy_filt_w2, hy_filt_b2,
           hy_filt_w3, hy_bias, ffn_w_gate, ffn_w_up, ffn_w_down, moe_router, moe_w_gate, moe_w_up, moe_w_down):
    bsz, n_lat, d = x.shape
    n_ctx = ctx.shape[1]
    t = n_ctx + n_lat
    r = bsz * t
    n_t, n_c = t // ROW_TILE, n_ctx // ROW_TILE
    n_s = n_lat // ROW_TILE
    assert n_ctx % ROW_TILE == 0 and n_lat % ROW_TILE == 0 and r % 512 == 0
    tk = _attn_key_chunk(t)
    assert tk >= n_ctx

    rope_c, rope_sa, rope_sb = _rope_tables(n_ctx, n_lat)
    xs = jnp.concatenate([ctx, x], axis=1).reshape(r, d)

    cvec = jnp.stack([jnp.broadcast_to(c_ctx, c.shape), c], axis=1).reshape(2 * bsz, d)
    cvec = jax.nn.silu(cvec)
    cvec = jnp.pad(cvec, ((0, 16 - 2 * bsz), (0, 0))).astype(BF16)

    n1_lat = 2 * n_lat // FFT_N2
    ctx_pad_tiles = max(n_c, 1024 // ROW_TILE)
    n1_ctx = 2 * ctx_pad_tiles * ROW_TILE // FFT_N2

    mods = []
    for l in range(DEPTH):
        mod = mm(cvec, ada_w, tn=1536, out_dtype=F32, lead=(l,))[:2 * bsz] + ada_b[l]
        mods.append(mod.reshape(2 * bsz, 6, 1, d))

    (h,) = norm_mod(xs, norm_g[0, 0], mods[0][:, 0], mods[0][:, 1], n_t, n_c, (BF16,))
    for l in range(DEPTH):
        lambda_init = 0.8 - 0.6 * math.exp(-0.3 * l)
        sh_m, sc_m, gt_m, sh_f, sc_f, gt_f = (mods[l][:, k] for k in range(6))
        beta = mix_beta[l].reshape(1, 4 * GROUP_WIDTH)
        beta_a, beta_b, beta_r, beta_d = (beta[:, GROUP_WIDTH * k:GROUP_WIDTH * (k + 1)] for k in range(4))
        moe_layer = l % 2 == 1
        i = l // 2
        nxt_mixer = (norm_g[l + 1, 0], mods[l + 1][:, 0], mods[l + 1][:, 1], BF16) if l + 1 < DEPTH else None

        p_attn = mm(h, w_in, tn=768, out_dtype=F32, lead=(l,), n_cols=COL_ATTN)
        p_hy = mm(h, w_in, tn=768, out_dtype=F32, lead=(l,), col_off=COL_HY // 768,
                  n_cols=IN_WIDTH - COL_HY)
        aq, ak, av, dq, dk, dv, rq, rk, rv, rg = prep_heads(p_attn, rope_c, rope_sa, rope_sb, gqa_qk_g[l], bsz, n_t)

        a_out = gqa_attention(aq, ak, av, beta_a, n_t=n_t, n_c=n_c, tk=tk)

        lamf = diff_lambda[l]
        lam_full = (jnp.exp(jnp.sum(lamf[0] * lamf[1])) - jnp.exp(jnp.sum(lamf[2] * lamf[3])) + lambda_init)
        b_out = diff_attention(dq, dk, dv, lam_full.reshape(1, 1), diff_subln_g[l].reshape(1, -1), beta_b,
                               1.0 - lambda_init, n_t=n_t, n_c=n_c, tk=tk)

        o2 = retention(rq, rk, rv, ret_log_decay[l], n_t=n_t, n_c=n_c)
        r_out = ret_gate(o2, rg, ret_gn_g[l].reshape(1, -1), beta_r)

        filt_args = (hy_filt_w1[l], hy_filt_b1[l], hy_filt_w2[l], hy_filt_b2[l], hy_filt_w3[l])
        z_l, x1_l, x2_l = short_conv(p_hy, hy_short_w[l], hy_short_b[l], bsz=bsz, n_t=n_t, first_tile=n_c,
                                     n_seq_tiles=n_s, n_pad_tiles=n_s)
        d_l = hyena_long_conv(z_l, x1_l, x2_l, hyena_filter_taps(n_lat, *filt_args), hy_bias[l], beta_d[0],
                              bsz=bsz, length=n_lat, n1=n1_lat)
        z_c, x1_c, x2_c = short_conv(p_hy, hy_short_w[l], hy_short_b[l], bsz=bsz, n_t=n_t, first_tile=0,
                                     n_seq_tiles=n_c, n_pad_tiles=ctx_pad_tiles)
        d_c = hyena_long_conv(z_c, x1_c, x2_c, hyena_filter_taps(n_ctx, *filt_args), hy_bias[l], beta_d[0],
                              bsz=bsz, length=n_ctx, n1=n1_ctx)
        d_out = jnp.concatenate([d_c.reshape(bsz, -1, HY_WIDTH)[:, :n_ctx], d_l.reshape(bsz, n_lat, HY_WIDTH)],
                                axis=1).reshape(r, HY_WIDTH).astype(BF16)

        y = mm_parts([a_out, b_out, r_out, d_out], w_out, tn=512, lead=(l,))
        xs, f_in = gate_res(xs, y, norm_g[l, 1], gt_m, n_t, n_c,
                            nxt=(norm_g[l, 2], sh_f, sc_f, F32 if moe_layer else BF16))

        if not moe_layer:
            hid = swiglu_up(f_in, ffn_w_gate, ffn_w_up, tn=512, lead=(i,))
            f_out = mm(hid, ffn_w_down, tn=512, out_dtype=F32, lead=(i,), max_tile_mb=6)
            res = gate_res(xs, f_out, norm_g[l, 3], gt_f, n_t, n_c, nxt=nxt_mixer)
        else:
            top_i, top_w = router(f_in, moe_router[i].T)
            n_pad_rows = (2 * r // MOE_TM + N_EXPERTS) * MOE_TM
            row_token, row_weight, tile_expert, n_used, pos = route_tokens(top_i, top_w, n_pad_rows)
            ff = moe_w_gate.shape[-1]
            chunked = lambda wt: wt.astype(BF16).reshape(N_EXPERTS, d, ff // MOE_TF, MOE_TF).transpose(0, 2, 1, 3)
            y_sorted = moe_experts(f_in, chunked(moe_w_gate[i]), chunked(moe_w_up[i]), moe_w_down[i].astype(BF16),
                                   row_token, row_weight, tile_expert, n_used)
            res = moe_combine_gate_res(xs, y_sorted, pos, norm_g[l, 3], gt_f, n_t, n_c, nxt=nxt_mixer)
        xs, h = res if nxt_mixer is not None else (res, None)

    return xs.reshape(bsz, t, d)[:, n_ctx:]
```

```python
import functools
import math

import numpy as np
import jax
import jax.numpy as jnp
from jax import lax
from jax.experimental import pallas as pl
from jax.experimental.pallas import tpu as pltpu

F32 = jnp.float32
BF16 = jnp.bfloat16

D_MODEL = 2048
DEPTH = 4
GRID_W = 64
HEAD_DIM = 64
ROPE_THETA = 10000.0
NORM_EPS = 1e-6
GROUP_WIDTH = 512
GQA_Q_HEADS = 8
GQA_KV_HEADS = 2
GQA_REP = 4
DIFF_HEADS = 4
RET_HEADS = 4
HY_WIDTH = 512
HY_ORDER = 2
HY_EMB_DIM = 33
HY_BANDS = 16
HY_FILT_HIDDEN = 64
HY_MIN_DECAY = math.log(1e-2) / 1.5
HY_MAX_DECAY = math.log(1e-2) / 0.3
N_EXPERTS = 8

COL_ATTN = 3840
COL_DIFF = 768
COL_RET = 2304
COL_RET_GATE = 3328
COL_HY = 3840
IN_WIDTH = 5376

ROW_TILE = 256
MOE_TM = 1024
MOE_TF = 256
FFT_N2 = 128
FFT_MID_F = 4
FFT_SUB = 8
VMEM_LIMIT_MB = 48


def _cparams(sem, vmem_mb=VMEM_LIMIT_MB):
    return pltpu.CompilerParams(dimension_semantics=sem, vmem_limit_bytes=vmem_mb * 1024 * 1024)


def _pcall(kern, **kw):
    fn = kern.func if isinstance(kern, functools.partial) else kern
    return pl.pallas_call(kern, name=fn.__name__.strip("_"), **kw)


def _pick_tm(m, k, max_tile_mb):
    for tm in (1536, 1280, 1024, 768, 512, 256, 128, 16):
        if m % tm == 0 and tm * k * 2 <= max_tile_mb * 2 ** 20:
            return tm
    raise ValueError((m, k))


def _mm_kernel(a_ref, b_ref, o_ref, bb_ref):
    @pl.when(pl.program_id(1) == 0)
    def _():
        bb_ref[...] = b_ref[...].astype(BF16)

    o_ref[...] = jnp.dot(a_ref[...], bb_ref[...], preferred_element_type=F32).astype(o_ref.dtype)


def _b_spec(b, lead, k, tn, col_off):
    nlead = len(lead)
    return pl.BlockSpec((None,) * nlead + (k, tn), lambda j, i: tuple(lead) + (0, j + col_off))


def mm(a, b, *, tn, out_dtype, lead=(), col_off=0, n_cols=None, max_tile_mb=9):
    m, k = a.shape
    n = b.shape[-1] if n_cols is None else n_cols
    tm = _pick_tm(m, k, max_tile_mb)
    assert n % tn == 0 and a.dtype == BF16, (n, tn, a.dtype)
    return _pcall(
        _mm_kernel,
        grid=(n // tn, m // tm),
        in_specs=[pl.BlockSpec((tm, k), lambda j, i: (i, 0)), _b_spec(b, lead, k, tn, col_off)],
        out_specs=pl.BlockSpec((tm, tn), lambda j, i: (i, j)),
        out_shape=jax.ShapeDtypeStruct((m, n), out_dtype),
        scratch_shapes=[pltpu.VMEM((k, tn), BF16)],
        compiler_params=_cparams(("parallel", "arbitrary")),
    )(a, b)


def _mm_parts_kernel(*refs, n_parts):
    a_refs, (b_ref, o_ref, bb_ref) = refs[:n_parts], refs[n_parts:]

    @pl.when(pl.program_id(1) == 0)
    def _():
        bb_ref[...] = b_ref[...].astype(BF16)

    kp = a_refs[0].shape[1]
    acc = jnp.dot(a_refs[0][...], bb_ref[0:kp, :], preferred_element_type=F32)
    for g in range(1, n_parts):
        acc += jnp.dot(a_refs[g][...], bb_ref[g * kp:(g + 1) * kp, :], preferred_element_type=F32)
    o_ref[...] = acc


def mm_parts(parts, b, *, tn, lead):
    m, kp = parts[0].shape
    k = kp * len(parts)
    n = b.shape[-1]
    tm = _pick_tm(m, k, 9)
    row = pl.BlockSpec((tm, kp), lambda j, i: (i, 0))
    return _pcall(
        functools.partial(_mm_parts_kernel, n_parts=len(parts)),
        grid=(n // tn, m // tm),
        in_specs=[row] * len(parts) + [_b_spec(b, lead, k, tn, 0)],
        out_specs=pl.BlockSpec((tm, tn), lambda j, i: (i, j)),
        out_shape=jax.ShapeDtypeStruct((m, n), F32),
        scratch_shapes=[pltpu.VMEM((k, tn), BF16)],
        compiler_params=_cparams(("parallel", "arbitrary")),
    )(*parts, b)


def _swiglu_up_kernel(a_ref, wg_ref, wu_ref, o_ref, bg_ref, bu_ref):
    @pl.when(pl.program_id(1) == 0)
    def _():
        bg_ref[...] = wg_ref[...].astype(BF16)
        bu_ref[...] = wu_ref[...].astype(BF16)

    a = a_ref[...]
    g = jnp.dot(a, bg_ref[...], preferred_element_type=F32)
    u = jnp.dot(a, bu_ref[...], preferred_element_type=F32)
    o_ref[...] = (g * jax.nn.sigmoid(g) * u).astype(o_ref.dtype)


def swiglu_up(a, wg, wu, *, tn, lead):
    m, k = a.shape
    n = wg.shape[-1]
    tm = _pick_tm(m, k, 4)
    assert n % tn == 0
    return _pcall(
        _swiglu_up_kernel,
        grid=(n // tn, m // tm),
        in_specs=[pl.BlockSpec((tm, k), lambda j, i: (i, 0)),
                  _b_spec(wg, lead, k, tn, 0), _b_spec(wu, lead, k, tn, 0)],
        out_specs=pl.BlockSpec((tm, tn), lambda j, i: (i, j)),
        out_shape=jax.ShapeDtypeStruct((m, n), BF16),
        scratch_shapes=[pltpu.VMEM((k, tn), BF16)] * 2,
        compiler_params=_cparams(("parallel", "arbitrary")),
    )(a, wg, wu)


def _seg_map(n_t, n_c):
    return lambda i: (2 * (i // n_t) + jnp.where(i % n_t >= n_c, 1, 0), 0, 0)


def _rms(x):
    return x * lax.rsqrt(jnp.mean(x * x, axis=-1, keepdims=True) + NORM_EPS)


def _norm_mod_kernel(x_ref, g_ref, sh_ref, sc_ref, *o_refs):
    y = _rms(x_ref[...]) * g_ref[...]
    y = y * (1.0 + sc_ref[...]) + sh_ref[...]
    for o_ref in o_refs:
        o_ref[...] = y.astype(o_ref.dtype)


def norm_mod(x, g, shift, scale, n_t, n_c, out_dtypes):
    r, d = x.shape
    seg = _seg_map(n_t, n_c)
    row = pl.BlockSpec((ROW_TILE, d), lambda i: (i, 0))
    outs = _pcall(
        _norm_mod_kernel,
        grid=(r // ROW_TILE,),
        in_specs=[row, pl.BlockSpec((1, d), lambda i: (0, 0)),
                  pl.BlockSpec((None, 1, d), seg), pl.BlockSpec((None, 1, d), seg)],
        out_specs=[row] * len(out_dtypes),
        out_shape=[jax.ShapeDtypeStruct((r, d), dt) for dt in out_dtypes],
        compiler_params=_cparams(("parallel",)),
    )(x, g.reshape(1, d), shift, scale)
    return outs


def _gate_res_kernel(x_ref, y_ref, g_ref, gt_ref, *rest):
    x = x_ref[...] + gt_ref[...] * (_rms(y_ref[...]) * g_ref[...])
    if len(rest) == 1:
        rest[0][...] = x
    else:
        g2_ref, sh_ref, sc_ref, o_ref, h_ref = rest
        o_ref[...] = x
        h_ref[...] = (_rms(x) * g2_ref[...] * (1.0 + sc_ref[...]) + sh_ref[...]).astype(h_ref.dtype)


def gate_res(x, y, g, gate, n_t, n_c, nxt=None):
    r, d = x.shape
    seg = _seg_map(n_t, n_c)
    row = pl.BlockSpec((ROW_TILE, d), lambda i: (i, 0))
    vec = pl.BlockSpec((1, d), lambda i: (0, 0))
    mod = pl.BlockSpec((None, 1, d), seg)
    in_specs, args = [row, row, vec, mod], [x, y, g.reshape(1, d), gate]
    out_specs, out_shape = row, jax.ShapeDtypeStruct((r, d), F32)
    if nxt is not None:
        g2, shift, scale, dtype = nxt
        in_specs, args = in_specs + [vec, mod, mod], args + [g2.reshape(1, d), shift, scale]
        out_specs, out_shape = [row, row], [out_shape, jax.ShapeDtypeStruct((r, d), dtype)]
    return _pcall(
        _gate_res_kernel,
        grid=(r // ROW_TILE,),
        in_specs=in_specs,
        out_specs=out_specs,
        out_shape=out_shape,
        compiler_params=_cparams(("parallel",)),
    )(*args)


def _router_kernel(f_ref, rt_ref, idx_ref, w_ref):
    lt = lax.dot_general(rt_ref[...], f_ref[...], (((1,), (1,)), ((), ())),
                         precision=lax.Precision.HIGHEST, preferred_element_type=F32)
    e_idx = lax.broadcasted_iota(jnp.int32, lt.shape, 0)
    m1 = jnp.max(lt, axis=0, keepdims=True)
    i1 = jnp.min(jnp.where(lt == m1, e_idx, N_EXPERTS), axis=0, keepdims=True)
    rest = jnp.where(e_idx == i1, -jnp.inf, lt)
    m2 = jnp.max(rest, axis=0, keepdims=True)
    i2 = jnp.min(jnp.where(rest == m2, e_idx, N_EXPERTS), axis=0, keepdims=True)
    e2 = jnp.exp(m2 - m1)
    idx_ref[...] = jnp.concatenate([i1, i2], axis=0)
    w_ref[...] = jnp.concatenate([1.0 / (1.0 + e2), e2 / (1.0 + e2)], axis=0)


def router(f, router_t):
    r, d = f.shape
    return _pcall(
        _router_kernel,
        grid=(r // ROW_TILE,),
        in_specs=[pl.BlockSpec((ROW_TILE, d), lambda i: (i, 0)),
                  pl.BlockSpec((N_EXPERTS, d), lambda i: (0, 0))],
        out_specs=[pl.BlockSpec((2, ROW_TILE), lambda i: (0, i))] * 2,
        out_shape=[jax.ShapeDtypeStruct((2, r), jnp.int32), jax.ShapeDtypeStruct((2, r), F32)],
        compiler_params=_cparams(("parallel",)),
    )(f, router_t)


def _row_copy(src_hbm, dst, src_row, dst_row, sem):
    return pltpu.make_async_copy(src_hbm.at[pl.ds(src_row, 1), :], dst.at[pl.ds(dst_row, 1), :], sem)


def _start_rows(src_hbm, dst, idx_ref, base, n_rows, sem):
    def start(r, carry):
        _row_copy(src_hbm, dst, idx_ref[base + r], r, sem).start()
        return carry

    lax.fori_loop(0, n_rows, start, 0, unroll=8)


def _wait_rows(src_hbm, dst, idx_ref, base, n_rows, sem):
    def wait(r, carry):
        _row_copy(src_hbm, dst, idx_ref[base + r], r, sem).wait()
        return carry

    lax.fori_loop(0, n_rows, wait, 0, unroll=8)


def _gather_rows(src_hbm, dst, idx_ref, base, n_rows, sem):
    _start_rows(src_hbm, dst, idx_ref, base, n_rows, sem)
    _wait_rows(src_hbm, dst, idx_ref, base, n_rows, sem)


def route_tokens(idx, w, n_pad_rows):
    r = idx.shape[1]
    n_assign = 2 * r
    n_tiles = n_pad_rows // MOE_TM
    e_flat = idx.reshape(n_assign)
    w_flat = w.reshape(n_assign)
    counts = jnp.sum((e_flat[:, None] == jnp.arange(N_EXPERTS)[None, :]).astype(jnp.int32), axis=0)
    padded = (counts + MOE_TM - 1) // MOE_TM * MOE_TM
    group_end = jnp.cumsum(padded)
    group_start = group_end - padded
    sorted_start = jnp.cumsum(counts) - counts
    order = jnp.argsort(e_flat, stable=True).astype(jnp.int32)
    inv = jnp.argsort(order).astype(jnp.int32)
    pos = group_start[e_flat] + (inv - sorted_start[e_flat])
    rows = jnp.arange(n_pad_rows, dtype=jnp.int32)
    row_e = jnp.minimum(jnp.sum((rows[:, None] >= group_end[None, :]).astype(jnp.int32), axis=1), N_EXPERTS - 1)
    rank = rows - group_start[row_e]
    valid = rank < counts[row_e]
    src = order[jnp.clip(sorted_start[row_e] + rank, 0, n_assign - 1)]
    row_token = jnp.where(valid, src % r, 0)
    row_weight = jnp.where(valid, w_flat[src], 0.0)
    n_used = group_end[-1] // MOE_TM
    tile_start = jnp.arange(n_tiles, dtype=jnp.int32) * MOE_TM
    tile_start = jnp.minimum(tile_start, group_end[-1] - 1)
    tile_expert = jnp.sum((tile_start[:, None] >= group_end[None, :]).astype(jnp.int32), axis=1)
    return (row_token, row_weight.reshape(n_pad_rows, 1), tile_expert.astype(jnp.int32),
            n_used.astype(jnp.int32).reshape(1), pos)


def _moe_expert_kernel(te_ref, nu_ref, tok_ref, f_hbm, wg_ref, wu_ref, wd_ref, rw_ref, o_ref, land_ref, xb_ref, sem):
    i, f = pl.program_id(0), pl.program_id(1)
    n_used = nu_ref[0]
    used = i < n_used

    @pl.when(jnp.logical_and(used, f == 0))
    def _():
        @pl.when(i == 0)
        def _():
            _start_rows(f_hbm, land_ref, tok_ref, 0, MOE_TM, sem)

        _wait_rows(f_hbm, land_ref, tok_ref, i * MOE_TM, MOE_TM, sem)
        xb_ref[...] = land_ref[...].astype(BF16)

        @pl.when(i + 1 < n_used)
        def _():
            _start_rows(f_hbm, land_ref, tok_ref, (i + 1) * MOE_TM, MOE_TM, sem)

    @pl.when(f == 0)
    def _():
        o_ref[...] = jnp.zeros_like(o_ref)

    @pl.when(used)
    def _():
        x = xb_ref[...]
        g = jnp.dot(x, wg_ref[...].astype(BF16), preferred_element_type=F32)
        u = jnp.dot(x, wu_ref[...].astype(BF16), preferred_element_type=F32)
        h = (g * jax.nn.sigmoid(g) * u).astype(BF16)
        o_ref[...] += jnp.dot(h, wd_ref[...].astype(BF16), preferred_element_type=F32)

    @pl.when(jnp.logical_and(used, f == pl.num_programs(1) - 1))
    def _():
        o_ref[...] = o_ref[...] * rw_ref[...]


def moe_experts(f_in, wg, wu, wd, layer, row_token, row_weight, tile_expert, n_used):
    r, d = f_in.shape
    ff = wg.shape[-1]
    n_pad_rows = row_token.shape[0]
    nt, nf = n_pad_rows // MOE_TM, ff // MOE_TF

    def fi(i, f, nu):
        return jnp.where(i < nu[0], f, nf - 1)

    grid_spec = pltpu.PrefetchScalarGridSpec(
        num_scalar_prefetch=3,
        grid=(nt, nf),
        in_specs=[pl.BlockSpec(memory_space=pl.ANY),
                  pl.BlockSpec((None, None, d, MOE_TF), lambda i, f, te, nu, tok: (layer, te[i], 0, fi(i, f, nu))),
                  pl.BlockSpec((None, None, d, MOE_TF), lambda i, f, te, nu, tok: (layer, te[i], 0, fi(i, f, nu))),
                  pl.BlockSpec((None, None, MOE_TF, d), lambda i, f, te, nu, tok: (layer, te[i], fi(i, f, nu), 0)),
                  pl.BlockSpec((MOE_TM, 1), lambda i, f, te, nu, tok: (i, 0))],
        out_specs=pl.BlockSpec((MOE_TM, d), lambda i, f, te, nu, tok: (i, 0)),
        scratch_shapes=[pltpu.VMEM((MOE_TM, d), F32), pltpu.VMEM((MOE_TM, d), BF16), pltpu.SemaphoreType.DMA(())],
    )
    return _pcall(
        _moe_expert_kernel,
        grid_spec=grid_spec,
        out_shape=jax.ShapeDtypeStruct((n_pad_rows, d), F32),
        compiler_params=_cparams(("arbitrary", "arbitrary")),
    )(tile_expert, n_used, row_token, f_in, wg, wu, wd, row_weight)


def _moe_combine_kernel(pos_ref, y_hbm, x_ref, g_ref, gt_ref, *rest):
    buf_ref, sem = rest[-2:]
    i = pl.program_id(0)

    def rows(fn, tile):
        for k in range(2):
            fn(y_hbm, buf_ref.at[tile % 2, k], pos_ref, (2 * tile + k) * ROW_TILE, ROW_TILE, sem)

    @pl.when(i == 0)
    def _():
        rows(_start_rows, i)

    rows(_wait_rows, i)

    @pl.when(i + 1 < pl.num_programs(0))
    def _():
        rows(_start_rows, i + 1)

    y = buf_ref[i % 2, 0] + buf_ref[i % 2, 1]
    x = x_ref[...] + gt_ref[...] * (_rms(y) * g_ref[...])
    if len(rest) == 3:
        rest[0][...] = x
    else:
        g2_ref, sh_ref, sc_ref, o_ref, h_ref = rest[:5]
        o_ref[...] = x
        h_ref[...] = (_rms(x) * g2_ref[...] * (1.0 + sc_ref[...]) + sh_ref[...]).astype(h_ref.dtype)


def moe_combine_gate_res(x, y_sorted, pos, g, gate, n_t, n_c, nxt=None):
    r, d = x.shape
    nrt = r // ROW_TILE
    pos_tiles = pos.reshape(2, nrt, ROW_TILE).transpose(1, 0, 2).reshape(2 * r)
    seg = _seg_map(n_t, n_c)
    row = pl.BlockSpec((ROW_TILE, d), lambda i, p: (i, 0))
    vec = pl.BlockSpec((1, d), lambda i, p: (0, 0))
    mod = pl.BlockSpec((None, 1, d), lambda i, p: seg(i))
    in_specs, args = [pl.BlockSpec(memory_space=pl.ANY), row, vec, mod], [y_sorted, x, g.reshape(1, d), gate]
    out_specs, out_shape = row, jax.ShapeDtypeStruct((r, d), F32)
    if nxt is not None:
        g2, shift, scale, dtype = nxt
        in_specs, args = in_specs + [vec, mod, mod], args + [g2.reshape(1, d), shift, scale]
        out_specs, out_shape = [row, row], [out_shape, jax.ShapeDtypeStruct((r, d), dtype)]
    grid_spec = pltpu.PrefetchScalarGridSpec(
        num_scalar_prefetch=1,
        grid=(nrt,),
        in_specs=in_specs,
        out_specs=out_specs,
        scratch_shapes=[pltpu.VMEM((2, 2, ROW_TILE, d), F32), pltpu.SemaphoreType.DMA(())],
    )
    return _pcall(
        _moe_combine_kernel,
        grid_spec=grid_spec,
        out_shape=out_shape,
        compiler_params=_cparams(("arbitrary",)),
    )(pos_tiles, *args)


def _rope(x, c, sa, sb):
    return x * c + pltpu.roll(x, 112, 1) * sa + pltpu.roll(x, 16, 1) * sb


def _head_rms(x, g, bd):
    sq = x * x
    hi = sq.astype(BF16)
    lo = (sq - hi.astype(F32)).astype(BF16)
    ms = jnp.dot(hi, bd, preferred_element_type=F32) + jnp.dot(lo, bd, preferred_element_type=F32)
    return x * lax.rsqrt(ms + NORM_EPS) * g


def _prep_kernel(p_ref, c_ref, sa_ref, sb_ref, gq_ref, gk_ref, bd_ref,
                 aq_ref, ak_ref, av_ref, dq_ref, dk_ref, dv_ref, rq_ref, rk_ref, rv_ref, rg_ref):
    c, sa, sb = c_ref[...], sa_ref[...], sb_ref[...]
    bd = bd_ref[...]
    scale = HEAD_DIM ** -0.5
    qscale = scale * math.log2(math.e)

    def chunk(j):
        return p_ref[:, 128 * j:128 * (j + 1)]

    def put_heads(ref, first, val):
        ref[first] = val[:, :HEAD_DIM].astype(ref.dtype)
        ref[first + 1] = val[:, HEAD_DIM:].astype(ref.dtype)

    for j in range(4):
        put_heads(aq_ref, 2 * j, _rope(_head_rms(chunk(j), gq_ref[...], bd), c, sa, sb) * qscale)
    put_heads(ak_ref, 0, _rope(_head_rms(chunk(4), gk_ref[...], bd), c, sa, sb))
    one64 = (lax.broadcasted_iota(jnp.int32, (ROW_TILE, HEAD_DIM), 1) == 0).astype(av_ref.dtype)
    one128 = (lax.broadcasted_iota(jnp.int32, (ROW_TILE, 2 * HEAD_DIM), 1) == 0).astype(dv_ref.dtype)
    v_gqa = chunk(5)
    for h in range(GQA_KV_HEADS):
        av_ref[h, :, :HEAD_DIM] = v_gqa[:, HEAD_DIM * h:HEAD_DIM * (h + 1)].astype(av_ref.dtype)
        av_ref[h, :, HEAD_DIM:] = one64
    for j in range(4):
        put_heads(dq_ref, 2 * j, _rope(chunk(6 + j), c, sa, sb) * qscale)
        put_heads(dk_ref, 2 * j, _rope(chunk(10 + j), c, sa, sb))
        dv_ref[j, :, :2 * HEAD_DIM] = chunk(14 + j).astype(dv_ref.dtype)
        dv_ref[j, :, 2 * HEAD_DIM:] = one128
    for j in range(2):
        put_heads(rq_ref, 2 * j, _rope(chunk(18 + j), c, sa, sb))
        put_heads(rk_ref, 2 * j, _rope(chunk(20 + j), c, sa, sb) * scale)
    for j in range(4):
        rv_ref[j] = chunk(22 + j).astype(rv_ref.dtype)
    rg_ref[...] = p_ref[:, COL_RET_GATE:COL_ATTN]


def prep_heads(p_attn, rope_c, rope_sa, rope_sb, qk_g, bsz, n_t):
    t = n_t * ROW_TILE
    bd = jnp.asarray(np.kron(np.eye(2), np.full((HEAD_DIM, HEAD_DIM), 1.0 / HEAD_DIM)), F32).astype(BF16)
    gq = jnp.tile(qk_g[0], 2).reshape(1, 128)
    gk = jnp.tile(qk_g[1], 2).reshape(1, 128)
    tab = pl.BlockSpec((ROW_TILE, 128), lambda i: (i % n_t, 0))
    vec = pl.BlockSpec((1, 128), lambda i: (0, 0))

    def heads(nh, dh):
        return (pl.BlockSpec((None, nh, ROW_TILE, dh), lambda i: (i // n_t, 0, i % n_t, 0)),
                jax.ShapeDtypeStruct((bsz, nh, t, dh), BF16))

    outs = [heads(8, 64), heads(2, 64), heads(2, 128), heads(8, 64), heads(8, 64), heads(4, 256),
            heads(4, 64), heads(4, 64), heads(4, 128)]
    return _pcall(
        _prep_kernel,
        grid=(bsz * n_t,),
        in_specs=[pl.BlockSpec((ROW_TILE, COL_ATTN), lambda i: (i, 0)), tab, tab, tab, vec, vec,
                  pl.BlockSpec((128, 128), lambda i: (0, 0))],
        out_specs=[o[0] for o in outs] + [pl.BlockSpec((ROW_TILE, GROUP_WIDTH), lambda i: (i, 0))],
        out_shape=[o[1] for o in outs] + [jax.ShapeDtypeStruct((bsz * t, GROUP_WIDTH), F32)],
        compiler_params=_cparams(("parallel",)),
    )(p_attn, rope_c, rope_sa, rope_sb, gq, gk, bd)


def _nt_dot(a, b):
    return lax.dot_general(a, b, (((1,), (1,)), ((), ())), preferred_element_type=F32)


def _softmax_chunks(score_fn, v_ref, s_ref, rows, dv, n_chunks, tk, n_ctx_keys, is_ctx):
    def update(s, start, carry):
        m, acc = carry
        m_new = jnp.maximum(m, jnp.max(s, axis=-1, keepdims=True))
        p = jnp.exp2(s - m_new)
        acc = jnp.exp2(m - m_new) * acc + jnp.dot(p.astype(BF16), v_ref[pl.ds(start, tk), :],
                                                  preferred_element_type=F32)
        return m_new, acc

    init = (jnp.full((rows, 1), -1e30, F32), jnp.zeros((rows, v_ref.shape[-1]), F32))

    def ctx_tile():
        s = score_fn(0, tk)
        col = lax.broadcasted_iota(jnp.int32, s.shape, 1)
        return update(jnp.where(col < n_ctx_keys, s, -1e30), 0, init)

    def lat_tile():
        stats = init
        s_ref[0] = score_fn(0, tk)
        for j in range(n_chunks):
            if j + 1 < n_chunks:
                s_ref[(j + 1) % 2] = score_fn((j + 1) * tk, tk)
            stats = update(s_ref[j % 2], j * tk, stats)
        return stats

    _, acc = lax.cond(is_ctx, ctx_tile, lat_tile)
    return acc[:, :dv] / acc[:, dv:dv + 1]


def _gqa_kernel(q_ref, k_ref, v_ref, beta_ref, o_ref, s_ref, *, tq, tk, n_chunks, n_ctx_tiles, n_ctx_keys):
    is_ctx = pl.program_id(2) < n_ctx_tiles
    q = q_ref[...].reshape(GQA_REP * tq, HEAD_DIM)

    def scores(start, size):
        return _nt_dot(q, k_ref[pl.ds(start, size), :])

    o = _softmax_chunks(scores, v_ref, s_ref, GQA_REP * tq, HEAD_DIM, n_chunks, tk, n_ctx_keys, is_ctx)
    for r in range(GQA_REP):
        sl = slice(HEAD_DIM * r, HEAD_DIM * (r + 1))
        o_ref[:, sl] = (o[r * tq:(r + 1) * tq] * beta_ref[:, sl]).astype(o_ref.dtype)


def gqa_attention(q, k, v, beta, *, n_t, n_c, tk):
    bsz, _, t, _ = q.shape
    tq = ROW_TILE
    kern = functools.partial(_gqa_kernel, tq=tq, tk=tk, n_chunks=t // tk, n_ctx_tiles=n_c,
                             n_ctx_keys=n_c * ROW_TILE)
    kspec = pl.BlockSpec((None, None, t, HEAD_DIM), lambda b, g, i: (b, g, 0, 0))
    vspec = pl.BlockSpec((None, None, t, v.shape[-1]), lambda b, g, i: (b, g, 0, 0))
    w = GQA_REP * HEAD_DIM
    return _pcall(
        kern,
        grid=(bsz, GQA_KV_HEADS, n_t),
        in_specs=[pl.BlockSpec((None, GQA_REP, tq, HEAD_DIM), lambda b, g, i: (b, g, i, 0)), kspec, vspec,
                  pl.BlockSpec((1, w), lambda b, g, i: (0, g))],
        out_specs=pl.BlockSpec((tq, w), lambda b, g, i: (b * n_t + i, g)),
        out_shape=jax.ShapeDtypeStruct((bsz * t, GROUP_WIDTH), BF16),
        scratch_shapes=[pltpu.VMEM((2, GQA_REP * tq, tk), F32)],
        compiler_params=_cparams(("parallel", "parallel", "arbitrary")),
    )(q, k, v, beta)


def _diff_kernel(q_ref, k_ref, v_ref, lam_ref, g_ref, beta_ref, o_ref, s_ref, *, tq, tk, n_chunks, n_ctx_tiles,
                 n_ctx_keys, out_scale):
    is_ctx = pl.program_id(2) < n_ctx_tiles
    q0, q1 = q_ref[0], q_ref[1]

    def scores(start, size):
        return jnp.concatenate([_nt_dot(q0, k_ref[0, pl.ds(start, size), :]),
                                _nt_dot(q1, k_ref[1, pl.ds(start, size), :])], axis=0)

    a = _softmax_chunks(scores, v_ref, s_ref, 2 * tq, 2 * HEAD_DIM, n_chunks, tk, n_ctx_keys, is_ctx)
    o = a[:tq] - lam_ref[...] * a[tq:]
    o = _rms(o) * g_ref[...] * out_scale
    o_ref[...] = (o * beta_ref[...]).astype(o_ref.dtype)


def diff_attention(q, k, v, lam, subln_g, beta, out_scale, *, n_t, n_c, tk):
    bsz, _, t, _ = q.shape
    tq = ROW_TILE
    dv = 2 * HEAD_DIM
    kern = functools.partial(_diff_kernel, tq=tq, tk=tk, n_chunks=t // tk, n_ctx_tiles=n_c,
                             n_ctx_keys=n_c * ROW_TILE, out_scale=out_scale)
    return _pcall(
        kern,
        grid=(bsz, DIFF_HEADS, n_t),
        in_specs=[pl.BlockSpec((None, 2, tq, HEAD_DIM), lambda b, h, i: (b, h, i, 0)),
                  pl.BlockSpec((None, 2, t, HEAD_DIM), lambda b, h, i: (b, h, 0, 0)),
                  pl.BlockSpec((None, None, t, v.shape[-1]), lambda b, h, i: (b, h, 0, 0)),
                  pl.BlockSpec((1, 1), lambda b, h, i: (0, 0)),
                  pl.BlockSpec((1, dv), lambda b, h, i: (0, 0)),
                  pl.BlockSpec((1, dv), lambda b, h, i: (0, h))],
        out_specs=pl.BlockSpec((tq, dv), lambda b, h, i: (b * n_t + i, h)),
        out_shape=jax.ShapeDtypeStruct((bsz * t, GROUP_WIDTH), BF16),
        scratch_shapes=[pltpu.VMEM((2, 2 * tq, tk), F32)],
        compiler_params=_cparams(("parallel", "parallel", "arbitrary")),
    )(q, k, v, lam, subln_g, beta)


def _ret_kernel(lg_ref, q_ref, k_ref, v_ref, o_ref, state_ref, decay_ref, xi_ref, zeta_ref):
    d = pl.program_id(1)
    c = ROW_TILE
    dv = v_ref.shape[-1]

    @pl.when(pl.program_id(2) == 0)
    def _():
        state_ref[...] = jnp.zeros_like(state_ref)
        fwd = d == 0
        ii = lax.broadcasted_iota(jnp.int32, (c, c), 0)
        jj = lax.broadcasted_iota(jnp.int32, (c, c), 1)
        rel = jnp.where(fwd, ii - jj, jj - ii).astype(F32)
        pos = lax.broadcasted_iota(jnp.int32, (c, 1), 0).astype(F32)
        for h in range(RET_HEADS):
            lg = lg_ref[h]
            decay_ref[h] = jnp.where(rel >= 0, jnp.exp(jnp.maximum(rel, 0.0) * lg), 0.0)
            xi_ref[h] = jnp.exp(jnp.where(fwd, pos + 1.0, c - pos) * lg)
            zeta_ref[h] = jnp.exp(jnp.where(fwd, c - 1.0 - pos, pos) * lg)

    for h in range(RET_HEADS):
        q, k, v = q_ref[h], k_ref[h], v_ref[h]
        state = state_ref[h]
        scores = _nt_dot(q, k) * decay_ref[h]
        inner = jnp.dot(scores.astype(BF16), v, preferred_element_type=F32)
        cross = jnp.dot(q, state.astype(BF16), preferred_element_type=F32) * xi_ref[h]
        o_ref[:, dv * h:dv * (h + 1)] = inner + cross
        kz = (k.astype(F32) * zeta_ref[h]).astype(BF16)
        upd = lax.dot_general(kz, v, (((0,), (0,)), ((), ())), preferred_element_type=F32)
        state_ref[h] = jnp.exp(c * lg_ref[h]) * state + upd


def retention(q, k, v, log_decay, *, n_t, n_c):
    bsz, nh, t, dk = q.shape
    dv = v.shape[-1]

    def blk(d, j):
        back = jnp.where(j < n_c, n_c - 1 - j, n_t - 1 - (j - n_c))
        return jnp.where(d == 0, j, back)

    qk = pl.BlockSpec((None, nh, ROW_TILE, dk), lambda b, d, j: (b, 0, blk(d, j), 0))
    return _pcall(
        _ret_kernel,
        grid=(bsz, 2, n_t),
        in_specs=[pl.BlockSpec((None, nh, 1, 1), lambda b, d, j: (d, 0, 0, 0)), qk, qk,
                  pl.BlockSpec((None, nh, ROW_TILE, dv), lambda b, d, j: (b, 0, blk(d, j), 0))],
        out_specs=pl.BlockSpec((None, ROW_TILE, nh * dv), lambda b, d, j: (d, b * n_t + blk(d, j), 0)),
        out_shape=jax.ShapeDtypeStruct((2, bsz * t, nh * dv), F32),
        scratch_shapes=[pltpu.VMEM((nh, dk, dv), F32), pltpu.VMEM((nh, ROW_TILE, ROW_TILE), F32),
                        pltpu.VMEM((nh, ROW_TILE, 1), F32), pltpu.VMEM((nh, ROW_TILE, 1), F32)],
        compiler_params=_cparams(("parallel", "arbitrary", "arbitrary")),
    )(log_decay.reshape(2, RET_HEADS, 1, 1), q, k, v)


def _ret_gate_kernel(of_ref, ob_ref, g_ref, gn_ref, beta_ref, o_ref):
    dv = 2 * HEAD_DIM
    for h in range(RET_HEADS):
        sl = slice(dv * h, dv * (h + 1))
        o = of_ref[:, sl] + ob_ref[:, sl]
        mu = jnp.mean(o, axis=-1, keepdims=True)
        var = jnp.mean(jnp.square(o - mu), axis=-1, keepdims=True)
        y = (o - mu) * lax.rsqrt(var + NORM_EPS) * gn_ref[:, sl]
        g = g_ref[:, sl]
        o_ref[:, sl] = (g * jax.nn.sigmoid(g) * y * beta_ref[:, sl]).astype(o_ref.dtype)


def ret_gate(o2, gate, gn_g, beta):
    _, r, w = o2.shape
    vec = pl.BlockSpec((1, w), lambda i: (0, 0))
    row = pl.BlockSpec((ROW_TILE, w), lambda i: (i, 0))
    return _pcall(
        _ret_gate_kernel,
        grid=(r // ROW_TILE,),
        in_specs=[pl.BlockSpec((None, ROW_TILE, w), lambda i: (0, i, 0)),
                  pl.BlockSpec((None, ROW_TILE, w), lambda i: (1, i, 0)), row, vec, vec],
        out_specs=row,
        out_shape=jax.ShapeDtypeStruct((r, w), BF16),
        compiler_params=_cparams(("parallel",)),
    )(o2, o2, gate, gn_g, beta)


def _short_conv_kernel(cur_ref, prev_ref, next_ref, w_ref, b_ref, z_ref, x1_ref, x2_ref, *, n_seq_tiles):
    j = pl.program_id(1)

    @pl.when(j < n_seq_tiles)
    def _():
        u = cur_ref[...]
        rows = lax.broadcasted_iota(jnp.int32, u.shape, 0)
        prev_row = jnp.where(j == 0, 0.0, prev_ref[7:8, :])
        next_row = jnp.where(j == n_seq_tiles - 1, 0.0, next_ref[0:1, :])
        up = jnp.where(rows == 0, prev_row, pltpu.roll(u, 1, 0))
        un = jnp.where(rows == ROW_TILE - 1, next_row, pltpu.roll(u, ROW_TILE - 1, 0))
        y = up * w_ref[0:1, :] + u * w_ref[1:2, :] + un * w_ref[2:3, :] + b_ref[...]
        z_ref[...] = y[:, :HY_WIDTH]
        x1_ref[...] = y[:, HY_WIDTH:2 * HY_WIDTH]
        x2_ref[...] = y[:, 2 * HY_WIDTH:]

    @pl.when(j >= n_seq_tiles)
    def _():
        z_ref[...] = jnp.zeros_like(z_ref)
        x1_ref[...] = jnp.zeros_like(x1_ref)
        x2_ref[...] = jnp.zeros_like(x2_ref)


def short_conv(p_hy, w, b, *, bsz, n_t, first_tile, n_seq_tiles, n_pad_tiles):
    w3 = 3 * HY_WIDTH
    sub = ROW_TILE // 8
    last_blk8 = p_hy.shape[0] // 8 - 1

    def cur(bb, j):
        return (bb * n_t + first_tile + jnp.minimum(j, n_seq_tiles - 1), 0)

    def prev(bb, j):
        return (jnp.maximum(cur(bb, j)[0] * sub - 1, 0), 0)

    def nxt(bb, j):
        return (jnp.minimum((cur(bb, j)[0] + 1) * sub, last_blk8), 0)

    kern = functools.partial(_short_conv_kernel, n_seq_tiles=n_seq_tiles)
    rows = bsz * n_pad_tiles * ROW_TILE
    return _pcall(
        kern,
        grid=(bsz, n_pad_tiles),
        in_specs=[pl.BlockSpec((ROW_TILE, w3), cur), pl.BlockSpec((8, w3), prev), pl.BlockSpec((8, w3), nxt),
                  pl.BlockSpec((3, w3), lambda bb, j: (0, 0)), pl.BlockSpec((1, w3), lambda bb, j: (0, 0))],
        out_specs=[pl.BlockSpec((ROW_TILE, HY_WIDTH), lambda bb, j: (bb * n_pad_tiles + j, 0))] * 3,
        out_shape=[jax.ShapeDtypeStruct((rows, HY_WIDTH), F32)] * 3,
        compiler_params=_cparams(("parallel", "arbitrary")),
    )(p_hy, p_hy, p_hy, w, b.reshape(1, w3))


def _filter_kernel(feat_ref, w1_ref, b1_ref, w2_ref, b2_ref, w3_ref, win_ref, h_ref, asum_ref):
    i = pl.program_id(0)
    hp = lax.Precision.HIGHEST
    h = jnp.sin(jnp.dot(feat_ref[...], w1_ref[...], precision=hp, preferred_element_type=F32) + b1_ref[...])
    h = jnp.sin(jnp.dot(h, w2_ref[...], precision=hp, preferred_element_type=F32) + b2_ref[...])
    h = jnp.dot(h, w3_ref[...], precision=hp, preferred_element_type=F32)
    win = win_ref[...]
    h = h * jnp.concatenate([win] * (2 * HY_ORDER), axis=-1)
    h_ref[...] = h
    rows = lax.broadcasted_iota(jnp.int32, h.shape, 0) + i * ROW_TILE
    cols = lax.broadcasted_iota(jnp.int32, h.shape, 1)
    is_bwd = (cols // HY_WIDTH) % 2 == 1
    part = jnp.sum(jnp.where(is_bwd & (rows == 0), 0.0, jnp.abs(h)), axis=0, keepdims=True)

    @pl.when(i == 0)
    def _():
        asum_ref[...] = part

    @pl.when(i > 0)
    def _():
        asum_ref[...] += part


def hyena_filter_taps(length, w1, b1, w2, b2, w3):
    t = jnp.arange(length, dtype=F32)
    t_norm = t / length
    f = jnp.linspace(1e-4, HY_BANDS - 1, HY_BANDS, dtype=F32)
    wt = 2.0 * math.pi * t_norm
    feats = jnp.concatenate([t_norm[:, None], jnp.cos(wt[:, None] * f), -jnp.sin(wt[:, None] * f)], axis=-1)
    feats = jnp.pad(feats, ((0, 0), (0, 128 - HY_EMB_DIM)))
    w1p = jnp.pad(w1, ((0, 128 - HY_EMB_DIM), (0, 0)))
    deltas = jnp.abs(jnp.linspace(HY_MIN_DECAY, HY_MAX_DECAY, HY_WIDTH, dtype=F32))
    window = jnp.exp(-t_norm[:, None] * deltas[None])
    wout = HY_ORDER * 2 * HY_WIDTH
    full = lambda shp: pl.BlockSpec(shp, lambda i: (0, 0))
    return _pcall(
        _filter_kernel,
        grid=(length // ROW_TILE,),
        in_specs=[pl.BlockSpec((ROW_TILE, 128), lambda i: (i, 0)), full((128, HY_FILT_HIDDEN)),
                  full((1, HY_FILT_HIDDEN)), full((HY_FILT_HIDDEN, HY_FILT_HIDDEN)), full((1, HY_FILT_HIDDEN)),
                  full((HY_FILT_HIDDEN, wout)), pl.BlockSpec((ROW_TILE, HY_WIDTH), lambda i: (i, 0))],
        out_specs=[pl.BlockSpec((ROW_TILE, wout), lambda i: (i, 0)), full((1, wout))],
        out_shape=[jax.ShapeDtypeStruct((length, wout), F32), jax.ShapeDtypeStruct((1, wout), F32)],
        compiler_params=_cparams(("arbitrary",)),
    )(feats, w1p, b1.reshape(1, -1), w2, b2.reshape(1, -1), w3, window)


class _Dft:
    def __init__(self, n1):
        n2 = FFT_N2
        assert n1 % 16 == 0
        self.n1, self.n = n1, n1 * n2
        self.half = n1 // 2
        self.nf = n1 // 2 + 1
        self.nfp = -(-self.nf // 16) * 16
        f1 = np.arange(self.nf)[:, None]
        ang = 2.0 * np.pi * f1 * np.arange(n1)[None, :] / n1
        s1 = np.zeros((2 * self.nfp, n1))
        s1[:self.nf] = np.cos(ang)
        s1[self.nfp:self.nfp + self.nf] = -np.sin(ang)
        self.s1_full = s1
        tw = 2.0 * np.pi * f1 * np.arange(n2)[None, :] / self.n
        self.tw_cos = np.cos(tw)[:, :, None]
        self.tw_sin = np.sin(tw)[:, :, None]
        a2 = 2.0 * np.pi * np.arange(n2)[:, None] * np.arange(n2)[None, :] / n2
        wc, ws = np.cos(a2), np.sin(a2)
        self.m_fwd = np.block([[wc, ws], [-ws, wc]])
        self.m_inv = np.block([[wc, -ws], [ws, wc]])
        wgt = np.full(self.nf, 2.0)
        wgt[0] = 1.0
        wgt[-1] = 1.0
        ango = 2.0 * np.pi * np.arange(self.half)[:, None] * np.arange(self.nf)[None, :] / n1
        self.s4_re = np.zeros((self.half, self.nfp))
        self.s4_im = np.zeros((self.half, self.nfp))
        self.s4_re[:, :self.nf] = np.cos(ango) * wgt / self.n
        self.s4_im[:, :self.nf] = -np.sin(ango) * wgt / self.n

    @staticmethod
    def const(a, dtype=BF16):
        return jnp.asarray(a, F32).astype(dtype)


def _fft_s1_kernel(m_ref, x_ref, o_ref):
    k_rows, n_sub, w = x_ref.shape
    x = x_ref[...].reshape(k_rows * n_sub, w)
    y = jnp.dot(m_ref[...], x.astype(BF16), preferred_element_type=F32)
    o_ref[...] = y.reshape(o_ref.shape)


def fft_stage1(mat, x3, *, n_batch, k_rows):
    w = x3.shape[-1]
    m = mat.shape[0]
    big = _Dft.const(np.kron(mat, np.eye(FFT_SUB)))
    return _pcall(
        _fft_s1_kernel,
        grid=(n_batch, FFT_N2 // FFT_SUB),
        in_specs=[pl.BlockSpec((m * FFT_SUB, k_rows * FFT_SUB), lambda b, j: (0, 0)),
                  pl.BlockSpec((k_rows, FFT_SUB, w), lambda b, j: (b, j, 0))],
        out_specs=pl.BlockSpec((None, m, FFT_SUB, w), lambda b, j: (b, 0, j, 0)),
        out_shape=jax.ShapeDtypeStruct((n_batch, m, FFT_N2, w), F32),
        compiler_params=_cparams(("parallel", "parallel")),
    )(big, x3)


def _twiddle(ar, ai, c, s):
    return ar * c + ai * s, ai * c - ar * s


def _fft_filter_mid_kernel(ar_ref, ai_ref, c_ref, s_ref, mf_ref, sc_ref, hb0_ref, kr_ref, ki_ref):
    br, bi = _twiddle(ar_ref[...], ai_ref[...], c_ref[...], s_ref[...])
    x = jnp.dot(mf_ref[...], jnp.concatenate([br, bi], axis=0).astype(BF16), preferred_element_type=F32)
    xr, xi = x[:FFT_N2], x[FFT_N2:]
    w = HY_WIDTH
    for o in range(HY_ORDER):
        fw = slice(2 * o * w, (2 * o + 1) * w)
        bw = slice((2 * o + 1) * w, (2 * o + 2) * w)
        oc = slice(o * w, (o + 1) * w)
        kr_ref[:, oc] = (xr[:, fw] + xr[:, bw] - hb0_ref[:, oc]) * sc_ref[:, oc]
        ki_ref[:, oc] = (xi[:, fw] - xi[:, bw]) * sc_ref[:, oc]


def fft_filter_mid(a, dft, kscale, hb0):
    cols = a.shape[-1]
    n2 = FFT_N2
    tw = pl.BlockSpec((None, n2, 1), lambda f: (f, 0, 0))
    blk = pl.BlockSpec((None, n2, cols // 2), lambda f: (f, 0, 0))
    vec = pl.BlockSpec((1, cols // 2), lambda f: (0, 0))
    return _pcall(
        _fft_filter_mid_kernel,
        grid=(dft.nf,),
        in_specs=[pl.BlockSpec((None, n2, cols), lambda f: (f, 0, 0)),
                  pl.BlockSpec((None, n2, cols), lambda f: (dft.nfp + f, 0, 0)), tw, tw,
                  pl.BlockSpec((2 * n2, 2 * n2), lambda f: (0, 0)), vec, vec],
        out_specs=[blk, blk],
        out_shape=[jax.ShapeDtypeStruct((dft.nf, n2, cols // 2), F32)] * 2,
        compiler_params=_cparams(("parallel",)),
    )(a, a, _Dft.const(dft.tw_cos, F32), _Dft.const(dft.tw_sin, F32), _Dft.const(dft.m_fwd), kscale, hb0)


def _fft_mid_kernel(ar_ref, ai_ref, c_ref, s_ref, mf_ref, mi_ref, kr_ref, ki_ref, er_ref, ei_ref, *, nf):
    for u in range(FFT_MID_F):
        f1 = pl.program_id(1) * FFT_MID_F + u

        @pl.when(f1 < nf)
        def _():
            c, s = c_ref[u], s_ref[u]
            br, bi = _twiddle(ar_ref[u], ai_ref[u], c, s)
            x = jnp.dot(mf_ref[...], jnp.concatenate([br, bi], axis=0).astype(BF16), preferred_element_type=F32)
            xr, xi = x[:FFT_N2], x[FFT_N2:]
            kr, ki = kr_ref[u], ki_ref[u]
            yr = xr * kr - xi * ki
            yi = xr * ki + xi * kr
            dd = jnp.dot(mi_ref[...], jnp.concatenate([yr, yi], axis=0).astype(BF16), preferred_element_type=F32)
            dr, di = dd[:FFT_N2], dd[FFT_N2:]
            er_ref[u] = dr * c - di * s
            ei_ref[u] = di * c + dr * s

        @pl.when(f1 >= nf)
        def _():
            er_ref[u] = jnp.zeros((FFT_N2, er_ref.shape[-1]), F32)
            ei_ref[u] = jnp.zeros((FFT_N2, ei_ref.shape[-1]), F32)


def fft_mid(a, dft, kr, ki, order):
    nb = a.shape[0]
    n2, w, g = FFT_N2, HY_WIDTH, FFT_MID_F
    nf, nfp = dft.nf, dft.nfp
    assert nfp % g == 0
    last = (nf - 1) // g
    fc = lambda f: jnp.minimum(f, last)
    tw = pl.BlockSpec((g, n2, 1), lambda b, f: (fc(f), 0, 0))
    mat = pl.BlockSpec((2 * n2, 2 * n2), lambda b, f: (0, 0))
    kf = pl.BlockSpec((g, n2, w), lambda b, f: (fc(f), 0, order))
    out = pl.BlockSpec((None, g, n2, w), lambda b, f: (b, f, 0, 0))
    return _pcall(
        functools.partial(_fft_mid_kernel, nf=nf),
        grid=(nb, nfp // g),
        in_specs=[pl.BlockSpec((None, g, n2, w), lambda b, f: (b, fc(f), 0, 0)),
                  pl.BlockSpec((None, g, n2, w), lambda b, f: (b, nfp // g + fc(f), 0, 0)),
                  tw, tw, mat, mat, kf, kf],
        out_specs=[out, out],
        out_shape=[jax.ShapeDtypeStruct((nb, nfp, n2, w), F32)] * 2,
        compiler_params=_cparams(("parallel", "arbitrary")),
    )(a, a, _Dft.const(dft.tw_cos, F32), _Dft.const(dft.tw_sin, F32), _Dft.const(dft.m_fwd),
      _Dft.const(dft.m_inv), kr, ki)


def _fft_s4_kernel(mr_ref, mi_ref, er_ref, ei_ref, gate_ref, z_ref, bias_ref, scale_ref, o_ref):
    nfp, n_sub, w = er_ref.shape
    er = er_ref[...].reshape(nfp * n_sub, w).astype(BF16)
    ei = ei_ref[...].reshape(nfp * n_sub, w).astype(BF16)
    y = (jnp.dot(mr_ref[...], er, preferred_element_type=F32)
         + jnp.dot(mi_ref[...], ei, preferred_element_type=F32)).reshape(o_ref.shape)
    o_ref[...] = gate_ref[...] * (y + bias_ref[...] * z_ref[...]) * scale_ref[...]


def fft_stage4(dft, er, ei, gate, z, bias_row, scale_row):
    nb = er.shape[0]
    m, nfp = dft.half, dft.nfp
    w = z.shape[-1]
    eye = np.eye(FFT_SUB)
    row = pl.BlockSpec((m, FFT_SUB, w), lambda b, j: (b, j, 0))
    vec = pl.BlockSpec((1, 1, w), lambda b, j: (0, 0, 0))
    mat = pl.BlockSpec((m * FFT_SUB, nfp * FFT_SUB), lambda b, j: (0, 0))
    spec = pl.BlockSpec((None, nfp, FFT_SUB, w), lambda b, j: (b, 0, j, 0))
    return _pcall(
        _fft_s4_kernel,
        grid=(nb, FFT_N2 // FFT_SUB),
        in_specs=[mat, mat, spec, spec, row, row, vec, vec],
        out_specs=row,
        out_shape=jax.ShapeDtypeStruct((nb * m, FFT_N2, w), F32),
        compiler_params=_cparams(("parallel", "parallel")),
    )(_Dft.const(np.kron(dft.s4_re, eye)), _Dft.const(np.kron(dft.s4_im, eye)), er, ei, gate, z,
      bias_row.reshape(1, 1, w), scale_row.reshape(1, 1, w))


def hyena_long_conv(z, x1, x2, filt, hbias, beta_hy, *, bsz, length, n1):
    dft = _Dft(n1)
    n2, w = FFT_N2, HY_WIDTH
    half = dft.half
    assert half >= 8 and length % n2 == 0
    taps, asum = filt
    asum = asum.reshape(HY_ORDER, 2, w)
    kscale = (1.0 / (asum[:, 0] + asum[:, 1] + NORM_EPS)).reshape(1, HY_ORDER * w)
    hb0 = taps[0].reshape(HY_ORDER, 2, w)[:, 1].reshape(1, HY_ORDER * w)
    k_taps = length // n2
    a_f = fft_stage1(dft.s1_full[:, :k_taps], taps.reshape(k_taps, n2, 2 * HY_ORDER * w), n_batch=1, k_rows=k_taps)
    kr, ki = fft_filter_mid(a_f[0], dft, kscale, hb0)
    s1 = dft.s1_full[:, :half]
    ones = jnp.ones((1, w), F32)
    shape3 = (bsz * half, n2, w)
    zc = z.reshape(shape3)
    for o, gate in enumerate((x1, x2)):
        a = fft_stage1(s1, zc, n_batch=bsz, k_rows=half)
        er, ei = fft_mid(a, dft, kr, ki, o)
        last = o == HY_ORDER - 1
        zc = fft_stage4(dft, er, ei, gate.reshape(shape3), zc, hbias[o].reshape(1, w),
                        beta_hy.reshape(1, w) if last else ones)
    return zc.reshape(bsz * half * n2, w)


def _rope_tables(n_ctx, n_lat):
    rows = n_lat // GRID_W
    row = jnp.repeat(jnp.arange(rows, dtype=F32), GRID_W)
    col = jnp.tile(jnp.arange(GRID_W, dtype=F32), rows)
    n_freq = HEAD_DIM // 4
    freqs = ROPE_THETA ** (-jnp.arange(n_freq, dtype=F32) / n_freq)
    ar = row[:, None] * freqs
    ac = col[:, None] * freqs
    cos = jnp.concatenate([jnp.cos(ar), jnp.cos(ar), jnp.cos(ac), jnp.cos(ac)], axis=-1)
    sin = jnp.concatenate([jnp.sin(ar), jnp.sin(ar), jnp.sin(ac), jnp.sin(ac)], axis=-1)
    cos = jnp.concatenate([jnp.ones((n_ctx, HEAD_DIM), F32), cos], axis=0)
    sin = jnp.concatenate([jnp.zeros((n_ctx, HEAD_DIM), F32), sin], axis=0)
    even = (np.arange(HEAD_DIM) // n_freq) % 2 == 0
    sin_a = jnp.where(even, -sin, 0.0)
    sin_b = jnp.where(even, 0.0, sin)
    tile2 = lambda a: jnp.concatenate([a, a], axis=-1)
    return tile2(cos), tile2(sin_a), tile2(sin_b)


def _attn_key_chunk(t):
    for tk in (1408, 1280, 1024, 768, 512, 256):
        if t % tk == 0:
            return tk
    raise ValueError(t)


def kernel(x, c, ctx, c_ctx, ada_w, ada_b, norm_g, w_in, w_out, mix_beta, gqa_qk_g, diff_lambda, diff_subln_g,
           ret_log_decay, ret_gn_g, hy_short_w, hy_short_b, hy_filt_w1, hy_filt_b1, hy_filt_w2, hy_filt_b2,
           hy_filt_w3, hy_bias, ffn_w_gate, ffn_w_up, ffn_w_down, moe_router, moe_w_gate, moe_w_up, moe_w_down):
    bsz, n_lat, d = x.shape
    n_ctx = ctx.shape[1]
    t = n_ctx + n_lat
    r = bsz * t
    n_t, n_c = t // ROW_TILE, n_ctx // ROW_TILE
    n_s = n_lat // ROW_TILE
    assert n_ctx % ROW_TILE == 0 and n_lat % ROW_TILE == 0 and r % 512 == 0
    tk = _attn_key_chunk(t)
    assert tk >= n_ctx

    rope_c, rope_sa, rope_sb = _rope_tables(n_ctx, n_lat)
    xs = jnp.concatenate([ctx, x], axis=1).reshape(r, d)

    cvec = jnp.stack([jnp.broadcast_to(c_ctx, c.shape), c], axis=1).reshape(2 * bsz, d)
    cvec = jax.nn.silu(cvec)
    cvec = jnp.pad(cvec, ((0, 16 - 2 * bsz), (0, 0))).astype(BF16)

    n1_lat = 2 * n_lat // FFT_N2
    ctx_pad_tiles = max(n_c, 1024 // ROW_TILE)
    n1_ctx = 2 * ctx_pad_tiles * ROW_TILE // FFT_N2

    mods = []
    for l in range(DEPTH):
        mod = mm(cvec, ada_w, tn=1536, out_dtype=F32, lead=(l,))[:2 * bsz] + ada_b[l]
        mods.append(mod.reshape(2 * bsz, 6, 1, d))

    (h,) = norm_mod(xs, norm_g[0, 0], mods[0][:, 0], mods[0][:, 1], n_t, n_c, (BF16,))
    for l in range(DEPTH):
        lambda_init = 0.8 - 0.6 * math.exp(-0.3 * l)
        sh_m, sc_m, gt_m, sh_f, sc_f, gt_f = (mods[l][:, k] for k in range(6))
        beta = mix_beta[l].reshape(1, 4 * GROUP_WIDTH)
        beta_a, beta_b, beta_r, beta_d = (beta[:, GROUP_WIDTH * k:GROUP_WIDTH * (k + 1)] for k in range(4))
        moe_layer = l % 2 == 1
        i = l // 2
        nxt_mixer = (norm_g[l + 1, 0], mods[l + 1][:, 0], mods[l + 1][:, 1], BF16) if l + 1 < DEPTH else None

        p_attn = mm(h, w_in, tn=768, out_dtype=F32, lead=(l,), n_cols=COL_ATTN)
        p_hy = mm(h, w_in, tn=768, out_dtype=F32, lead=(l,), col_off=COL_HY // 768,
                  n_cols=IN_WIDTH - COL_HY)
        aq, ak, av, dq, dk, dv, rq, rk, rv, rg = prep_heads(p_attn, rope_c, rope_sa, rope_sb, gqa_qk_g[l], bsz, n_t)

        a_out = gqa_attention(aq, ak, av, beta_a, n_t=n_t, n_c=n_c, tk=tk)

        lamf = diff_lambda[l]
        lam_full = (jnp.exp(jnp.sum(lamf[0] * lamf[1])) - jnp.exp(jnp.sum(lamf[2] * lamf[3])) + lambda_init)
        b_out = diff_attention(dq, dk, dv, lam_full.reshape(1, 1), diff_subln_g[l].reshape(1, -1), beta_b,
                               1.0 - lambda_init, n_t=n_t, n_c=n_c, tk=tk)

        o2 = retention(rq, rk, rv, ret_log_decay[l], n_t=n_t, n_c=n_c)
        r_out = ret_gate(o2, rg, ret_gn_g[l].reshape(1, -1), beta_r)

        filt_args = (hy_filt_w1[l], hy_filt_b1[l], hy_filt_w2[l], hy_filt_b2[l], hy_filt_w3[l])
        z_l, x1_l, x2_l = short_conv(p_hy, hy_short_w[l], hy_short_b[l], bsz=bsz, n_t=n_t, first_tile=n_c,
                                     n_seq_tiles=n_s, n_pad_tiles=n_s)
        d_l = hyena_long_conv(z_l, x1_l, x2_l, hyena_filter_taps(n_lat, *filt_args), hy_bias[l], beta_d[0],
                              bsz=bsz, length=n_lat, n1=n1_lat)
        z_c, x1_c, x2_c = short_conv(p_hy, hy_short_w[l], hy_short_b[l], bsz=bsz, n_t=n_t, first_tile=0,
                                     n_seq_tiles=n_c, n_pad_tiles=ctx_pad_tiles)
        d_c = hyena_long_conv(z_c, x1_c, x2_c, hyena_filter_taps(n_ctx, *filt_args), hy_bias[l], beta_d[0],
                              bsz=bsz, length=n_ctx, n1=n1_ctx)
        d_out = jnp.concatenate([d_c.reshape(bsz, -1, HY_WIDTH)[:, :n_ctx], d_l.reshape(bsz, n_lat, HY_WIDTH)],
                                axis=1).reshape(r, HY_WIDTH).astype(BF16)

        y = mm_parts([a_out, b_out, r_out, d_out], w_out, tn=512, lead=(l,))
        xs, f_in = gate_res(xs, y, norm_g[l, 1], gt_m, n_t, n_c,
                            nxt=(norm_g[l, 2], sh_f, sc_f, F32 if moe_layer else BF16))

        if not moe_layer:
            hid = swiglu_up(f_in, ffn_w_gate, ffn_w_up, tn=512, lead=(i,))
            f_out = mm(hid, ffn_w_down, tn=512, out_dtype=F32, lead=(i,), max_tile_mb=6)
            res = gate_res(xs, f_out, norm_g[l, 3], gt_f, n_t, n_c, nxt=nxt_mixer)
        else:
            top_i, top_w = router(f_in, moe_router[i].T)
            n_pad_rows = (2 * r // MOE_TM + N_EXPERTS) * MOE_TM
            row_token, row_weight, tile_expert, n_used, pos = route_tokens(top_i, top_w, n_pad_rows)
            y_sorted = moe_experts(f_in, moe_w_gate, moe_w_up, moe_w_down, i, row_token, row_weight,
                                   tile_expert, n_used)
            res = moe_combine_gate_res(xs, y_sorted, pos, norm_g[l, 3], gt_f, n_t, n_c, nxt=nxt_mixer)
        xs, h = res if nxt_mixer is not None else (res, None)

    return xs.reshape(bsz, t, d)[:, n_ctx:]
```

```python
import functools
import math

import numpy as np
import jax
import jax.numpy as jnp
from jax import lax
from jax.experimental import pallas as pl
from jax.experimental.pallas import tpu as pltpu

F32 = jnp.float32
BF16 = jnp.bfloat16

D_MODEL = 2048
DEPTH = 4
GRID_W = 64
HEAD_DIM = 64
ROPE_THETA = 10000.0
NORM_EPS = 1e-6
GROUP_WIDTH = 512
GQA_Q_HEADS = 8
GQA_KV_HEADS = 2
GQA_REP = 4
DIFF_HEADS = 4
RET_HEADS = 4
HY_WIDTH = 512
HY_ORDER = 2
HY_EMB_DIM = 33
HY_BANDS = 16
HY_FILT_HIDDEN = 64
HY_MIN_DECAY = math.log(1e-2) / 1.5
HY_MAX_DECAY = math.log(1e-2) / 0.3
N_EXPERTS = 8

COL_ATTN = 3840
COL_DIFF = 768
COL_RET = 2304
COL_RET_GATE = 3328
COL_HY = 3840
IN_WIDTH = 5376

ROW_TILE = 256
MOE_TM = 1024
MOE_TF = 256
MOE_STEPS = 4096 // MOE_TF
FFT_N2 = 128
FFT_MID_F = 4
FFT_SUB = 8
VMEM_LIMIT_MB = 48


def _cparams(sem, vmem_mb=VMEM_LIMIT_MB):
    return pltpu.CompilerParams(dimension_semantics=sem, vmem_limit_bytes=vmem_mb * 1024 * 1024)


def _pcall(kern, **kw):
    fn = kern.func if isinstance(kern, functools.partial) else kern
    return pl.pallas_call(kern, name=fn.__name__.strip("_"), **kw)


def _pick_tm(m, k, max_tile_mb):
    for tm in (1536, 1280, 1024, 768, 512, 256, 128, 16):
        if m % tm == 0 and tm * k * 2 <= max_tile_mb * 2 ** 20:
            return tm
    raise ValueError((m, k))


def _mm_kernel(a_ref, b_ref, o_ref, bb_ref):
    @pl.when(pl.program_id(1) == 0)
    def _():
        bb_ref[...] = b_ref[...].astype(BF16)

    o_ref[...] = jnp.dot(a_ref[...], bb_ref[...], preferred_element_type=F32).astype(o_ref.dtype)


def _b_spec(b, lead, k, tn, col_off):
    nlead = len(lead)
    return pl.BlockSpec((None,) * nlead + (k, tn), lambda j, i: tuple(lead) + (0, j + col_off))


def mm(a, b, *, tn, out_dtype, lead=(), col_off=0, n_cols=None, max_tile_mb=9):
    m, k = a.shape
    n = b.shape[-1] if n_cols is None else n_cols
    tm = _pick_tm(m, k, max_tile_mb)
    assert n % tn == 0 and a.dtype == BF16, (n, tn, a.dtype)
    return _pcall(
        _mm_kernel,
        grid=(n // tn, m // tm),
        in_specs=[pl.BlockSpec((tm, k), lambda j, i: (i, 0)), _b_spec(b, lead, k, tn, col_off)],
        out_specs=pl.BlockSpec((tm, tn), lambda j, i: (i, j)),
        out_shape=jax.ShapeDtypeStruct((m, n), out_dtype),
        scratch_shapes=[pltpu.VMEM((k, tn), BF16)],
        compiler_params=_cparams(("parallel", "arbitrary")),
    )(a, b)


def _mm_parts_kernel(*refs, n_parts):
    a_refs, (b_ref, o_ref, bb_ref) = refs[:n_parts], refs[n_parts:]

    @pl.when(pl.program_id(1) == 0)
    def _():
        bb_ref[...] = b_ref[...].astype(BF16)

    kp = a_refs[0].shape[1]
    acc = jnp.dot(a_refs[0][...], bb_ref[0:kp, :], preferred_element_type=F32)
    for g in range(1, n_parts):
        acc += jnp.dot(a_refs[g][...], bb_ref[g * kp:(g + 1) * kp, :], preferred_element_type=F32)
    o_ref[...] = acc


def mm_parts(parts, b, *, tn, lead):
    m, kp = parts[0].shape
    k = kp * len(parts)
    n = b.shape[-1]
    tm = _pick_tm(m, k, 9)
    row = pl.BlockSpec((tm, kp), lambda j, i: (i, 0))
    return _pcall(
        functools.partial(_mm_parts_kernel, n_parts=len(parts)),
        grid=(n // tn, m // tm),
        in_specs=[row] * len(parts) + [_b_spec(b, lead, k, tn, 0)],
        out_specs=pl.BlockSpec((tm, tn), lambda j, i: (i, j)),
        out_shape=jax.ShapeDtypeStruct((m, n), F32),
        scratch_shapes=[pltpu.VMEM((k, tn), BF16)],
        compiler_params=_cparams(("parallel", "arbitrary")),
    )(*parts, b)


def _swiglu_up_kernel(a_ref, wg_ref, wu_ref, o_ref, bg_ref, bu_ref):
    @pl.when(pl.program_id(1) == 0)
    def _():
        bg_ref[...] = wg_ref[...].astype(BF16)
        bu_ref[...] = wu_ref[...].astype(BF16)

    a = a_ref[...]
    g = jnp.dot(a, bg_ref[...], preferred_element_type=F32)
    u = jnp.dot(a, bu_ref[...], preferred_element_type=F32)
    o_ref[...] = (g * jax.nn.sigmoid(g) * u).astype(o_ref.dtype)


def swiglu_up(a, wg, wu, *, tn, lead):
    m, k = a.shape
    n = wg.shape[-1]
    tm = _pick_tm(m, k, 4)
    assert n % tn == 0
    return _pcall(
        _swiglu_up_kernel,
        grid=(n // tn, m // tm),
        in_specs=[pl.BlockSpec((tm, k), lambda j, i: (i, 0)),
                  _b_spec(wg, lead, k, tn, 0), _b_spec(wu, lead, k, tn, 0)],
        out_specs=pl.BlockSpec((tm, tn), lambda j, i: (i, j)),
        out_shape=jax.ShapeDtypeStruct((m, n), BF16),
        scratch_shapes=[pltpu.VMEM((k, tn), BF16)] * 2,
        compiler_params=_cparams(("parallel", "arbitrary")),
    )(a, wg, wu)


def _seg_map(n_t, n_c):
    return lambda i: (2 * (i // n_t) + jnp.where(i % n_t >= n_c, 1, 0), 0, 0)


def _rms(x):
    return x * lax.rsqrt(jnp.mean(x * x, axis=-1, keepdims=True) + NORM_EPS)


def _norm_mod_kernel(x_ref, g_ref, sh_ref, sc_ref, *o_refs):
    y = _rms(x_ref[...]) * g_ref[...]
    y = y * (1.0 + sc_ref[...]) + sh_ref[...]
    for o_ref in o_refs:
        o_ref[...] = y.astype(o_ref.dtype)


def norm_mod(x, g, shift, scale, n_t, n_c, out_dtypes):
    r, d = x.shape
    seg = _seg_map(n_t, n_c)
    row = pl.BlockSpec((ROW_TILE, d), lambda i: (i, 0))
    outs = _pcall(
        _norm_mod_kernel,
        grid=(r // ROW_TILE,),
        in_specs=[row, pl.BlockSpec((1, d), lambda i: (0, 0)),
                  pl.BlockSpec((None, 1, d), seg), pl.BlockSpec((None, 1, d), seg)],
        out_specs=[row] * len(out_dtypes),
        out_shape=[jax.ShapeDtypeStruct((r, d), dt) for dt in out_dtypes],
        compiler_params=_cparams(("parallel",)),
    )(x, g.reshape(1, d), shift, scale)
    return outs


def _gate_res_kernel(x_ref, y_ref, g_ref, gt_ref, *rest):
    x = x_ref[...] + gt_ref[...] * (_rms(y_ref[...]) * g_ref[...])
    if len(rest) == 1:
        rest[0][...] = x
    else:
        g2_ref, sh_ref, sc_ref, o_ref, h_ref = rest
        o_ref[...] = x
        h_ref[...] = (_rms(x) * g2_ref[...] * (1.0 + sc_ref[...]) + sh_ref[...]).astype(h_ref.dtype)


def gate_res(x, y, g, gate, n_t, n_c, nxt=None):
    r, d = x.shape
    seg = _seg_map(n_t, n_c)
    row = pl.BlockSpec((ROW_TILE, d), lambda i: (i, 0))
    vec = pl.BlockSpec((1, d), lambda i: (0, 0))
    mod = pl.BlockSpec((None, 1, d), seg)
    in_specs, args = [row, row, vec, mod], [x, y, g.reshape(1, d), gate]
    out_specs, out_shape = row, jax.ShapeDtypeStruct((r, d), F32)
    if nxt is not None:
        g2, shift, scale, dtype = nxt
        in_specs, args = in_specs + [vec, mod, mod], args + [g2.reshape(1, d), shift, scale]
        out_specs, out_shape = [row, row], [out_shape, jax.ShapeDtypeStruct((r, d), dtype)]
    return _pcall(
        _gate_res_kernel,
        grid=(r // ROW_TILE,),
        in_specs=in_specs,
        out_specs=out_specs,
        out_shape=out_shape,
        compiler_params=_cparams(("parallel",)),
    )(*args)


def _router_kernel(f_ref, rt_ref, idx_ref, w_ref):
    lt = lax.dot_general(rt_ref[...], f_ref[...], (((1,), (1,)), ((), ())),
                         precision=lax.Precision.HIGHEST, preferred_element_type=F32)
    e_idx = lax.broadcasted_iota(jnp.int32, lt.shape, 0)
    m1 = jnp.max(lt, axis=0, keepdims=True)
    i1 = jnp.min(jnp.where(lt == m1, e_idx, N_EXPERTS), axis=0, keepdims=True)
    rest = jnp.where(e_idx == i1, -jnp.inf, lt)
    m2 = jnp.max(rest, axis=0, keepdims=True)
    i2 = jnp.min(jnp.where(rest == m2, e_idx, N_EXPERTS), axis=0, keepdims=True)
    e2 = jnp.exp(m2 - m1)
    idx_ref[...] = jnp.concatenate([i1, i2], axis=0)
    w_ref[...] = jnp.concatenate([1.0 / (1.0 + e2), e2 / (1.0 + e2)], axis=0)


def router(f, router_t):
    r, d = f.shape
    return _pcall(
        _router_kernel,
        grid=(r // ROW_TILE,),
        in_specs=[pl.BlockSpec((ROW_TILE, d), lambda i: (i, 0)),
                  pl.BlockSpec((N_EXPERTS, d), lambda i: (0, 0))],
        out_specs=[pl.BlockSpec((2, ROW_TILE), lambda i: (0, i))] * 2,
        out_shape=[jax.ShapeDtypeStruct((2, r), jnp.int32), jax.ShapeDtypeStruct((2, r), F32)],
        compiler_params=_cparams(("parallel",)),
    )(f, router_t)


def _row_copy(src_hbm, dst, src_row, dst_row, sem):
    return pltpu.make_async_copy(src_hbm.at[pl.ds(src_row, 1), :], dst.at[pl.ds(dst_row, 1), :], sem)


def _start_rows(src_hbm, dst, idx_ref, base, n_rows, sem):
    def start(r, carry):
        _row_copy(src_hbm, dst, idx_ref[base + r], r, sem).start()
        return carry

    lax.fori_loop(0, n_rows, start, 0, unroll=8)


def _wait_rows(src_hbm, dst, idx_ref, base, n_rows, sem):
    def wait(r, carry):
        _row_copy(src_hbm, dst, idx_ref[base + r], r, sem).wait()
        return carry

    lax.fori_loop(0, n_rows, wait, 0, unroll=8)


def _gather_rows(src_hbm, dst, idx_ref, base, n_rows, sem):
    _start_rows(src_hbm, dst, idx_ref, base, n_rows, sem)
    _wait_rows(src_hbm, dst, idx_ref, base, n_rows, sem)


def route_tokens(idx, w, n_pad_rows):
    r = idx.shape[1]
    n_assign = 2 * r
    n_tiles = n_pad_rows // MOE_TM
    e_flat = idx.reshape(n_assign)
    w_flat = w.reshape(n_assign)
    counts = jnp.sum((e_flat[:, None] == jnp.arange(N_EXPERTS)[None, :]).astype(jnp.int32), axis=0)
    padded = (counts + MOE_TM - 1) // MOE_TM * MOE_TM
    group_end = jnp.cumsum(padded)
    group_start = group_end - padded
    sorted_start = jnp.cumsum(counts) - counts
    order = jnp.argsort(e_flat, stable=True).astype(jnp.int32)
    inv = jnp.argsort(order).astype(jnp.int32)
    pos = group_start[e_flat] + (inv - sorted_start[e_flat])
    n_used = group_end[-1] // MOE_TM
    tile_start = jnp.arange(n_tiles, dtype=jnp.int32) * MOE_TM
    tile_start = jnp.minimum(tile_start, group_end[-1] - 1)
    tile_expert = jnp.sum((tile_start[:, None] >= group_end[None, :]).astype(jnp.int32), axis=1)
    rows = jnp.arange(n_pad_rows, dtype=jnp.int32)
    row_e = jnp.repeat(tile_expert, MOE_TM)
    rank = rows - group_start[row_e]
    valid = rank < counts[row_e]
    src = order[jnp.clip(sorted_start[row_e] + rank, 0, n_assign - 1)]
    row_token = jnp.where(valid, src % r, 0)
    row_weight = jnp.where(valid, w_flat[src], 0.0)
    return (row_token, row_weight.reshape(n_pad_rows, 1), tile_expert.astype(jnp.int32),
            n_used.astype(jnp.int32).reshape(1), pos)


def _moe_expert_kernel(te_ref, nu_ref, tok_ref, f_hbm, wg_ref, wu_ref, wd_ref, rw_ref, o_ref, land_ref, xb_ref, sem):
    i, f = pl.program_id(0), pl.program_id(1)
    nf = pl.num_programs(1)
    n_used = nu_ref[0]
    used = i < n_used
    rows_per_step = MOE_TM // MOE_STEPS
    nxt = jnp.minimum(i + 1, n_used - 1)

    def row_slice(tile, step):
        base = tile * MOE_TM + step * rows_per_step
        return f_hbm, land_ref.at[pl.ds(step * rows_per_step, rows_per_step)], tok_ref, base, rows_per_step, sem

    @pl.when(jnp.logical_and(used, f == 0))
    def _():
        @pl.when(i == 0)
        def _():
            _start_rows(f_hbm, land_ref, tok_ref, 0, MOE_TM, sem)

        _wait_rows(f_hbm, land_ref, tok_ref, i * MOE_TM, MOE_TM, sem)
        xb_ref[...] = land_ref[...].astype(BF16)

    @pl.when(f == 0)
    def _():
        o_ref[...] = jnp.zeros_like(o_ref)

    @pl.when(used)
    def _():
        src, dst, idx_ref, base, n, _ = row_slice(nxt, f)
        for r in range(n):
            _row_copy(src, dst, idx_ref[base + r], r, sem).start()
        x = xb_ref[...]
        g = jnp.dot(x, wg_ref[...].astype(BF16), preferred_element_type=F32)
        u = jnp.dot(x, wu_ref[...].astype(BF16), preferred_element_type=F32)
        h = (g * jax.nn.sigmoid(g) * u).astype(BF16)
        o_ref[...] += jnp.dot(h, wd_ref[...].astype(BF16), preferred_element_type=F32)

    @pl.when(jnp.logical_and(used, f == nf - 1))
    def _():
        o_ref[...] = o_ref[...] * rw_ref[...]

    @pl.when(jnp.logical_and(i == n_used - 1, f == nf - 1))
    def _():
        _wait_rows(f_hbm, land_ref, tok_ref, i * MOE_TM, MOE_TM, sem)


def moe_experts(f_in, wg, wu, wd, layer, row_token, row_weight, tile_expert, n_used):
    r, d = f_in.shape
    ff = wg.shape[-1]
    n_pad_rows = row_token.shape[0]
    nt, nf = n_pad_rows // MOE_TM, ff // MOE_TF
    assert nf == MOE_STEPS and MOE_TM % MOE_STEPS == 0

    def fi(i, f, nu):
        return jnp.where(i < nu[0], f, nf - 1)

    grid_spec = pltpu.PrefetchScalarGridSpec(
        num_scalar_prefetch=3,
        grid=(nt, nf),
        in_specs=[pl.BlockSpec(memory_space=pl.ANY),
                  pl.BlockSpec((None, None, d, MOE_TF), lambda i, f, te, nu, tok: (layer, te[i], 0, fi(i, f, nu))),
                  pl.BlockSpec((None, None, d, MOE_TF), lambda i, f, te, nu, tok: (layer, te[i], 0, fi(i, f, nu))),
                  pl.BlockSpec((None, None, MOE_TF, d), lambda i, f, te, nu, tok: (layer, te[i], fi(i, f, nu), 0)),
                  pl.BlockSpec((MOE_TM, 1), lambda i, f, te, nu, tok: (i, 0))],
        out_specs=pl.BlockSpec((MOE_TM, d), lambda i, f, te, nu, tok: (i, 0)),
        scratch_shapes=[pltpu.VMEM((MOE_TM, d), F32), pltpu.VMEM((MOE_TM, d), BF16), pltpu.SemaphoreType.DMA(())],
    )
    return _pcall(
        _moe_expert_kernel,
        grid_spec=grid_spec,
        out_shape=jax.ShapeDtypeStruct((n_pad_rows, d), F32),
        compiler_params=_cparams(("arbitrary", "arbitrary")),
    )(tile_expert, n_used, row_token, f_in, wg, wu, wd, row_weight)


def _moe_combine_kernel(pos_ref, y_hbm, x_ref, g_ref, gt_ref, *rest):
    buf_ref, sem = rest[-2:]
    i = pl.program_id(0)

    def rows(fn, tile):
        for k in range(2):
            fn(y_hbm, buf_ref.at[tile % 2, k], pos_ref, (2 * tile + k) * ROW_TILE, ROW_TILE, sem)

    @pl.when(i == 0)
    def _():
        rows(_start_rows, i)

    rows(_wait_rows, i)

    @pl.when(i + 1 < pl.num_programs(0))
    def _():
        rows(_start_rows, i + 1)

    y = buf_ref[i % 2, 0] + buf_ref[i % 2, 1]
    x = x_ref[...] + gt_ref[...] * (_rms(y) * g_ref[...])
    if len(rest) == 3:
        rest[0][...] = x
    else:
        g2_ref, sh_ref, sc_ref, o_ref, h_ref = rest[:5]
        o_ref[...] = x
        h_ref[...] = (_rms(x) * g2_ref[...] * (1.0 + sc_ref[...]) + sh_ref[...]).astype(h_ref.dtype)


def moe_combine_gate_res(x, y_sorted, pos, g, gate, n_t, n_c, nxt=None):
    r, d = x.shape
    nrt = r // ROW_TILE
    pos_tiles = pos.reshape(2, nrt, ROW_TILE).transpose(1, 0, 2).reshape(2 * r)
    seg = _seg_map(n_t, n_c)
    row = pl.BlockSpec((ROW_TILE, d), lambda i, p: (i, 0))
    vec = pl.BlockSpec((1, d), lambda i, p: (0, 0))
    mod = pl.BlockSpec((None, 1, d), lambda i, p: seg(i))
    in_specs, args = [pl.BlockSpec(memory_space=pl.ANY), row, vec, mod], [y_sorted, x, g.reshape(1, d), gate]
    out_specs, out_shape = row, jax.ShapeDtypeStruct((r, d), F32)
    if nxt is not None:
        g2, shift, scale, dtype = nxt
        in_specs, args = in_specs + [vec, mod, mod], args + [g2.reshape(1, d), shift, scale]
        out_specs, out_shape = [row, row], [out_shape, jax.ShapeDtypeStruct((r, d), dtype)]
    grid_spec = pltpu.PrefetchScalarGridSpec(
        num_scalar_prefetch=1,
        grid=(nrt,),
        in_specs=in_specs,
        out_specs=out_specs,
        scratch_shapes=[pltpu.VMEM((2, 2, ROW_TILE, d), F32), pltpu.SemaphoreType.DMA(())],
    )
    return _pcall(
        _moe_combine_kernel,
        grid_spec=grid_spec,
        out_shape=out_shape,
        compiler_params=_cparams(("arbitrary",)),
    )(pos_tiles, *args)


def _rope(x, c, sa, sb):
    return x * c + pltpu.roll(x, 112, 1) * sa + pltpu.roll(x, 16, 1) * sb


def _head_rms(x, g, bd):
    sq = x * x
    hi = sq.astype(BF16)
    lo = (sq - hi.astype(F32)).astype(BF16)
    ms = jnp.dot(hi, bd, preferred_element_type=F32) + jnp.dot(lo, bd, preferred_element_type=F32)
    return x * lax.rsqrt(ms + NORM_EPS) * g


def _prep_kernel(p_ref, c_ref, sa_ref, sb_ref, gq_ref, gk_ref, bd_ref,
                 aq_ref, ak_ref, av_ref, dq_ref, dk_ref, dv_ref, rq_ref, rk_ref, rv_ref, rg_ref):
    c, sa, sb = c_ref[...], sa_ref[...], sb_ref[...]
    bd = bd_ref[...]
    scale = HEAD_DIM ** -0.5
    qscale = scale * math.log2(math.e)

    def chunk(j):
        return p_ref[:, 128 * j:128 * (j + 1)]

    def put_heads(ref, first, val):
        ref[first] = val[:, :HEAD_DIM].astype(ref.dtype)
        ref[first + 1] = val[:, HEAD_DIM:].astype(ref.dtype)

    for j in range(4):
        put_heads(aq_ref, 2 * j, _rope(_head_rms(chunk(j), gq_ref[...], bd), c, sa, sb) * qscale)
    put_heads(ak_ref, 0, _rope(_head_rms(chunk(4), gk_ref[...], bd), c, sa, sb))
    one64 = (lax.broadcasted_iota(jnp.int32, (ROW_TILE, HEAD_DIM), 1) == 0).astype(av_ref.dtype)
    one128 = (lax.broadcasted_iota(jnp.int32, (ROW_TILE, 2 * HEAD_DIM), 1) == 0).astype(dv_ref.dtype)
    v_gqa = chunk(5)
    for h in range(GQA_KV_HEADS):
        av_ref[h, :, :HEAD_DIM] = v_gqa[:, HEAD_DIM * h:HEAD_DIM * (h + 1)].astype(av_ref.dtype)
        av_ref[h, :, HEAD_DIM:] = one64
    for j in range(4):
        put_heads(dq_ref, 2 * j, _rope(chunk(6 + j), c, sa, sb) * qscale)
        put_heads(dk_ref, 2 * j, _rope(chunk(10 + j), c, sa, sb))
        dv_ref[j, :, :2 * HEAD_DIM] = chunk(14 + j).astype(dv_ref.dtype)
        dv_ref[j, :, 2 * HEAD_DIM:] = one128
    for j in range(2):
        put_heads(rq_ref, 2 * j, _rope(chunk(18 + j), c, sa, sb))
        put_heads(rk_ref, 2 * j, _rope(chunk(20 + j), c, sa, sb) * scale)
    for j in range(4):
        rv_ref[j] = chunk(22 + j).astype(rv_ref.dtype)
    rg_ref[...] = p_ref[:, COL_RET_GATE:COL_ATTN]


def prep_heads(p_attn, rope_c, rope_sa, rope_sb, qk_g, bsz, n_t):
    t = n_t * ROW_TILE
    bd = jnp.asarray(np.kron(np.eye(2), np.full((HEAD_DIM, HEAD_DIM), 1.0 / HEAD_DIM)), F32).astype(BF16)
    gq = jnp.tile(qk_g[0], 2).reshape(1, 128)
    gk = jnp.tile(qk_g[1], 2).reshape(1, 128)
    tab = pl.BlockSpec((ROW_TILE, 128), lambda i: (i % n_t, 0))
    vec = pl.BlockSpec((1, 128), lambda i: (0, 0))

    def heads(nh, dh):
        return (pl.BlockSpec((None, nh, ROW_TILE, dh), lambda i: (i // n_t, 0, i % n_t, 0)),
                jax.ShapeDtypeStruct((bsz, nh, t, dh), BF16))

    outs = [heads(8, 64), heads(2, 64), heads(2, 128), heads(8, 64), heads(8, 64), heads(4, 256),
            heads(4, 64), heads(4, 64), heads(4, 128)]
    return _pcall(
        _prep_kernel,
        grid=(bsz * n_t,),
        in_specs=[pl.BlockSpec((ROW_TILE, COL_ATTN), lambda i: (i, 0)), tab, tab, tab, vec, vec,
                  pl.BlockSpec((128, 128), lambda i: (0, 0))],
        out_specs=[o[0] for o in outs] + [pl.BlockSpec((ROW_TILE, GROUP_WIDTH), lambda i: (i, 0))],
        out_shape=[o[1] for o in outs] + [jax.ShapeDtypeStruct((bsz * t, GROUP_WIDTH), F32)],
        compiler_params=_cparams(("parallel",)),
    )(p_attn, rope_c, rope_sa, rope_sb, gq, gk, bd)


def _nt_dot(a, b):
    return lax.dot_general(a, b, (((1,), (1,)), ((), ())), preferred_element_type=F32)


def _softmax_chunks(score_fn, v_ref, s_ref, rows, dv, n_chunks, tk, n_ctx_keys, is_ctx):
    def update(s, start, carry):
        m, acc = carry
        m_new = jnp.maximum(m, jnp.max(s, axis=-1, keepdims=True))
        p = jnp.exp2(s - m_new)
        acc = jnp.exp2(m - m_new) * acc + jnp.dot(p.astype(BF16), v_ref[pl.ds(start, tk), :],
                                                  preferred_element_type=F32)
        return m_new, acc

    init = (jnp.full((rows, 1), -1e30, F32), jnp.zeros((rows, v_ref.shape[-1]), F32))

    def ctx_tile():
        s = score_fn(0, tk)
        col = lax.broadcasted_iota(jnp.int32, s.shape, 1)
        return update(jnp.where(col < n_ctx_keys, s, -1e30), 0, init)

    def lat_tile():
        stats = init
        s_ref[0] = score_fn(0, tk)
        for j in range(n_chunks):
            if j + 1 < n_chunks:
                s_ref[(j + 1) % 2] = score_fn((j + 1) * tk, tk)
            stats = update(s_ref[j % 2], j * tk, stats)
        return stats

    _, acc = lax.cond(is_ctx, ctx_tile, lat_tile)
    return acc[:, :dv] / acc[:, dv:dv + 1]


def _gqa_kernel(q_ref, k_ref, v_ref, beta_ref, o_ref, s_ref, *, tq, tk, n_chunks, n_ctx_tiles, n_ctx_keys):
    is_ctx = pl.program_id(2) < n_ctx_tiles
    q = q_ref[...].reshape(GQA_REP * tq, HEAD_DIM)

    def scores(start, size):
        return _nt_dot(q, k_ref[pl.ds(start, size), :])

    o = _softmax_chunks(scores, v_ref, s_ref, GQA_REP * tq, HEAD_DIM, n_chunks, tk, n_ctx_keys, is_ctx)
    for r in range(GQA_REP):
        sl = slice(HEAD_DIM * r, HEAD_DIM * (r + 1))
        o_ref[:, sl] = (o[r * tq:(r + 1) * tq] * beta_ref[:, sl]).astype(o_ref.dtype)


def gqa_attention(q, k, v, beta, *, n_t, n_c, tk):
    bsz, _, t, _ = q.shape
    tq = ROW_TILE
    kern = functools.partial(_gqa_kernel, tq=tq, tk=tk, n_chunks=t // tk, n_ctx_tiles=n_c,
                             n_ctx_keys=n_c * ROW_TILE)
    kspec = pl.BlockSpec((None, None, t, HEAD_DIM), lambda b, g, i: (b, g, 0, 0))
    vspec = pl.BlockSpec((None, None, t, v.shape[-1]), lambda b, g, i: (b, g, 0, 0))
    w = GQA_REP * HEAD_DIM
    return _pcall(
        kern,
        grid=(bsz, GQA_KV_HEADS, n_t),
        in_specs=[pl.BlockSpec((None, GQA_REP, tq, HEAD_DIM), lambda b, g, i: (b, g, i, 0)), kspec, vspec,
                  pl.BlockSpec((1, w), lambda b, g, i: (0, g))],
        out_specs=pl.BlockSpec((tq, w), lambda b, g, i: (b * n_t + i, g)),
        out_shape=jax.ShapeDtypeStruct((bsz * t, GROUP_WIDTH), BF16),
        scratch_shapes=[pltpu.VMEM((2, GQA_REP * tq, tk), F32)],
        compiler_params=_cparams(("parallel", "parallel", "arbitrary")),
    )(q, k, v, beta)


def _diff_kernel(q_ref, k_ref, v_ref, lam_ref, g_ref, beta_ref, o_ref, s_ref, *, tq, tk, n_chunks, n_ctx_tiles,
                 n_ctx_keys, out_scale):
    is_ctx = pl.program_id(2) < n_ctx_tiles
    q0, q1 = q_ref[0], q_ref[1]

    def scores(start, size):
        return jnp.concatenate([_nt_dot(q0, k_ref[0, pl.ds(start, size), :]),
                                _nt_dot(q1, k_ref[1, pl.ds(start, size), :])], axis=0)

    a = _softmax_chunks(scores, v_ref, s_ref, 2 * tq, 2 * HEAD_DIM, n_chunks, tk, n_ctx_keys, is_ctx)
    o = a[:tq] - lam_ref[...] * a[tq:]
    o = _rms(o) * g_ref[...] * out_scale
    o_ref[...] = (o * beta_ref[...]).astype(o_ref.dtype)


def diff_attention(q, k, v, lam, subln_g, beta, out_scale, *, n_t, n_c, tk):
    bsz, _, t, _ = q.shape
    tq = ROW_TILE
    dv = 2 * HEAD_DIM
    kern = functools.partial(_diff_kernel, tq=tq, tk=tk, n_chunks=t // tk, n_ctx_tiles=n_c,
                             n_ctx_keys=n_c * ROW_TILE, out_scale=out_scale)
    return _pcall(
        kern,
        grid=(bsz, DIFF_HEADS, n_t),
        in_specs=[pl.BlockSpec((None, 2, tq, HEAD_DIM), lambda b, h, i: (b, h, i, 0)),
                  pl.BlockSpec((None, 2, t, HEAD_DIM), lambda b, h, i: (b, h, 0, 0)),
                  pl.BlockSpec((None, None, t, v.shape[-1]), lambda b, h, i: (b, h, 0, 0)),
                  pl.BlockSpec((1, 1), lambda b, h, i: (0, 0)),
                  pl.BlockSpec((1, dv), lambda b, h, i: (0, 0)),
                  pl.BlockSpec((1, dv), lambda b, h, i: (0, h))],
        out_specs=pl.BlockSpec((tq, dv), lambda b, h, i: (b * n_t + i, h)),
        out_shape=jax.ShapeDtypeStruct((bsz * t, GROUP_WIDTH), BF16),
        scratch_shapes=[pltpu.VMEM((2, 2 * tq, tk), F32)],
        compiler_params=_cparams(("parallel", "parallel", "arbitrary")),
    )(q, k, v, lam, subln_g, beta)


def _ret_kernel(lg_ref, q_ref, k_ref, v_ref, o_ref, state_ref, decay_ref, xi_ref, zeta_ref):
    d = pl.program_id(1)
    c = ROW_TILE
    dv = v_ref.shape[-1]

    @pl.when(pl.program_id(2) == 0)
    def _():
        state_ref[...] = jnp.zeros_like(state_ref)
        fwd = d == 0
        ii = lax.broadcasted_iota(jnp.int32, (c, c), 0)
        jj = lax.broadcasted_iota(jnp.int32, (c, c), 1)
        rel = jnp.where(fwd, ii - jj, jj - ii).astype(F32)
        pos = lax.broadcasted_iota(jnp.int32, (c, 1), 0).astype(F32)
        for h in range(RET_HEADS):
            lg = lg_ref[h]
            decay_ref[h] = jnp.where(rel >= 0, jnp.exp(jnp.maximum(rel, 0.0) * lg), 0.0)
            xi_ref[h] = jnp.exp(jnp.where(fwd, pos + 1.0, c - pos) * lg)
            zeta_ref[h] = jnp.exp(jnp.where(fwd, c - 1.0 - pos, pos) * lg)

    for h in range(RET_HEADS):
        q, k, v = q_ref[h], k_ref[h], v_ref[h]
        state = state_ref[h]
        scores = _nt_dot(q, k) * decay_ref[h]
        inner = jnp.dot(scores.astype(BF16), v, preferred_element_type=F32)
        cross = jnp.dot(q, state.astype(BF16), preferred_element_type=F32) * xi_ref[h]
        o_ref[:, dv * h:dv * (h + 1)] = inner + cross
        kz = (k.astype(F32) * zeta_ref[h]).astype(BF16)
        upd = lax.dot_general(kz, v, (((0,), (0,)), ((), ())), preferred_element_type=F32)
        state_ref[h] = jnp.exp(c * lg_ref[h]) * state + upd


def retention(q, k, v, log_decay, *, n_t, n_c):
    bsz, nh, t, dk = q.shape
    dv = v.shape[-1]

    def blk(d, j):
        back = jnp.where(j < n_c, n_c - 1 - j, n_t - 1 - (j - n_c))
        return jnp.where(d == 0, j, back)

    qk = pl.BlockSpec((None, nh, ROW_TILE, dk), lambda b, d, j: (b, 0, blk(d, j), 0))
    return _pcall(
        _ret_kernel,
        grid=(bsz, 2, n_t),
        in_specs=[pl.BlockSpec((None, nh, 1, 1), lambda b, d, j: (d, 0, 0, 0)), qk, qk,
                  pl.BlockSpec((None, nh, ROW_TILE, dv), lambda b, d, j: (b, 0, blk(d, j), 0))],
        out_specs=pl.BlockSpec((None, ROW_TILE, nh * dv), lambda b, d, j: (d, b * n_t + blk(d, j), 0)),
        out_shape=jax.ShapeDtypeStruct((2, bsz * t, nh * dv), F32),
        scratch_shapes=[pltpu.VMEM((nh, dk, dv), F32), pltpu.VMEM((nh, ROW_TILE, ROW_TILE), F32),
                        pltpu.VMEM((nh, ROW_TILE, 1), F32), pltpu.VMEM((nh, ROW_TILE, 1), F32)],
        compiler_params=_cparams(("parallel", "arbitrary", "arbitrary")),
    )(log_decay.reshape(2, RET_HEADS, 1, 1), q, k, v)


def _ret_gate_kernel(of_ref, ob_ref, g_ref, gn_ref, beta_ref, o_ref):
    dv = 2 * HEAD_DIM
    for h in range(RET_HEADS):
        sl = slice(dv * h, dv * (h + 1))
        o = of_ref[:, sl] + ob_ref[:, sl]
        mu = jnp.mean(o, axis=-1, keepdims=True)
        var = jnp.mean(jnp.square(o - mu), axis=-1, keepdims=True)
        y = (o - mu) * lax.rsqrt(var + NORM_EPS) * gn_ref[:, sl]
        g = g_ref[:, sl]
        o_ref[:, sl] = (g * jax.nn.sigmoid(g) * y * beta_ref[:, sl]).astype(o_ref.dtype)


def ret_gate(o2, gate, gn_g, beta):
    _, r, w = o2.shape
    vec = pl.BlockSpec((1, w), lambda i: (0, 0))
    row = pl.BlockSpec((ROW_TILE, w), lambda i: (i, 0))
    return _pcall(
        _ret_gate_kernel,
        grid=(r // ROW_TILE,),
        in_specs=[pl.BlockSpec((None, ROW_TILE, w), lambda i: (0, i, 0)),
                  pl.BlockSpec((None, ROW_TILE, w), lambda i: (1, i, 0)), row, vec, vec],
        out_specs=row,
        out_shape=jax.ShapeDtypeStruct((r, w), BF16),
        compiler_params=_cparams(("parallel",)),
    )(o2, o2, gate, gn_g, beta)


def _short_conv_kernel(cur_ref, prev_ref, next_ref, w_ref, b_ref, z_ref, x1_ref, x2_ref, *, n_seq_tiles):
    j = pl.program_id(1)

    @pl.when(j < n_seq_tiles)
    def _():
        u = cur_ref[...]
        rows = lax.broadcasted_iota(jnp.int32, u.shape, 0)
        prev_row = jnp.where(j == 0, 0.0, prev_ref[7:8, :])
        next_row = jnp.where(j == n_seq_tiles - 1, 0.0, next_ref[0:1, :])
        up = jnp.where(rows == 0, prev_row, pltpu.roll(u, 1, 0))
        un = jnp.where(rows == ROW_TILE - 1, next_row, pltpu.roll(u, ROW_TILE - 1, 0))
        y = up * w_ref[0:1, :] + u * w_ref[1:2, :] + un * w_ref[2:3, :] + b_ref[...]
        z_ref[...] = y[:, :HY_WIDTH]
        x1_ref[...] = y[:, HY_WIDTH:2 * HY_WIDTH]
        x2_ref[...] = y[:, 2 * HY_WIDTH:]

    @pl.when(j >= n_seq_tiles)
    def _():
        z_ref[...] = jnp.zeros_like(z_ref)
        x1_ref[...] = jnp.zeros_like(x1_ref)
        x2_ref[...] = jnp.zeros_like(x2_ref)


def short_conv(p_hy, w, b, *, bsz, n_t, first_tile, n_seq_tiles, n_pad_tiles):
    w3 = 3 * HY_WIDTH
    sub = ROW_TILE // 8
    last_blk8 = p_hy.shape[0] // 8 - 1

    def cur(bb, j):
        return (bb * n_t + first_tile + jnp.minimum(j, n_seq_tiles - 1), 0)

    def prev(bb, j):
        return (jnp.maximum(cur(bb, j)[0] * sub - 1, 0), 0)

    def nxt(bb, j):
        return (jnp.minimum((cur(bb, j)[0] + 1) * sub, last_blk8), 0)

    kern = functools.partial(_short_conv_kernel, n_seq_tiles=n_seq_tiles)
    rows = bsz * n_pad_tiles * ROW_TILE
    return _pcall(
        kern,
        grid=(bsz, n_pad_tiles),
        in_specs=[pl.BlockSpec((ROW_TILE, w3), cur), pl.BlockSpec((8, w3), prev), pl.BlockSpec((8, w3), nxt),
                  pl.BlockSpec((3, w3), lambda bb, j: (0, 0)), pl.BlockSpec((1, w3), lambda bb, j: (0, 0))],
        out_specs=[pl.BlockSpec((ROW_TILE, HY_WIDTH), lambda bb, j: (bb * n_pad_tiles + j, 0))] * 3,
        out_shape=[jax.ShapeDtypeStruct((rows, HY_WIDTH), F32)] * 3,
        compiler_params=_cparams(("parallel", "arbitrary")),
    )(p_hy, p_hy, p_hy, w, b.reshape(1, w3))


def _filter_kernel(feat_ref, w1_ref, b1_ref, w2_ref, b2_ref, w3_ref, win_ref, h_ref, asum_ref):
    i = pl.program_id(0)
    hp = lax.Precision.HIGHEST
    h = jnp.sin(jnp.dot(feat_ref[...], w1_ref[...], precision=hp, preferred_element_type=F32) + b1_ref[...])
    h = jnp.sin(jnp.dot(h, w2_ref[...], precision=hp, preferred_element_type=F32) + b2_ref[...])
    h = jnp.dot(h, w3_ref[...], precision=hp, preferred_element_type=F32)
    win = win_ref[...]
    h = h * jnp.concatenate([win] * (2 * HY_ORDER), axis=-1)
    h_ref[...] = h
    rows = lax.broadcasted_iota(jnp.int32, h.shape, 0) + i * ROW_TILE
    cols = lax.broadcasted_iota(jnp.int32, h.shape, 1)
    is_bwd = (cols // HY_WIDTH) % 2 == 1
    part = jnp.sum(jnp.where(is_bwd & (rows == 0), 0.0, jnp.abs(h)), axis=0, keepdims=True)

    @pl.when(i == 0)
    def _():
        asum_ref[...] = part

    @pl.when(i > 0)
    def _():
        asum_ref[...] += part


def hyena_filter_taps(length, w1, b1, w2, b2, w3):
    t = jnp.arange(length, dtype=F32)
    t_norm = t / length
    f = jnp.linspace(1e-4, HY_BANDS - 1, HY_BANDS, dtype=F32)
    wt = 2.0 * math.pi * t_norm
    feats = jnp.concatenate([t_norm[:, None], jnp.cos(wt[:, None] * f), -jnp.sin(wt[:, None] * f)], axis=-1)
    feats = jnp.pad(feats, ((0, 0), (0, 128 - HY_EMB_DIM)))
    w1p = jnp.pad(w1, ((0, 128 - HY_EMB_DIM), (0, 0)))
    deltas = jnp.abs(jnp.linspace(HY_MIN_DECAY, HY_MAX_DECAY, HY_WIDTH, dtype=F32))
    window = jnp.exp(-t_norm[:, None] * deltas[None])
    wout = HY_ORDER * 2 * HY_WIDTH
    full = lambda shp: pl.BlockSpec(shp, lambda i: (0, 0))
    return _pcall(
        _filter_kernel,
        grid=(length // ROW_TILE,),
        in_specs=[pl.BlockSpec((ROW_TILE, 128), lambda i: (i, 0)), full((128, HY_FILT_HIDDEN)),
                  full((1, HY_FILT_HIDDEN)), full((HY_FILT_HIDDEN, HY_FILT_HIDDEN)), full((1, HY_FILT_HIDDEN)),
                  full((HY_FILT_HIDDEN, wout)), pl.BlockSpec((ROW_TILE, HY_WIDTH), lambda i: (i, 0))],
        out_specs=[pl.BlockSpec((ROW_TILE, wout), lambda i: (i, 0)), full((1, wout))],
        out_shape=[jax.ShapeDtypeStruct((length, wout), F32), jax.ShapeDtypeStruct((1, wout), F32)],
        compiler_params=_cparams(("arbitrary",)),
    )(feats, w1p, b1.reshape(1, -1), w2, b2.reshape(1, -1), w3, window)


class _Dft:
    def __init__(self, n1):
        n2 = FFT_N2
        assert n1 % 16 == 0
        self.n1, self.n = n1, n1 * n2
        self.half = n1 // 2
        self.nf = n1 // 2 + 1
        self.nfp = -(-self.nf // 16) * 16
        f1 = np.arange(self.nf)[:, None]
        ang = 2.0 * np.pi * f1 * np.arange(n1)[None, :] / n1
        s1 = np.zeros((2 * self.nfp, n1))
        s1[:self.nf] = np.cos(ang)
        s1[self.nfp:self.nfp + self.nf] = -np.sin(ang)
        self.s1_full = s1
        tw = 2.0 * np.pi * f1 * np.arange(n2)[None, :] / self.n
        self.tw_cos = np.cos(tw)[:, :, None]
        self.tw_sin = np.sin(tw)[:, :, None]
        a2 = 2.0 * np.pi * np.arange(n2)[:, None] * np.arange(n2)[None, :] / n2
        wc, ws = np.cos(a2), np.sin(a2)
        self.m_fwd = np.block([[wc, ws], [-ws, wc]])
        self.m_inv = np.block([[wc, -ws], [ws, wc]])
        wgt = np.full(self.nf, 2.0)
        wgt[0] = 1.0
        wgt[-1] = 1.0
        ango = 2.0 * np.pi * np.arange(self.half)[:, None] * np.arange(self.nf)[None, :] / n1
        self.s4_re = np.zeros((self.half, self.nfp))
        self.s4_im = np.zeros((self.half, self.nfp))
        self.s4_re[:, :self.nf] = np.cos(ango) * wgt / self.n
        self.s4_im[:, :self.nf] = -np.sin(ango) * wgt / self.n

    @staticmethod
    def const(a, dtype=BF16):
        return jnp.asarray(a, F32).astype(dtype)


def _fft_s1_kernel(m_ref, x_ref, o_ref):
    k_rows, n_sub, w = x_ref.shape
    x = x_ref[...].reshape(k_rows * n_sub, w)
    y = jnp.dot(m_ref[...], x.astype(BF16), preferred_element_type=F32)
    o_ref[...] = y.reshape(o_ref.shape)


def fft_stage1(mat, x3, *, n_batch, k_rows):
    w = x3.shape[-1]
    m = mat.shape[0]
    big = _Dft.const(np.kron(mat, np.eye(FFT_SUB)))
    return _pcall(
        _fft_s1_kernel,
        grid=(n_batch, FFT_N2 // FFT_SUB),
        in_specs=[pl.BlockSpec((m * FFT_SUB, k_rows * FFT_SUB), lambda b, j: (0, 0)),
                  pl.BlockSpec((k_rows, FFT_SUB, w), lambda b, j: (b, j, 0))],
        out_specs=pl.BlockSpec((None, m, FFT_SUB, w), lambda b, j: (b, 0, j, 0)),
        out_shape=jax.ShapeDtypeStruct((n_batch, m, FFT_N2, w), F32),
        compiler_params=_cparams(("parallel", "parallel")),
    )(big, x3)


def _twiddle(ar, ai, c, s):
    return ar * c + ai * s, ai * c - ar * s


def _fft_filter_mid_kernel(ar_ref, ai_ref, c_ref, s_ref, mf_ref, sc_ref, hb0_ref, kr_ref, ki_ref):
    br, bi = _twiddle(ar_ref[...], ai_ref[...], c_ref[...], s_ref[...])
    x = jnp.dot(mf_ref[...], jnp.concatenate([br, bi], axis=0).astype(BF16), preferred_element_type=F32)
    xr, xi = x[:FFT_N2], x[FFT_N2:]
    w = HY_WIDTH
    for o in range(HY_ORDER):
        fw = slice(2 * o * w, (2 * o + 1) * w)
        bw = slice((2 * o + 1) * w, (2 * o + 2) * w)
        oc = slice(o * w, (o + 1) * w)
        kr_ref[:, oc] = (xr[:, fw] + xr[:, bw] - hb0_ref[:, oc]) * sc_ref[:, oc]
        ki_ref[:, oc] = (xi[:, fw] - xi[:, bw]) * sc_ref[:, oc]


def fft_filter_mid(a, dft, kscale, hb0):
    cols = a.shape[-1]
    n2 = FFT_N2
    tw = pl.BlockSpec((None, n2, 1), lambda f: (f, 0, 0))
    blk = pl.BlockSpec((None, n2, cols // 2), lambda f: (f, 0, 0))
    vec = pl.BlockSpec((1, cols // 2), lambda f: (0, 0))
    return _pcall(
        _fft_filter_mid_kernel,
        grid=(dft.nf,),
        in_specs=[pl.BlockSpec((None, n2, cols), lambda f: (f, 0, 0)),
                  pl.BlockSpec((None, n2, cols), lambda f: (dft.nfp + f, 0, 0)), tw, tw,
                  pl.BlockSpec((2 * n2, 2 * n2), lambda f: (0, 0)), vec, vec],
        out_specs=[blk, blk],
        out_shape=[jax.ShapeDtypeStruct((dft.nf, n2, cols // 2), F32)] * 2,
        compiler_params=_cparams(("parallel",)),
    )(a, a, _Dft.const(dft.tw_cos, F32), _Dft.const(dft.tw_sin, F32), _Dft.const(dft.m_fwd), kscale, hb0)


def _fft_mid_kernel(ar_ref, ai_ref, c_ref, s_ref, mf_ref, mi_ref, kr_ref, ki_ref, er_ref, ei_ref, *, nf):
    for u in range(FFT_MID_F):
        f1 = pl.program_id(1) * FFT_MID_F + u

        @pl.when(f1 < nf)
        def _():
            c, s = c_ref[u], s_ref[u]
            br, bi = _twiddle(ar_ref[u], ai_ref[u], c, s)
            x = jnp.dot(mf_ref[...], jnp.concatenate([br, bi], axis=0).astype(BF16), preferred_element_type=F32)
            xr, xi = x[:FFT_N2], x[FFT_N2:]
            kr, ki = kr_ref[u], ki_ref[u]
            yr = xr * kr - xi * ki
            yi = xr * ki + xi * kr
            dd = jnp.dot(mi_ref[...], jnp.concatenate([yr, yi], axis=0).astype(BF16), preferred_element_type=F32)
            dr, di = dd[:FFT_N2], dd[FFT_N2:]
            er_ref[u] = dr * c - di * s
            ei_ref[u] = di * c + dr * s

        @pl.when(f1 >= nf)
        def _():
            er_ref[u] = jnp.zeros((FFT_N2, er_ref.shape[-1]), F32)
            ei_ref[u] = jnp.zeros((FFT_N2, ei_ref.shape[-1]), F32)


def fft_mid(a, dft, kr, ki, order):
    nb = a.shape[0]
    n2, w, g = FFT_N2, HY_WIDTH, FFT_MID_F
    nf, nfp = dft.nf, dft.nfp
    assert nfp % g == 0
    last = (nf - 1) // g
    fc = lambda f: jnp.minimum(f, last)
    tw = pl.BlockSpec((g, n2, 1), lambda b, f: (fc(f), 0, 0))
    mat = pl.BlockSpec((2 * n2, 2 * n2), lambda b, f: (0, 0))
    kf = pl.BlockSpec((g, n2, w), lambda b, f: (fc(f), 0, order))
    out = pl.BlockSpec((None, g, n2, w), lambda b, f: (b, f, 0, 0))
    return _pcall(
        functools.partial(_fft_mid_kernel, nf=nf),
        grid=(nb, nfp // g),
        in_specs=[pl.BlockSpec((None, g, n2, w), lambda b, f: (b, fc(f), 0, 0)),
                  pl.BlockSpec((None, g, n2, w), lambda b, f: (b, nfp // g + fc(f), 0, 0)),
                  tw, tw, mat, mat, kf, kf],
        out_specs=[out, out],
        out_shape=[jax.ShapeDtypeStruct((nb, nfp, n2, w), F32)] * 2,
        compiler_params=_cparams(("parallel", "arbitrary")),
    )(a, a, _Dft.const(dft.tw_cos, F32), _Dft.const(dft.tw_sin, F32), _Dft.const(dft.m_fwd),
      _Dft.const(dft.m_inv), kr, ki)


def _fft_s4_kernel(mr_ref, mi_ref, er_ref, ei_ref, gate_ref, z_ref, bias_ref, scale_ref, o_ref):
    nfp, n_sub, w = er_ref.shape
    er = er_ref[...].reshape(nfp * n_sub, w).astype(BF16)
    ei = ei_ref[...].reshape(nfp * n_sub, w).astype(BF16)
    y = (jnp.dot(mr_ref[...], er, preferred_element_type=F32)
         + jnp.dot(mi_ref[...], ei, preferred_element_type=F32)).reshape(o_ref.shape)
    o_ref[...] = gate_ref[...] * (y + bias_ref[...] * z_ref[...]) * scale_ref[...]


def fft_stage4(dft, er, ei, gate, z, bias_row, scale_row):
    nb = er.shape[0]
    m, nfp = dft.half, dft.nfp
    w = z.shape[-1]
    eye = np.eye(FFT_SUB)
    row = pl.BlockSpec((m, FFT_SUB, w), lambda b, j: (b, j, 0))
    vec = pl.BlockSpec((1, 1, w), lambda b, j: (0, 0, 0))
    mat = pl.BlockSpec((m * FFT_SUB, nfp * FFT_SUB), lambda b, j: (0, 0))
    spec = pl.BlockSpec((None, nfp, FFT_SUB, w), lambda b, j: (b, 0, j, 0))
    return _pcall(
        _fft_s4_kernel,
        grid=(nb, FFT_N2 // FFT_SUB),
        in_specs=[mat, mat, spec, spec, row, row, vec, vec],
        out_specs=row,
        out_shape=jax.ShapeDtypeStruct((nb * m, FFT_N2, w), F32),
        compiler_params=_cparams(("parallel", "parallel")),
    )(_Dft.const(np.kron(dft.s4_re, eye)), _Dft.const(np.kron(dft.s4_im, eye)), er, ei, gate, z,
      bias_row.reshape(1, 1, w), scale_row.reshape(1, 1, w))


def hyena_long_conv(z, x1, x2, filt, hbias, beta_hy, *, bsz, length, n1):
    dft = _Dft(n1)
    n2, w = FFT_N2, HY_WIDTH
    half = dft.half
    assert half >= 8 and length % n2 == 0
    taps, asum = filt
    asum = asum.reshape(HY_ORDER, 2, w)
    kscale = (1.0 / (asum[:, 0] + asum[:, 1] + NORM_EPS)).reshape(1, HY_ORDER * w)
    hb0 = taps[0].reshape(HY_ORDER, 2, w)[:, 1].reshape(1, HY_ORDER * w)
    k_taps = length // n2
    a_f = fft_stage1(dft.s1_full[:, :k_taps], taps.reshape(k_taps, n2, 2 * HY_ORDER * w), n_batch=1, k_rows=k_taps)
    kr, ki = fft_filter_mid(a_f[0], dft, kscale, hb0)
    s1 = dft.s1_full[:, :half]
    ones = jnp.ones((1, w), F32)
    shape3 = (bsz * half, n2, w)
    zc = z.reshape(shape3)
    for o, gate in enumerate((x1, x2)):
        a = fft_stage1(s1, zc, n_batch=bsz, k_rows=half)
        er, ei = fft_mid(a, dft, kr, ki, o)
        last = o == HY_ORDER - 1
        zc = fft_stage4(dft, er, ei, gate.reshape(shape3), zc, hbias[o].reshape(1, w),
                        beta_hy.reshape(1, w) if last else ones)
    return zc.reshape(bsz * half * n2, w)


def _rope_tables(n_ctx, n_lat):
    rows = n_lat // GRID_W
    row = jnp.repeat(jnp.arange(rows, dtype=F32), GRID_W)
    col = jnp.tile(jnp.arange(GRID_W, dtype=F32), rows)
    n_freq = HEAD_DIM // 4
    freqs = ROPE_THETA ** (-jnp.arange(n_freq, dtype=F32) / n_freq)
    ar = row[:, None] * freqs
    ac = col[:, None] * freqs
    cos = jnp.concatenate([jnp.cos(ar), jnp.cos(ar), jnp.cos(ac), jnp.cos(ac)], axis=-1)
    sin = jnp.concatenate([jnp.sin(ar), jnp.sin(ar), jnp.sin(ac), jnp.sin(ac)], axis=-1)
    cos = jnp.concatenate([jnp.ones((n_ctx, HEAD_DIM), F32), cos], axis=0)
    sin = jnp.concatenate([jnp.zeros((n_ctx, HEAD_DIM), F32), sin], axis=0)
    even = (np.arange(HEAD_DIM) // n_freq) % 2 == 0
    sin_a = jnp.where(even, -sin, 0.0)
    sin_b = jnp.where(even, 0.0, sin)
    tile2 = lambda a: jnp.concatenate([a, a], axis=-1)
    return tile2(cos), tile2(sin_a), tile2(sin_b)


def _attn_key_chunk(t):
    for tk in (1408, 1280, 1024, 768, 512, 256):
        if t % tk == 0:
            return tk
    raise ValueError(t)


def kernel(x, c, ctx, c_ctx, ada_w, ada_b, norm_g, w_in, w_out, mix_beta, gqa_qk_g, diff_lambda, diff_subln_g,
           ret_log_decay, ret_gn_g, hy_short_w, hy_short_b, hy_filt_w1, hy_filt_b1, hy_filt_w2, hy_filt_b2,
           hy_filt_w3, hy_bias, ffn_w_gate, ffn_w_up, ffn_w_down, moe_router, moe_w_gate, moe_w_up, moe_w_down):
    bsz, n_lat, d = x.shape
    n_ctx = ctx.shape[1]
    t = n_ctx + n_lat
    r = bsz * t
    n_t, n_c = t // ROW_TILE, n_ctx // ROW_TILE
    n_s = n_lat // ROW_TILE
    assert n_ctx % ROW_TILE == 0 and n_lat % ROW_TILE == 0 and r % 512 == 0
    tk = _attn_key_chunk(t)
    assert tk >= n_ctx

    rope_c, rope_sa, rope_sb = _rope_tables(n_ctx, n_lat)
    xs = jnp.concatenate([ctx, x], axis=1).reshape(r, d)

    cvec = jnp.stack([jnp.broadcast_to(c_ctx, c.shape), c], axis=1).reshape(2 * bsz, d)
    cvec = jax.nn.silu(cvec)
    cvec = jnp.pad(cvec, ((0, 16 - 2 * bsz), (0, 0))).astype(BF16)

    n1_lat = 2 * n_lat // FFT_N2
    ctx_pad_tiles = max(n_c, 1024 // ROW_TILE)
    n1_ctx = 2 * ctx_pad_tiles * ROW_TILE // FFT_N2

    mods = []
    for l in range(DEPTH):
        mod = mm(cvec, ada_w, tn=1536, out_dtype=F32, lead=(l,))[:2 * bsz] + ada_b[l]
        mods.append(mod.reshape(2 * bsz, 6, 1, d))

    (h,) = norm_mod(xs, norm_g[0, 0], mods[0][:, 0], mods[0][:, 1], n_t, n_c, (BF16,))
    for l in range(DEPTH):
        lambda_init = 0.8 - 0.6 * math.exp(-0.3 * l)
        sh_m, sc_m, gt_m, sh_f, sc_f, gt_f = (mods[l][:, k] for k in range(6))
        beta = mix_beta[l].reshape(1, 4 * GROUP_WIDTH)
        beta_a, beta_b, beta_r, beta_d = (beta[:, GROUP_WIDTH * k:GROUP_WIDTH * (k + 1)] for k in range(4))
        moe_layer = l % 2 == 1
        i = l // 2
        nxt_mixer = (norm_g[l + 1, 0], mods[l + 1][:, 0], mods[l + 1][:, 1], BF16) if l + 1 < DEPTH else None

        p_attn = mm(h, w_in, tn=768, out_dtype=F32, lead=(l,), n_cols=COL_ATTN)
        p_hy = mm(h, w_in, tn=768, out_dtype=F32, lead=(l,), col_off=COL_HY // 768,
                  n_cols=IN_WIDTH - COL_HY)
        aq, ak, av, dq, dk, dv, rq, rk, rv, rg = prep_heads(p_attn, rope_c, rope_sa, rope_sb, gqa_qk_g[l], bsz, n_t)

        a_out = gqa_attention(aq, ak, av, beta_a, n_t=n_t, n_c=n_c, tk=tk)

        lamf = diff_lambda[l]
        lam_full = (jnp.exp(jnp.sum(lamf[0] * lamf[1])) - jnp.exp(jnp.sum(lamf[2] * lamf[3])) + lambda_init)
        b_out = diff_attention(dq, dk, dv, lam_full.reshape(1, 1), diff_subln_g[l].reshape(1, -1), beta_b,
                               1.0 - lambda_init, n_t=n_t, n_c=n_c, tk=tk)

        o2 = retention(rq, rk, rv, ret_log_decay[l], n_t=n_t, n_c=n_c)
        r_out = ret_gate(o2, rg, ret_gn_g[l].reshape(1, -1), beta_r)

        filt_args = (hy_filt_w1[l], hy_filt_b1[l], hy_filt_w2[l], hy_filt_b2[l], hy_filt_w3[l])
        z_l, x1_l, x2_l = short_conv(p_hy, hy_short_w[l], hy_short_b[l], bsz=bsz, n_t=n_t, first_tile=n_c,
                                     n_seq_tiles=n_s, n_pad_tiles=n_s)
        d_l = hyena_long_conv(z_l, x1_l, x2_l, hyena_filter_taps(n_lat, *filt_args), hy_bias[l], beta_d[0],
                              bsz=bsz, length=n_lat, n1=n1_lat)
        z_c, x1_c, x2_c = short_conv(p_hy, hy_short_w[l], hy_short_b[l], bsz=bsz, n_t=n_t, first_tile=0,
                                     n_seq_tiles=n_c, n_pad_tiles=ctx_pad_tiles)
        d_c = hyena_long_conv(z_c, x1_c, x2_c, hyena_filter_taps(n_ctx, *filt_args), hy_bias[l], beta_d[0],
                              bsz=bsz, length=n_ctx, n1=n1_ctx)
        d_out = jnp.concatenate([d_c.reshape(bsz, -1, HY_WIDTH)[:, :n_ctx], d_l.reshape(bsz, n_lat, HY_WIDTH)],
                                axis=1).reshape(r, HY_WIDTH).astype(BF16)

        y = mm_parts([a_out, b_out, r_out, d_out], w_out, tn=512, lead=(l,))
        xs, f_in = gate_res(xs, y, norm_g[l, 1], gt_m, n_t, n_c,
                            nxt=(norm_g[l, 2], sh_f, sc_f, F32 if moe_layer else BF16))

        if not moe_layer:
            hid = swiglu_up(f_in, ffn_w_gate, ffn_w_up, tn=512, lead=(i,))
            f_out = mm(hid, ffn_w_down, tn=512, out_dtype=F32, lead=(i,), max_tile_mb=6)
            res = gate_res(xs, f_out, norm_g[l, 3], gt_f, n_t, n_c, nxt=nxt_mixer)
        else:
            top_i, top_w = router(f_in, moe_router[i].T)
            n_pad_rows = (2 * r // MOE_TM + N_EXPERTS) * MOE_TM
            row_token, row_weight, tile_expert, n_used, pos = route_tokens(top_i, top_w, n_pad_rows)
            y_sorted = moe_experts(f_in, moe_w_gate, moe_w_up, moe_w_down, i, row_token, row_weight,
                                   tile_expert, n_used)
            res = moe_combine_gate_res(xs, y_sorted, pos, norm_g[l, 3], gt_f, n_t, n_c, nxt=nxt_mixer)
        xs, h = res if nxt_mixer is not None else (res, None)

    return xs.reshape(bsz, t, d)[:, n_ctx:]
```

```python
import functools
import math

import numpy as np
import jax
import jax.numpy as jnp
from jax import lax
from jax.experimental import pallas as pl
from jax.experimental.pallas import tpu as pltpu

F32 = jnp.float32
BF16 = jnp.bfloat16

D_MODEL = 2048
DEPTH = 4
GRID_W = 64
HEAD_DIM = 64
ROPE_THETA = 10000.0
NORM_EPS = 1e-6
GROUP_WIDTH = 512
GQA_Q_HEADS = 8
GQA_KV_HEADS = 2
GQA_REP = 4
DIFF_HEADS = 4
RET_HEADS = 4
HY_WIDTH = 512
HY_ORDER = 2
HY_EMB_DIM = 33
HY_BANDS = 16
HY_FILT_HIDDEN = 64
HY_MIN_DECAY = math.log(1e-2) / 1.5
HY_MAX_DECAY = math.log(1e-2) / 0.3
N_EXPERTS = 8

COL_ATTN = 3840
COL_DIFF = 768
COL_RET = 2304
COL_RET_GATE = 3328
COL_HY = 3840
IN_WIDTH = 5376

ROW_TILE = 256
MOE_TM = 1024
MOE_TF = 256
MOE_STEPS = 4096 // MOE_TF
FFT_N2 = 128
FFT_MID_F = 8
FFT_SUB = 8
VMEM_LIMIT_MB = 48


def _cparams(sem, vmem_mb=VMEM_LIMIT_MB):
    return pltpu.CompilerParams(dimension_semantics=sem, vmem_limit_bytes=vmem_mb * 1024 * 1024)


def _pcall(kern, **kw):
    fn = kern.func if isinstance(kern, functools.partial) else kern
    return pl.pallas_call(kern, name=fn.__name__.strip("_"), **kw)


def _pick_tm(m, k, max_tile_mb):
    for tm in (1536, 1280, 1024, 768, 512, 256, 128, 16):
        if m % tm == 0 and tm * k * 2 <= max_tile_mb * 2 ** 20:
            return tm
    raise ValueError((m, k))


def _mm_kernel(a_ref, b_ref, o_ref, bb_ref):
    @pl.when(pl.program_id(1) == 0)
    def _():
        bb_ref[...] = b_ref[...].astype(BF16)

    o_ref[...] = jnp.dot(a_ref[...], bb_ref[...], preferred_element_type=F32).astype(o_ref.dtype)


def _b_spec(b, lead, k, tn, col_off):
    nlead = len(lead)
    return pl.BlockSpec((None,) * nlead + (k, tn), lambda j, i: tuple(lead) + (0, j + col_off))


def mm(a, b, *, tn, out_dtype, lead=(), col_off=0, n_cols=None, max_tile_mb=9):
    m, k = a.shape
    n = b.shape[-1] if n_cols is None else n_cols
    tm = _pick_tm(m, k, max_tile_mb)
    assert n % tn == 0 and a.dtype == BF16, (n, tn, a.dtype)
    return _pcall(
        _mm_kernel,
        grid=(n // tn, m // tm),
        in_specs=[pl.BlockSpec((tm, k), lambda j, i: (i, 0)), _b_spec(b, lead, k, tn, col_off)],
        out_specs=pl.BlockSpec((tm, tn), lambda j, i: (i, j)),
        out_shape=jax.ShapeDtypeStruct((m, n), out_dtype),
        scratch_shapes=[pltpu.VMEM((k, tn), BF16)],
        compiler_params=_cparams(("parallel", "arbitrary")),
    )(a, b)


def _mm_parts_kernel(*refs, n_parts):
    a_refs, (b_ref, o_ref, bb_ref) = refs[:n_parts], refs[n_parts:]

    @pl.when(pl.program_id(1) == 0)
    def _():
        bb_ref[...] = b_ref[...].astype(BF16)

    kp = a_refs[0].shape[1]
    acc = jnp.dot(a_refs[0][...], bb_ref[0:kp, :], preferred_element_type=F32)
    for g in range(1, n_parts):
        acc += jnp.dot(a_refs[g][...], bb_ref[g * kp:(g + 1) * kp, :], preferred_element_type=F32)
    o_ref[...] = acc


def mm_parts(parts, b, *, tn, lead):
    m, kp = parts[0].shape
    k = kp * len(parts)
    n = b.shape[-1]
    tm = _pick_tm(m, k, 9)
    row = pl.BlockSpec((tm, kp), lambda j, i: (i, 0))
    return _pcall(
        functools.partial(_mm_parts_kernel, n_parts=len(parts)),
        grid=(n // tn, m // tm),
        in_specs=[row] * len(parts) + [_b_spec(b, lead, k, tn, 0)],
        out_specs=pl.BlockSpec((tm, tn), lambda j, i: (i, j)),
        out_shape=jax.ShapeDtypeStruct((m, n), F32),
        scratch_shapes=[pltpu.VMEM((k, tn), BF16)],
        compiler_params=_cparams(("parallel", "arbitrary")),
    )(*parts, b)


def _swiglu_up_kernel(a_ref, wg_ref, wu_ref, o_ref, bg_ref, bu_ref):
    @pl.when(pl.program_id(1) == 0)
    def _():
        bg_ref[...] = wg_ref[...].astype(BF16)
        bu_ref[...] = wu_ref[...].astype(BF16)

    a = a_ref[...]
    g = jnp.dot(a, bg_ref[...], preferred_element_type=F32)
    u = jnp.dot(a, bu_ref[...], preferred_element_type=F32)
    o_ref[...] = (g * jax.nn.sigmoid(g) * u).astype(o_ref.dtype)


def swiglu_up(a, wg, wu, *, tn, lead):
    m, k = a.shape
    n = wg.shape[-1]
    tm = _pick_tm(m, k, 7)
    assert n % tn == 0
    return _pcall(
        _swiglu_up_kernel,
        grid=(n // tn, m // tm),
        in_specs=[pl.BlockSpec((tm, k), lambda j, i: (i, 0)),
                  _b_spec(wg, lead, k, tn, 0), _b_spec(wu, lead, k, tn, 0)],
        out_specs=pl.BlockSpec((tm, tn), lambda j, i: (i, j)),
        out_shape=jax.ShapeDtypeStruct((m, n), BF16),
        scratch_shapes=[pltpu.VMEM((k, tn), BF16)] * 2,
        compiler_params=_cparams(("parallel", "arbitrary")),
    )(a, wg, wu)


def _seg_map(n_t, n_c):
    return lambda i: (2 * (i // n_t) + jnp.where(i % n_t >= n_c, 1, 0), 0, 0)


def _rms(x):
    return x * lax.rsqrt(jnp.mean(x * x, axis=-1, keepdims=True) + NORM_EPS)


def _norm_mod_kernel(x_ref, g_ref, sh_ref, sc_ref, *o_refs):
    y = _rms(x_ref[...]) * g_ref[...]
    y = y * (1.0 + sc_ref[...]) + sh_ref[...]
    for o_ref in o_refs:
        o_ref[...] = y.astype(o_ref.dtype)


def norm_mod(x, g, shift, scale, n_t, n_c, out_dtypes):
    r, d = x.shape
    seg = _seg_map(n_t, n_c)
    row = pl.BlockSpec((ROW_TILE, d), lambda i: (i, 0))
    outs = _pcall(
        _norm_mod_kernel,
        grid=(r // ROW_TILE,),
        in_specs=[row, pl.BlockSpec((1, d), lambda i: (0, 0)),
                  pl.BlockSpec((None, 1, d), seg), pl.BlockSpec((None, 1, d), seg)],
        out_specs=[row] * len(out_dtypes),
        out_shape=[jax.ShapeDtypeStruct((r, d), dt) for dt in out_dtypes],
        compiler_params=_cparams(("parallel",)),
    )(x, g.reshape(1, d), shift, scale)
    return outs


def _gate_res_kernel(x_ref, y_ref, g_ref, gt_ref, *rest):
    x = x_ref[...] + gt_ref[...] * (_rms(y_ref[...]) * g_ref[...])
    if len(rest) == 1:
        rest[0][...] = x
    else:
        g2_ref, sh_ref, sc_ref, o_ref, h_ref = rest
        o_ref[...] = x
        h_ref[...] = (_rms(x) * g2_ref[...] * (1.0 + sc_ref[...]) + sh_ref[...]).astype(h_ref.dtype)


def gate_res(x, y, g, gate, n_t, n_c, nxt=None):
    r, d = x.shape
    seg = _seg_map(n_t, n_c)
    row = pl.BlockSpec((ROW_TILE, d), lambda i: (i, 0))
    vec = pl.BlockSpec((1, d), lambda i: (0, 0))
    mod = pl.BlockSpec((None, 1, d), seg)
    in_specs, args = [row, row, vec, mod], [x, y, g.reshape(1, d), gate]
    out_specs, out_shape = row, jax.ShapeDtypeStruct((r, d), F32)
    if nxt is not None:
        g2, shift, scale, dtype = nxt
        in_specs, args = in_specs + [vec, mod, mod], args + [g2.reshape(1, d), shift, scale]
        out_specs, out_shape = [row, row], [out_shape, jax.ShapeDtypeStruct((r, d), dtype)]
    return _pcall(
        _gate_res_kernel,
        grid=(r // ROW_TILE,),
        in_specs=in_specs,
        out_specs=out_specs,
        out_shape=out_shape,
        compiler_params=_cparams(("parallel",)),
    )(*args)


def _router_kernel(f_ref, rt_ref, idx_ref, w_ref):
    lt = lax.dot_general(rt_ref[...], f_ref[...], (((1,), (1,)), ((), ())),
                         precision=lax.Precision.HIGHEST, preferred_element_type=F32)
    e_idx = lax.broadcasted_iota(jnp.int32, lt.shape, 0)
    m1 = jnp.max(lt, axis=0, keepdims=True)
    i1 = jnp.min(jnp.where(lt == m1, e_idx, N_EXPERTS), axis=0, keepdims=True)
    rest = jnp.where(e_idx == i1, -jnp.inf, lt)
    m2 = jnp.max(rest, axis=0, keepdims=True)
    i2 = jnp.min(jnp.where(rest == m2, e_idx, N_EXPERTS), axis=0, keepdims=True)
    e2 = jnp.exp(m2 - m1)
    idx_ref[...] = jnp.concatenate([i1, i2], axis=0)
    w_ref[...] = jnp.concatenate([1.0 / (1.0 + e2), e2 / (1.0 + e2)], axis=0)


def router(f, router_t):
    r, d = f.shape
    return _pcall(
        _router_kernel,
        grid=(r // ROW_TILE,),
        in_specs=[pl.BlockSpec((ROW_TILE, d), lambda i: (i, 0)),
                  pl.BlockSpec((N_EXPERTS, d), lambda i: (0, 0))],
        out_specs=[pl.BlockSpec((2, ROW_TILE), lambda i: (0, i))] * 2,
        out_shape=[jax.ShapeDtypeStruct((2, r), jnp.int32), jax.ShapeDtypeStruct((2, r), F32)],
        compiler_params=_cparams(("parallel",)),
    )(f, router_t)


def _row_copy(src_hbm, dst, src_row, dst_row, sem):
    return pltpu.make_async_copy(src_hbm.at[pl.ds(src_row, 1), :], dst.at[pl.ds(dst_row, 1), :], sem)


def _start_rows(src_hbm, dst, idx_ref, base, n_rows, sem):
    def start(r, carry):
        _row_copy(src_hbm, dst, idx_ref[base + r], r, sem).start()
        return carry

    lax.fori_loop(0, n_rows, start, 0, unroll=8)


def _wait_rows(src_hbm, dst, idx_ref, base, n_rows, sem):
    def wait(r, carry):
        _row_copy(src_hbm, dst, idx_ref[base + r], r, sem).wait()
        return carry

    lax.fori_loop(0, n_rows, wait, 0, unroll=8)


def _gather_rows(src_hbm, dst, idx_ref, base, n_rows, sem):
    _start_rows(src_hbm, dst, idx_ref, base, n_rows, sem)
    _wait_rows(src_hbm, dst, idx_ref, base, n_rows, sem)


def route_tokens(idx, w, n_pad_rows):
    r = idx.shape[1]
    n_assign = 2 * r
    n_tiles = n_pad_rows // MOE_TM
    e_flat = idx.reshape(n_assign)
    w_flat = w.reshape(n_assign)
    counts = jnp.sum((e_flat[:, None] == jnp.arange(N_EXPERTS)[None, :]).astype(jnp.int32), axis=0)
    padded = (counts + MOE_TM - 1) // MOE_TM * MOE_TM
    group_end = jnp.cumsum(padded)
    group_start = group_end - padded
    sorted_start = jnp.cumsum(counts) - counts
    order = jnp.argsort(e_flat, stable=True).astype(jnp.int32)
    inv = jnp.argsort(order).astype(jnp.int32)
    pos = group_start[e_flat] + (inv - sorted_start[e_flat])
    n_used = group_end[-1] // MOE_TM
    tile_start = jnp.arange(n_tiles, dtype=jnp.int32) * MOE_TM
    tile_start = jnp.minimum(tile_start, group_end[-1] - 1)
    tile_expert = jnp.sum((tile_start[:, None] >= group_end[None, :]).astype(jnp.int32), axis=1)
    rows = jnp.arange(n_pad_rows, dtype=jnp.int32)
    row_e = jnp.repeat(tile_expert, MOE_TM)
    rank = rows - group_start[row_e]
    valid = rank < counts[row_e]
    src = order[jnp.clip(sorted_start[row_e] + rank, 0, n_assign - 1)]
    row_token = jnp.where(valid, src % r, 0)
    row_weight = jnp.where(valid, w_flat[src], 0.0)
    return (row_token, row_weight.reshape(n_pad_rows, 1), tile_expert.astype(jnp.int32),
            n_used.astype(jnp.int32).reshape(1), pos)


def _moe_expert_kernel(te_ref, nu_ref, tok_ref, f_hbm, wg_ref, wu_ref, wd_ref, rw_ref, o_ref, land_ref, xb_ref, sem):
    i, f = pl.program_id(0), pl.program_id(1)
    nf = pl.num_programs(1)
    n_used = nu_ref[0]
    used = i < n_used
    rows_per_step = MOE_TM // MOE_STEPS
    nxt = jnp.minimum(i + 1, n_used - 1)

    def row_slice(tile, step):
        base = tile * MOE_TM + step * rows_per_step
        return f_hbm, land_ref.at[pl.ds(step * rows_per_step, rows_per_step)], tok_ref, base, rows_per_step, sem

    @pl.when(jnp.logical_and(used, f == 0))
    def _():
        @pl.when(i == 0)
        def _():
            _start_rows(f_hbm, land_ref, tok_ref, 0, MOE_TM, sem)

        _wait_rows(f_hbm, land_ref, tok_ref, i * MOE_TM, MOE_TM, sem)
        xb_ref[...] = land_ref[...].astype(BF16)

    @pl.when(f == 0)
    def _():
        o_ref[...] = jnp.zeros_like(o_ref)

    @pl.when(used)
    def _():
        src, dst, idx_ref, base, n, _ = row_slice(nxt, f)
        for r in range(n):
            _row_copy(src, dst, idx_ref[base + r], r, sem).start()
        x = xb_ref[...]
        g = jnp.dot(x, wg_ref[...].astype(BF16), preferred_element_type=F32)
        u = jnp.dot(x, wu_ref[...].astype(BF16), preferred_element_type=F32)
        h = (g * jax.nn.sigmoid(g) * u).astype(BF16)
        o_ref[...] += jnp.dot(h, wd_ref[...].astype(BF16), preferred_element_type=F32)

    @pl.when(jnp.logical_and(used, f == nf - 1))
    def _():
        o_ref[...] = o_ref[...] * rw_ref[...]

    @pl.when(jnp.logical_and(i == n_used - 1, f == nf - 1))
    def _():
        _wait_rows(f_hbm, land_ref, tok_ref, i * MOE_TM, MOE_TM, sem)


def moe_experts(f_in, wg, wu, wd, layer, row_token, row_weight, tile_expert, n_used):
    r, d = f_in.shape
    ff = wg.shape[-1]
    n_pad_rows = row_token.shape[0]
    nt, nf = n_pad_rows // MOE_TM, ff // MOE_TF
    assert nf == MOE_STEPS and MOE_TM % MOE_STEPS == 0

    def fi(i, f, nu):
        return jnp.where(i < nu[0], f, nf - 1)

    grid_spec = pltpu.PrefetchScalarGridSpec(
        num_scalar_prefetch=3,
        grid=(nt, nf),
        in_specs=[pl.BlockSpec(memory_space=pl.ANY),
                  pl.BlockSpec((None, None, d, MOE_TF), lambda i, f, te, nu, tok: (layer, te[i], 0, fi(i, f, nu))),
                  pl.BlockSpec((None, None, d, MOE_TF), lambda i, f, te, nu, tok: (layer, te[i], 0, fi(i, f, nu))),
                  pl.BlockSpec((None, None, MOE_TF, d), lambda i, f, te, nu, tok: (layer, te[i], fi(i, f, nu), 0)),
                  pl.BlockSpec((MOE_TM, 1), lambda i, f, te, nu, tok: (i, 0))],
        out_specs=pl.BlockSpec((MOE_TM, d), lambda i, f, te, nu, tok: (i, 0)),
        scratch_shapes=[pltpu.VMEM((MOE_TM, d), F32), pltpu.VMEM((MOE_TM, d), BF16), pltpu.SemaphoreType.DMA(())],
    )
    return _pcall(
        _moe_expert_kernel,
        grid_spec=grid_spec,
        out_shape=jax.ShapeDtypeStruct((n_pad_rows, d), F32),
        compiler_params=_cparams(("arbitrary", "arbitrary")),
    )(tile_expert, n_used, row_token, f_in, wg, wu, wd, row_weight)


def _moe_combine_kernel(pos_ref, y_hbm, x_ref, g_ref, gt_ref, *rest):
    buf_ref, sem = rest[-2:]
    i = pl.program_id(0)

    def rows(fn, tile):
        for k in range(2):
            fn(y_hbm, buf_ref.at[tile % 2, k], pos_ref, (2 * tile + k) * ROW_TILE, ROW_TILE, sem)

    @pl.when(i == 0)
    def _():
        rows(_start_rows, i)

    rows(_wait_rows, i)

    @pl.when(i + 1 < pl.num_programs(0))
    def _():
        rows(_start_rows, i + 1)

    y = buf_ref[i % 2, 0] + buf_ref[i % 2, 1]
    x = x_ref[...] + gt_ref[...] * (_rms(y) * g_ref[...])
    if len(rest) == 3:
        rest[0][...] = x
    else:
        g2_ref, sh_ref, sc_ref, o_ref, h_ref = rest[:5]
        o_ref[...] = x
        h_ref[...] = (_rms(x) * g2_ref[...] * (1.0 + sc_ref[...]) + sh_ref[...]).astype(h_ref.dtype)


def moe_combine_gate_res(x, y_sorted, pos, g, gate, n_t, n_c, nxt=None):
    r, d = x.shape
    nrt = r // ROW_TILE
    pos_tiles = pos.reshape(2, nrt, ROW_TILE).transpose(1, 0, 2).reshape(2 * r)
    seg = _seg_map(n_t, n_c)
    row = pl.BlockSpec((ROW_TILE, d), lambda i, p: (i, 0))
    vec = pl.BlockSpec((1, d), lambda i, p: (0, 0))
    mod = pl.BlockSpec((None, 1, d), lambda i, p: seg(i))
    in_specs, args = [pl.BlockSpec(memory_space=pl.ANY), row, vec, mod], [y_sorted, x, g.reshape(1, d), gate]
    out_specs, out_shape = row, jax.ShapeDtypeStruct((r, d), F32)
    if nxt is not None:
        g2, shift, scale, dtype = nxt
        in_specs, args = in_specs + [vec, mod, mod], args + [g2.reshape(1, d), shift, scale]
        out_specs, out_shape = [row, row], [out_shape, jax.ShapeDtypeStruct((r, d), dtype)]
    grid_spec = pltpu.PrefetchScalarGridSpec(
        num_scalar_prefetch=1,
        grid=(nrt,),
        in_specs=in_specs,
        out_specs=out_specs,
        scratch_shapes=[pltpu.VMEM((2, 2, ROW_TILE, d), F32), pltpu.SemaphoreType.DMA(())],
    )
    return _pcall(
        _moe_combine_kernel,
        grid_spec=grid_spec,
        out_shape=out_shape,
        compiler_params=_cparams(("arbitrary",)),
    )(pos_tiles, *args)


def _rope(x, c, sa, sb):
    return x * c + pltpu.roll(x, 112, 1) * sa + pltpu.roll(x, 16, 1) * sb


def _head_rms(x, g, bd):
    sq = x * x
    hi = sq.astype(BF16)
    lo = (sq - hi.astype(F32)).astype(BF16)
    ms = jnp.dot(hi, bd, preferred_element_type=F32) + jnp.dot(lo, bd, preferred_element_type=F32)
    return x * lax.rsqrt(ms + NORM_EPS) * g


def _prep_kernel(p_ref, c_ref, sa_ref, sb_ref, gq_ref, gk_ref, bd_ref,
                 aq_ref, ak_ref, av_ref, dq_ref, dk_ref, dv_ref, rq_ref, rk_ref, rv_ref, rg_ref):
    c, sa, sb = c_ref[...], sa_ref[...], sb_ref[...]
    bd = bd_ref[...]
    scale = HEAD_DIM ** -0.5
    qscale = scale * math.log2(math.e)

    def chunk(j):
        return p_ref[:, 128 * j:128 * (j + 1)]

    def put_heads(ref, first, val):
        ref[first] = val[:, :HEAD_DIM].astype(ref.dtype)
        ref[first + 1] = val[:, HEAD_DIM:].astype(ref.dtype)

    for j in range(4):
        put_heads(aq_ref, 2 * j, _rope(_head_rms(chunk(j), gq_ref[...], bd), c, sa, sb) * qscale)
    put_heads(ak_ref, 0, _rope(_head_rms(chunk(4), gk_ref[...], bd), c, sa, sb))
    one64 = (lax.broadcasted_iota(jnp.int32, (ROW_TILE, HEAD_DIM), 1) == 0).astype(av_ref.dtype)
    one128 = (lax.broadcasted_iota(jnp.int32, (ROW_TILE, 2 * HEAD_DIM), 1) == 0).astype(dv_ref.dtype)
    v_gqa = chunk(5)
    for h in range(GQA_KV_HEADS):
        av_ref[h, :, :HEAD_DIM] = v_gqa[:, HEAD_DIM * h:HEAD_DIM * (h + 1)].astype(av_ref.dtype)
        av_ref[h, :, HEAD_DIM:] = one64
    for j in range(4):
        put_heads(dq_ref, 2 * j, _rope(chunk(6 + j), c, sa, sb) * qscale)
        put_heads(dk_ref, 2 * j, _rope(chunk(10 + j), c, sa, sb))
        dv_ref[j, :, :2 * HEAD_DIM] = chunk(14 + j).astype(dv_ref.dtype)
        dv_ref[j, :, 2 * HEAD_DIM:] = one128
    for j in range(2):
        put_heads(rq_ref, 2 * j, _rope(chunk(18 + j), c, sa, sb))
        put_heads(rk_ref, 2 * j, _rope(chunk(20 + j), c, sa, sb) * scale)
    for j in range(4):
        rv_ref[j] = chunk(22 + j).astype(rv_ref.dtype)
    rg_ref[...] = p_ref[:, COL_RET_GATE:COL_ATTN]


def prep_heads(p_attn, rope_c, rope_sa, rope_sb, qk_g, bsz, n_t):
    t = n_t * ROW_TILE
    bd = jnp.asarray(np.kron(np.eye(2), np.full((HEAD_DIM, HEAD_DIM), 1.0 / HEAD_DIM)), F32).astype(BF16)
    gq = jnp.tile(qk_g[0], 2).reshape(1, 128)
    gk = jnp.tile(qk_g[1], 2).reshape(1, 128)
    tab = pl.BlockSpec((ROW_TILE, 128), lambda i: (i % n_t, 0))
    vec = pl.BlockSpec((1, 128), lambda i: (0, 0))

    def heads(nh, dh):
        return (pl.BlockSpec((None, nh, ROW_TILE, dh), lambda i: (i // n_t, 0, i % n_t, 0)),
                jax.ShapeDtypeStruct((bsz, nh, t, dh), BF16))

    outs = [heads(8, 64), heads(2, 64), heads(2, 128), heads(8, 64), heads(8, 64), heads(4, 256),
            heads(4, 64), heads(4, 64), heads(4, 128)]
    return _pcall(
        _prep_kernel,
        grid=(bsz * n_t,),
        in_specs=[pl.BlockSpec((ROW_TILE, COL_ATTN), lambda i: (i, 0)), tab, tab, tab, vec, vec,
                  pl.BlockSpec((128, 128), lambda i: (0, 0))],
        out_specs=[o[0] for o in outs] + [pl.BlockSpec((ROW_TILE, GROUP_WIDTH), lambda i: (i, 0))],
        out_shape=[o[1] for o in outs] + [jax.ShapeDtypeStruct((bsz * t, GROUP_WIDTH), F32)],
        compiler_params=_cparams(("parallel",)),
    )(p_attn, rope_c, rope_sa, rope_sb, gq, gk, bd)


def _nt_dot(a, b):
    return lax.dot_general(a, b, (((1,), (1,)), ((), ())), preferred_element_type=F32)


def _softmax_chunks(score_fn, v_ref, s_ref, rows, dv, n_chunks, tk, n_ctx_keys, is_ctx):
    def update(s, start, carry):
        m, acc = carry
        m_new = jnp.maximum(m, jnp.max(s, axis=-1, keepdims=True))
        p = jnp.exp2(s - m_new)
        acc = jnp.exp2(m - m_new) * acc + jnp.dot(p.astype(BF16), v_ref[pl.ds(start, tk), :],
                                                  preferred_element_type=F32)
        return m_new, acc

    init = (jnp.full((rows, 1), -1e30, F32), jnp.zeros((rows, v_ref.shape[-1]), F32))

    def ctx_tile():
        s = score_fn(0, tk)
        col = lax.broadcasted_iota(jnp.int32, s.shape, 1)
        return update(jnp.where(col < n_ctx_keys, s, -1e30), 0, init)

    def lat_tile():
        stats = init
        s_ref[0] = score_fn(0, tk)
        for j in range(n_chunks):
            if j + 1 < n_chunks:
                s_ref[(j + 1) % 2] = score_fn((j + 1) * tk, tk)
            stats = update(s_ref[j % 2], j * tk, stats)
        return stats

    _, acc = lax.cond(is_ctx, ctx_tile, lat_tile)
    return acc[:, :dv] / acc[:, dv:dv + 1]


def _gqa_kernel(q_ref, k_ref, v_ref, beta_ref, o_ref, s_ref, *, tq, tk, n_chunks, n_ctx_tiles, n_ctx_keys):
    is_ctx = pl.program_id(2) < n_ctx_tiles
    q = q_ref[...].reshape(GQA_REP * tq, HEAD_DIM)

    def scores(start, size):
        return _nt_dot(q, k_ref[pl.ds(start, size), :])

    o = _softmax_chunks(scores, v_ref, s_ref, GQA_REP * tq, HEAD_DIM, n_chunks, tk, n_ctx_keys, is_ctx)
    for r in range(GQA_REP):
        sl = slice(HEAD_DIM * r, HEAD_DIM * (r + 1))
        o_ref[:, sl] = (o[r * tq:(r + 1) * tq] * beta_ref[:, sl]).astype(o_ref.dtype)


def gqa_attention(q, k, v, beta, *, n_t, n_c, tk):
    bsz, _, t, _ = q.shape
    tq = ROW_TILE
    kern = functools.partial(_gqa_kernel, tq=tq, tk=tk, n_chunks=t // tk, n_ctx_tiles=n_c,
                             n_ctx_keys=n_c * ROW_TILE)
    kspec = pl.BlockSpec((None, None, t, HEAD_DIM), lambda b, g, i: (b, g, 0, 0))
    vspec = pl.BlockSpec((None, None, t, v.shape[-1]), lambda b, g, i: (b, g, 0, 0))
    w = GQA_REP * HEAD_DIM
    return _pcall(
        kern,
        grid=(bsz, GQA_KV_HEADS, n_t),
        in_specs=[pl.BlockSpec((None, GQA_REP, tq, HEAD_DIM), lambda b, g, i: (b, g, i, 0)), kspec, vspec,
                  pl.BlockSpec((1, w), lambda b, g, i: (0, g))],
        out_specs=pl.BlockSpec((tq, w), lambda b, g, i: (b * n_t + i, g)),
        out_shape=jax.ShapeDtypeStruct((bsz * t, GROUP_WIDTH), BF16),
        scratch_shapes=[pltpu.VMEM((2, GQA_REP * tq, tk), F32)],
        compiler_params=_cparams(("parallel", "parallel", "arbitrary")),
    )(q, k, v, beta)


def _diff_kernel(q_ref, k_ref, v_ref, lam_ref, g_ref, beta_ref, o_ref, s_ref, *, tq, tk, n_chunks, n_ctx_tiles,
                 n_ctx_keys, out_scale):
    is_ctx = pl.program_id(2) < n_ctx_tiles
    q0, q1 = q_ref[0], q_ref[1]

    def scores(start, size):
        return jnp.concatenate([_nt_dot(q0, k_ref[0, pl.ds(start, size), :]),
                                _nt_dot(q1, k_ref[1, pl.ds(start, size), :])], axis=0)

    a = _softmax_chunks(scores, v_ref, s_ref, 2 * tq, 2 * HEAD_DIM, n_chunks, tk, n_ctx_keys, is_ctx)
    o = a[:tq] - lam_ref[...] * a[tq:]
    o = _rms(o) * g_ref[...] * out_scale
    o_ref[...] = (o * beta_ref[...]).astype(o_ref.dtype)


def diff_attention(q, k, v, lam, subln_g, beta, out_scale, *, n_t, n_c, tk):
    bsz, _, t, _ = q.shape
    tq = ROW_TILE
    dv = 2 * HEAD_DIM
    kern = functools.partial(_diff_kernel, tq=tq, tk=tk, n_chunks=t // tk, n_ctx_tiles=n_c,
                             n_ctx_keys=n_c * ROW_TILE, out_scale=out_scale)
    return _pcall(
        kern,
        grid=(bsz, DIFF_HEADS, n_t),
        in_specs=[pl.BlockSpec((None, 2, tq, HEAD_DIM), lambda b, h, i: (b, h, i, 0)),
                  pl.BlockSpec((None, 2, t, HEAD_DIM), lambda b, h, i: (b, h, 0, 0)),
                  pl.BlockSpec((None, None, t, v.shape[-1]), lambda b, h, i: (b, h, 0, 0)),
                  pl.BlockSpec((1, 1), lambda b, h, i: (0, 0)),
                  pl.BlockSpec((1, dv), lambda b, h, i: (0, 0)),
                  pl.BlockSpec((1, dv), lambda b, h, i: (0, h))],
        out_specs=pl.BlockSpec((tq, dv), lambda b, h, i: (b * n_t + i, h)),
        out_shape=jax.ShapeDtypeStruct((bsz * t, GROUP_WIDTH), BF16),
        scratch_shapes=[pltpu.VMEM((2, 2 * tq, tk), F32)],
        compiler_params=_cparams(("parallel", "parallel", "arbitrary")),
    )(q, k, v, lam, subln_g, beta)


def _ret_kernel(lg_ref, q_ref, k_ref, v_ref, o_ref, state_ref, decay_ref, xi_ref, zeta_ref):
    d = pl.program_id(1)
    c = ROW_TILE
    dv = v_ref.shape[-1]

    @pl.when(pl.program_id(2) == 0)
    def _():
        state_ref[...] = jnp.zeros_like(state_ref)
        fwd = d == 0
        ii = lax.broadcasted_iota(jnp.int32, (c, c), 0)
        jj = lax.broadcasted_iota(jnp.int32, (c, c), 1)
        rel = jnp.where(fwd, ii - jj, jj - ii).astype(F32)
        pos = lax.broadcasted_iota(jnp.int32, (c, 1), 0).astype(F32)
        for h in range(RET_HEADS):
            lg = lg_ref[h]
            decay_ref[h] = jnp.where(rel >= 0, jnp.exp(jnp.maximum(rel, 0.0) * lg), 0.0)
            xi_ref[h] = jnp.exp(jnp.where(fwd, pos + 1.0, c - pos) * lg)
            zeta_ref[h] = jnp.exp(jnp.where(fwd, c - 1.0 - pos, pos) * lg)

    for h in range(RET_HEADS):
        q, k, v = q_ref[h], k_ref[h], v_ref[h]
        state = state_ref[h]
        scores = _nt_dot(q, k) * decay_ref[h]
        inner = jnp.dot(scores.astype(BF16), v, preferred_element_type=F32)
        cross = jnp.dot(q, state.astype(BF16), preferred_element_type=F32) * xi_ref[h]
        o_ref[:, dv * h:dv * (h + 1)] = inner + cross
        kz = (k.astype(F32) * zeta_ref[h]).astype(BF16)
        upd = lax.dot_general(kz, v, (((0,), (0,)), ((), ())), preferred_element_type=F32)
        state_ref[h] = jnp.exp(c * lg_ref[h]) * state + upd


def retention(q, k, v, log_decay, *, n_t, n_c):
    bsz, nh, t, dk = q.shape
    dv = v.shape[-1]

    def blk(d, j):
        back = jnp.where(j < n_c, n_c - 1 - j, n_t - 1 - (j - n_c))
        return jnp.where(d == 0, j, back)

    qk = pl.BlockSpec((None, nh, ROW_TILE, dk), lambda b, d, j: (b, 0, blk(d, j), 0))
    return _pcall(
        _ret_kernel,
        grid=(bsz, 2, n_t),
        in_specs=[pl.BlockSpec((None, nh, 1, 1), lambda b, d, j: (d, 0, 0, 0)), qk, qk,
                  pl.BlockSpec((None, nh, ROW_TILE, dv), lambda b, d, j: (b, 0, blk(d, j), 0))],
        out_specs=pl.BlockSpec((None, ROW_TILE, nh * dv), lambda b, d, j: (d, b * n_t + blk(d, j), 0)),
        out_shape=jax.ShapeDtypeStruct((2, bsz * t, nh * dv), F32),
        scratch_shapes=[pltpu.VMEM((nh, dk, dv), F32), pltpu.VMEM((nh, ROW_TILE, ROW_TILE), F32),
                        pltpu.VMEM((nh, ROW_TILE, 1), F32), pltpu.VMEM((nh, ROW_TILE, 1), F32)],
        compiler_params=_cparams(("parallel", "arbitrary", "arbitrary")),
    )(log_decay.reshape(2, RET_HEADS, 1, 1), q, k, v)


def _ret_gate_kernel(of_ref, ob_ref, g_ref, gn_ref, beta_ref, o_ref):
    dv = 2 * HEAD_DIM
    for h in range(RET_HEADS):
        sl = slice(dv * h, dv * (h + 1))
        o = of_ref[:, sl] + ob_ref[:, sl]
        mu = jnp.mean(o, axis=-1, keepdims=True)
        var = jnp.mean(jnp.square(o - mu), axis=-1, keepdims=True)
        y = (o - mu) * lax.rsqrt(var + NORM_EPS) * gn_ref[:, sl]
        g = g_ref[:, sl]
        o_ref[:, sl] = (g * jax.nn.sigmoid(g) * y * beta_ref[:, sl]).astype(o_ref.dtype)


def ret_gate(o2, gate, gn_g, beta):
    _, r, w = o2.shape
    vec = pl.BlockSpec((1, w), lambda i: (0, 0))
    row = pl.BlockSpec((ROW_TILE, w), lambda i: (i, 0))
    return _pcall(
        _ret_gate_kernel,
        grid=(r // ROW_TILE,),
        in_specs=[pl.BlockSpec((None, ROW_TILE, w), lambda i: (0, i, 0)),
                  pl.BlockSpec((None, ROW_TILE, w), lambda i: (1, i, 0)), row, vec, vec],
        out_specs=row,
        out_shape=jax.ShapeDtypeStruct((r, w), BF16),
        compiler_params=_cparams(("parallel",)),
    )(o2, o2, gate, gn_g, beta)


def _short_conv_kernel(cur_ref, prev_ref, next_ref, w_ref, b_ref, z_ref, x1_ref, x2_ref, *, n_seq_tiles):
    j = pl.program_id(1)

    @pl.when(j < n_seq_tiles)
    def _():
        u = cur_ref[...]
        rows = lax.broadcasted_iota(jnp.int32, u.shape, 0)
        prev_row = jnp.where(j == 0, 0.0, prev_ref[7:8, :])
        next_row = jnp.where(j == n_seq_tiles - 1, 0.0, next_ref[0:1, :])
        up = jnp.where(rows == 0, prev_row, pltpu.roll(u, 1, 0))
        un = jnp.where(rows == ROW_TILE - 1, next_row, pltpu.roll(u, ROW_TILE - 1, 0))
        y = up * w_ref[0:1, :] + u * w_ref[1:2, :] + un * w_ref[2:3, :] + b_ref[...]
        z_ref[...] = y[:, :HY_WIDTH]
        x1_ref[...] = y[:, HY_WIDTH:2 * HY_WIDTH]
        x2_ref[...] = y[:, 2 * HY_WIDTH:]

    @pl.when(j >= n_seq_tiles)
    def _():
        z_ref[...] = jnp.zeros_like(z_ref)
        x1_ref[...] = jnp.zeros_like(x1_ref)
        x2_ref[...] = jnp.zeros_like(x2_ref)


def short_conv(p_hy, w, b, *, bsz, n_t, first_tile, n_seq_tiles, n_pad_tiles):
    w3 = 3 * HY_WIDTH
    sub = ROW_TILE // 8
    last_blk8 = p_hy.shape[0] // 8 - 1

    def cur(bb, j):
        return (bb * n_t + first_tile + jnp.minimum(j, n_seq_tiles - 1), 0)

    def prev(bb, j):
        return (jnp.maximum(cur(bb, j)[0] * sub - 1, 0), 0)

    def nxt(bb, j):
        return (jnp.minimum((cur(bb, j)[0] + 1) * sub, last_blk8), 0)

    kern = functools.partial(_short_conv_kernel, n_seq_tiles=n_seq_tiles)
    rows = bsz * n_pad_tiles * ROW_TILE
    return _pcall(
        kern,
        grid=(bsz, n_pad_tiles),
        in_specs=[pl.BlockSpec((ROW_TILE, w3), cur), pl.BlockSpec((8, w3), prev), pl.BlockSpec((8, w3), nxt),
                  pl.BlockSpec((3, w3), lambda bb, j: (0, 0)), pl.BlockSpec((1, w3), lambda bb, j: (0, 0))],
        out_specs=[pl.BlockSpec((ROW_TILE, HY_WIDTH), lambda bb, j: (bb * n_pad_tiles + j, 0))] * 3,
        out_shape=[jax.ShapeDtypeStruct((rows, HY_WIDTH), F32)] * 3,
        compiler_params=_cparams(("parallel", "arbitrary")),
    )(p_hy, p_hy, p_hy, w, b.reshape(1, w3))


def _filter_kernel(feat_ref, w1_ref, b1_ref, w2_ref, b2_ref, w3_ref, win_ref, h_ref, asum_ref):
    i = pl.program_id(0)
    hp = lax.Precision.HIGHEST
    h = jnp.sin(jnp.dot(feat_ref[...], w1_ref[...], precision=hp, preferred_element_type=F32) + b1_ref[...])
    h = jnp.sin(jnp.dot(h, w2_ref[...], precision=hp, preferred_element_type=F32) + b2_ref[...])
    h = jnp.dot(h, w3_ref[...], precision=hp, preferred_element_type=F32)
    win = win_ref[...]
    h = h * jnp.concatenate([win] * (2 * HY_ORDER), axis=-1)
    h_ref[...] = h
    rows = lax.broadcasted_iota(jnp.int32, h.shape, 0) + i * ROW_TILE
    cols = lax.broadcasted_iota(jnp.int32, h.shape, 1)
    is_bwd = (cols // HY_WIDTH) % 2 == 1
    part = jnp.sum(jnp.where(is_bwd & (rows == 0), 0.0, jnp.abs(h)), axis=0, keepdims=True)

    @pl.when(i == 0)
    def _():
        asum_ref[...] = part

    @pl.when(i > 0)
    def _():
        asum_ref[...] += part


def hyena_filter_taps(length, w1, b1, w2, b2, w3):
    t = jnp.arange(length, dtype=F32)
    t_norm = t / length
    f = jnp.linspace(1e-4, HY_BANDS - 1, HY_BANDS, dtype=F32)
    wt = 2.0 * math.pi * t_norm
    feats = jnp.concatenate([t_norm[:, None], jnp.cos(wt[:, None] * f), -jnp.sin(wt[:, None] * f)], axis=-1)
    feats = jnp.pad(feats, ((0, 0), (0, 128 - HY_EMB_DIM)))
    w1p = jnp.pad(w1, ((0, 128 - HY_EMB_DIM), (0, 0)))
    deltas = jnp.abs(jnp.linspace(HY_MIN_DECAY, HY_MAX_DECAY, HY_WIDTH, dtype=F32))
    window = jnp.exp(-t_norm[:, None] * deltas[None])
    wout = HY_ORDER * 2 * HY_WIDTH
    full = lambda shp: pl.BlockSpec(shp, lambda i: (0, 0))
    return _pcall(
        _filter_kernel,
        grid=(length // ROW_TILE,),
        in_specs=[pl.BlockSpec((ROW_TILE, 128), lambda i: (i, 0)), full((128, HY_FILT_HIDDEN)),
                  full((1, HY_FILT_HIDDEN)), full((HY_FILT_HIDDEN, HY_FILT_HIDDEN)), full((1, HY_FILT_HIDDEN)),
                  full((HY_FILT_HIDDEN, wout)), pl.BlockSpec((ROW_TILE, HY_WIDTH), lambda i: (i, 0))],
        out_specs=[pl.BlockSpec((ROW_TILE, wout), lambda i: (i, 0)), full((1, wout))],
        out_shape=[jax.ShapeDtypeStruct((length, wout), F32), jax.ShapeDtypeStruct((1, wout), F32)],
        compiler_params=_cparams(("arbitrary",)),
    )(feats, w1p, b1.reshape(1, -1), w2, b2.reshape(1, -1), w3, window)


class _Dft:
    def __init__(self, n1):
        n2 = FFT_N2
        assert n1 % 16 == 0
        self.n1, self.n = n1, n1 * n2
        self.half = n1 // 2
        self.nf = n1 // 2 + 1
        self.nfp = -(-self.nf // 16) * 16
        f1 = np.arange(self.nf)[:, None]
        ang = 2.0 * np.pi * f1 * np.arange(n1)[None, :] / n1
        s1 = np.zeros((2 * self.nfp, n1))
        s1[:self.nf] = np.cos(ang)
        s1[self.nfp:self.nfp + self.nf] = -np.sin(ang)
        self.s1_full = s1
        tw = 2.0 * np.pi * f1 * np.arange(n2)[None, :] / self.n
        self.tw_cos = np.cos(tw)[:, :, None]
        self.tw_sin = np.sin(tw)[:, :, None]
        a2 = 2.0 * np.pi * np.arange(n2)[:, None] * np.arange(n2)[None, :] / n2
        wc, ws = np.cos(a2), np.sin(a2)
        self.m_fwd = np.block([[wc, ws], [-ws, wc]])
        self.m_inv = np.block([[wc, -ws], [ws, wc]])
        wgt = np.full(self.nf, 2.0)
        wgt[0] = 1.0
        wgt[-1] = 1.0
        ango = 2.0 * np.pi * np.arange(self.half)[:, None] * np.arange(self.nf)[None, :] / n1
        self.s4_re = np.zeros((self.half, self.nfp))
        self.s4_im = np.zeros((self.half, self.nfp))
        self.s4_re[:, :self.nf] = np.cos(ango) * wgt / self.n
        self.s4_im[:, :self.nf] = -np.sin(ango) * wgt / self.n

    @staticmethod
    def const(a, dtype=BF16):
        return jnp.asarray(a, F32).astype(dtype)


def _fft_s1_kernel(m_ref, x_ref, o_ref):
    k_rows, n_sub, w = x_ref.shape
    x = x_ref[...].reshape(k_rows * n_sub, w)
    y = jnp.dot(m_ref[...], x.astype(BF16), preferred_element_type=F32)
    o_ref[...] = y.reshape(o_ref.shape)


def fft_stage1(mat, x3, *, n_batch, k_rows):
    w = x3.shape[-1]
    m = mat.shape[0]
    big = _Dft.const(np.kron(mat, np.eye(FFT_SUB)))
    return _pcall(
        _fft_s1_kernel,
        grid=(n_batch, FFT_N2 // FFT_SUB),
        in_specs=[pl.BlockSpec((m * FFT_SUB, k_rows * FFT_SUB), lambda b, j: (0, 0)),
                  pl.BlockSpec((k_rows, FFT_SUB, w), lambda b, j: (b, j, 0))],
        out_specs=pl.BlockSpec((None, m, FFT_SUB, w), lambda b, j: (b, 0, j, 0)),
        out_shape=jax.ShapeDtypeStruct((n_batch, m, FFT_N2, w), F32),
        compiler_params=_cparams(("parallel", "parallel")),
    )(big, x3)


def _twiddle(ar, ai, c, s):
    return ar * c + ai * s, ai * c - ar * s


def _fft_filter_mid_kernel(ar_ref, ai_ref, c_ref, s_ref, mf_ref, sc_ref, hb0_ref, kr_ref, ki_ref):
    br, bi = _twiddle(ar_ref[...], ai_ref[...], c_ref[...], s_ref[...])
    x = jnp.dot(mf_ref[...], jnp.concatenate([br, bi], axis=0).astype(BF16), preferred_element_type=F32)
    xr, xi = x[:FFT_N2], x[FFT_N2:]
    w = HY_WIDTH
    for o in range(HY_ORDER):
        fw = slice(2 * o * w, (2 * o + 1) * w)
        bw = slice((2 * o + 1) * w, (2 * o + 2) * w)
        oc = slice(o * w, (o + 1) * w)
        kr_ref[:, oc] = (xr[:, fw] + xr[:, bw] - hb0_ref[:, oc]) * sc_ref[:, oc]
        ki_ref[:, oc] = (xi[:, fw] - xi[:, bw]) * sc_ref[:, oc]


def fft_filter_mid(a, dft, kscale, hb0):
    cols = a.shape[-1]
    n2 = FFT_N2
    tw = pl.BlockSpec((None, n2, 1), lambda f: (f, 0, 0))
    blk = pl.BlockSpec((None, n2, cols // 2), lambda f: (f, 0, 0))
    vec = pl.BlockSpec((1, cols // 2), lambda f: (0, 0))
    return _pcall(
        _fft_filter_mid_kernel,
        grid=(dft.nf,),
        in_specs=[pl.BlockSpec((None, n2, cols), lambda f: (f, 0, 0)),
                  pl.BlockSpec((None, n2, cols), lambda f: (dft.nfp + f, 0, 0)), tw, tw,
                  pl.BlockSpec((2 * n2, 2 * n2), lambda f: (0, 0)), vec, vec],
        out_specs=[blk, blk],
        out_shape=[jax.ShapeDtypeStruct((dft.nf, n2, cols // 2), F32)] * 2,
        compiler_params=_cparams(("parallel",)),
    )(a, a, _Dft.const(dft.tw_cos, F32), _Dft.const(dft.tw_sin, F32), _Dft.const(dft.m_fwd), kscale, hb0)


def _fft_mid_kernel(ar_ref, ai_ref, c_ref, s_ref, mf_ref, mi_ref, kr_ref, ki_ref, er_ref, ei_ref, *, nf):
    for u in range(FFT_MID_F):
        f1 = pl.program_id(1) * FFT_MID_F + u

        @pl.when(f1 < nf)
        def _():
            c, s = c_ref[u], s_ref[u]
            br, bi = _twiddle(ar_ref[u], ai_ref[u], c, s)
            x = jnp.dot(mf_ref[...], jnp.concatenate([br, bi], axis=0).astype(BF16), preferred_element_type=F32)
            xr, xi = x[:FFT_N2], x[FFT_N2:]
            kr, ki = kr_ref[u], ki_ref[u]
            yr = xr * kr - xi * ki
            yi = xr * ki + xi * kr
            dd = jnp.dot(mi_ref[...], jnp.concatenate([yr, yi], axis=0).astype(BF16), preferred_element_type=F32)
            dr, di = dd[:FFT_N2], dd[FFT_N2:]
            er_ref[u] = dr * c - di * s
            ei_ref[u] = di * c + dr * s

        @pl.when(f1 >= nf)
        def _():
            er_ref[u] = jnp.zeros((FFT_N2, er_ref.shape[-1]), F32)
            ei_ref[u] = jnp.zeros((FFT_N2, ei_ref.shape[-1]), F32)


def fft_mid(a, dft, kr, ki, order):
    nb = a.shape[0]
    n2, w, g = FFT_N2, HY_WIDTH, FFT_MID_F
    nf, nfp = dft.nf, dft.nfp
    assert nfp % g == 0
    last = (nf - 1) // g
    fc = lambda f: jnp.minimum(f, last)
    tw = pl.BlockSpec((g, n2, 1), lambda b, f: (fc(f), 0, 0))
    mat = pl.BlockSpec((2 * n2, 2 * n2), lambda b, f: (0, 0))
    kf = pl.BlockSpec((g, n2, w), lambda b, f: (fc(f), 0, order))
    out = pl.BlockSpec((None, g, n2, w), lambda b, f: (b, f, 0, 0))
    return _pcall(
        functools.partial(_fft_mid_kernel, nf=nf),
        grid=(nb, nfp // g),
        in_specs=[pl.BlockSpec((None, g, n2, w), lambda b, f: (b, fc(f), 0, 0)),
                  pl.BlockSpec((None, g, n2, w), lambda b, f: (b, nfp // g + fc(f), 0, 0)),
                  tw, tw, mat, mat, kf, kf],
        out_specs=[out, out],
        out_shape=[jax.ShapeDtypeStruct((nb, nfp, n2, w), F32)] * 2,
        compiler_params=_cparams(("parallel", "arbitrary")),
    )(a, a, _Dft.const(dft.tw_cos, F32), _Dft.const(dft.tw_sin, F32), _Dft.const(dft.m_fwd),
      _Dft.const(dft.m_inv), kr, ki)


def _fft_s4_kernel(mr_ref, mi_ref, er_ref, ei_ref, gate_ref, z_ref, bias_ref, scale_ref, o_ref):
    nfp, n_sub, w = er_ref.shape
    er = er_ref[...].reshape(nfp * n_sub, w).astype(BF16)
    ei = ei_ref[...].reshape(nfp * n_sub, w).astype(BF16)
    y = (jnp.dot(mr_ref[...], er, preferred_element_type=F32)
         + jnp.dot(mi_ref[...], ei, preferred_element_type=F32)).reshape(o_ref.shape)
    o_ref[...] = gate_ref[...] * (y + bias_ref[...] * z_ref[...]) * scale_ref[...]


def fft_stage4(dft, er, ei, gate, z, bias_row, scale_row):
    nb = er.shape[0]
    m, nfp = dft.half, dft.nfp
    w = z.shape[-1]
    eye = np.eye(FFT_SUB)
    row = pl.BlockSpec((m, FFT_SUB, w), lambda b, j: (b, j, 0))
    vec = pl.BlockSpec((1, 1, w), lambda b, j: (0, 0, 0))
    mat = pl.BlockSpec((m * FFT_SUB, nfp * FFT_SUB), lambda b, j: (0, 0))
    spec = pl.BlockSpec((None, nfp, FFT_SUB, w), lambda b, j: (b, 0, j, 0))
    return _pcall(
        _fft_s4_kernel,
        grid=(nb, FFT_N2 // FFT_SUB),
        in_specs=[mat, mat, spec, spec, row, row, vec, vec],
        out_specs=row,
        out_shape=jax.ShapeDtypeStruct((nb * m, FFT_N2, w), F32),
        compiler_params=_cparams(("parallel", "parallel")),
    )(_Dft.const(np.kron(dft.s4_re, eye)), _Dft.const(np.kron(dft.s4_im, eye)), er, ei, gate, z,
      bias_row.reshape(1, 1, w), scale_row.reshape(1, 1, w))


def hyena_long_conv(z, x1, x2, filt, hbias, beta_hy, *, bsz, length, n1):
    dft = _Dft(n1)
    n2, w = FFT_N2, HY_WIDTH
    half = dft.half
    assert half >= 8 and length % n2 == 0
    taps, asum = filt
    asum = asum.reshape(HY_ORDER, 2, w)
    kscale = (1.0 / (asum[:, 0] + asum[:, 1] + NORM_EPS)).reshape(1, HY_ORDER * w)
    hb0 = taps[0].reshape(HY_ORDER, 2, w)[:, 1].reshape(1, HY_ORDER * w)
    k_taps = length // n2
    a_f = fft_stage1(dft.s1_full[:, :k_taps], taps.reshape(k_taps, n2, 2 * HY_ORDER * w), n_batch=1, k_rows=k_taps)
    kr, ki = fft_filter_mid(a_f[0], dft, kscale, hb0)
    s1 = dft.s1_full[:, :half]
    ones = jnp.ones((1, w), F32)
    shape3 = (bsz * half, n2, w)
    zc = z.reshape(shape3)
    for o, gate in enumerate((x1, x2)):
        a = fft_stage1(s1, zc, n_batch=bsz, k_rows=half)
        er, ei = fft_mid(a, dft, kr, ki, o)
        last = o == HY_ORDER - 1
        zc = fft_stage4(dft, er, ei, gate.reshape(shape3), zc, hbias[o].reshape(1, w),
                        beta_hy.reshape(1, w) if last else ones)
    return zc.reshape(bsz * half * n2, w)


def _rope_tables(n_ctx, n_lat):
    rows = n_lat // GRID_W
    row = jnp.repeat(jnp.arange(rows, dtype=F32), GRID_W)
    col = jnp.tile(jnp.arange(GRID_W, dtype=F32), rows)
    n_freq = HEAD_DIM // 4
    freqs = ROPE_THETA ** (-jnp.arange(n_freq, dtype=F32) / n_freq)
    ar = row[:, None] * freqs
    ac = col[:, None] * freqs
    cos = jnp.concatenate([jnp.cos(ar), jnp.cos(ar), jnp.cos(ac), jnp.cos(ac)], axis=-1)
    sin = jnp.concatenate([jnp.sin(ar), jnp.sin(ar), jnp.sin(ac), jnp.sin(ac)], axis=-1)
    cos = jnp.concatenate([jnp.ones((n_ctx, HEAD_DIM), F32), cos], axis=0)
    sin = jnp.concatenate([jnp.zeros((n_ctx, HEAD_DIM), F32), sin], axis=0)
    even = (np.arange(HEAD_DIM) // n_freq) % 2 == 0
    sin_a = jnp.where(even, -sin, 0.0)
    sin_b = jnp.where(even, 0.0, sin)
    tile2 = lambda a: jnp.concatenate([a, a], axis=-1)
    return tile2(cos), tile2(sin_a), tile2(sin_b)


def _attn_key_chunk(t):
    for tk in (2816, 1408, 1280, 1024, 768, 512, 256):
        if t % tk == 0:
            return tk
    raise ValueError(t)


def kernel(x, c, ctx, c_ctx, ada_w, ada_b, norm_g, w_in, w_out, mix_beta, gqa_qk_g, diff_lambda, diff_subln_g,
           ret_log_decay, ret_gn_g, hy_short_w, hy_short_b, hy_filt_w1, hy_filt_b1, hy_filt_w2, hy_filt_b2,
           hy_filt_w3, hy_bias, ffn_w_gate, ffn_w_up, ffn_w_down, moe_router, moe_w_gate, moe_w_up, moe_w_down):
    bsz, n_lat, d = x.shape
    n_ctx = ctx.shape[1]
    t = n_ctx + n_lat
    r = bsz * t
    n_t, n_c = t // ROW_TILE, n_ctx // ROW_TILE
    n_s = n_lat // ROW_TILE
    assert n_ctx % ROW_TILE == 0 and n_lat % ROW_TILE == 0 and r % 512 == 0
    tk = _attn_key_chunk(t)
    assert tk >= n_ctx

    rope_c, rope_sa, rope_sb = _rope_tables(n_ctx, n_lat)
    xs = jnp.concatenate([ctx, x], axis=1).reshape(r, d)

    cvec = jnp.stack([jnp.broadcast_to(c_ctx, c.shape), c], axis=1).reshape(2 * bsz, d)
    cvec = jax.nn.silu(cvec)
    cvec = jnp.pad(cvec, ((0, 16 - 2 * bsz), (0, 0))).astype(BF16)

    n1_lat = 2 * n_lat // FFT_N2
    ctx_pad_tiles = max(n_c, 1024 // ROW_TILE)
    n1_ctx = 2 * ctx_pad_tiles * ROW_TILE // FFT_N2

    mods = []
    for l in range(DEPTH):
        mod = mm(cvec, ada_w, tn=1536, out_dtype=F32, lead=(l,))[:2 * bsz] + ada_b[l]
        mods.append(mod.reshape(2 * bsz, 6, 1, d))

    (h,) = norm_mod(xs, norm_g[0, 0], mods[0][:, 0], mods[0][:, 1], n_t, n_c, (BF16,))
    for l in range(DEPTH):
        lambda_init = 0.8 - 0.6 * math.exp(-0.3 * l)
        sh_m, sc_m, gt_m, sh_f, sc_f, gt_f = (mods[l][:, k] for k in range(6))
        beta = mix_beta[l].reshape(1, 4 * GROUP_WIDTH)
        beta_a, beta_b, beta_r, beta_d = (beta[:, GROUP_WIDTH * k:GROUP_WIDTH * (k + 1)] for k in range(4))
        moe_layer = l % 2 == 1
        i = l // 2
        nxt_mixer = (norm_g[l + 1, 0], mods[l + 1][:, 0], mods[l + 1][:, 1], BF16) if l + 1 < DEPTH else None

        p_attn = mm(h, w_in, tn=768, out_dtype=F32, lead=(l,), n_cols=COL_ATTN)
        p_hy = mm(h, w_in, tn=768, out_dtype=F32, lead=(l,), col_off=COL_HY // 768,
                  n_cols=IN_WIDTH - COL_HY)
        aq, ak, av, dq, dk, dv, rq, rk, rv, rg = prep_heads(p_attn, rope_c, rope_sa, rope_sb, gqa_qk_g[l], bsz, n_t)

        a_out = gqa_attention(aq, ak, av, beta_a, n_t=n_t, n_c=n_c, tk=tk)

        lamf = diff_lambda[l]
        lam_full = (jnp.exp(jnp.sum(lamf[0] * lamf[1])) - jnp.exp(jnp.sum(lamf[2] * lamf[3])) + lambda_init)
        b_out = diff_attention(dq, dk, dv, lam_full.reshape(1, 1), diff_subln_g[l].reshape(1, -1), beta_b,
                               1.0 - lambda_init, n_t=n_t, n_c=n_c, tk=tk)

        o2 = retention(rq, rk, rv, ret_log_decay[l], n_t=n_t, n_c=n_c)
        r_out = ret_gate(o2, rg, ret_gn_g[l].reshape(1, -1), beta_r)

        filt_args = (hy_filt_w1[l], hy_filt_b1[l], hy_filt_w2[l], hy_filt_b2[l], hy_filt_w3[l])
        z_l, x1_l, x2_l = short_conv(p_hy, hy_short_w[l], hy_short_b[l], bsz=bsz, n_t=n_t, first_tile=n_c,
                                     n_seq_tiles=n_s, n_pad_tiles=n_s)
        d_l = hyena_long_conv(z_l, x1_l, x2_l, hyena_filter_taps(n_lat, *filt_args), hy_bias[l], beta_d[0],
                              bsz=bsz, length=n_lat, n1=n1_lat)
        z_c, x1_c, x2_c = short_conv(p_hy, hy_short_w[l], hy_short_b[l], bsz=bsz, n_t=n_t, first_tile=0,
                                     n_seq_tiles=n_c, n_pad_tiles=ctx_pad_tiles)
        d_c = hyena_long_conv(z_c, x1_c, x2_c, hyena_filter_taps(n_ctx, *filt_args), hy_bias[l], beta_d[0],
                              bsz=bsz, length=n_ctx, n1=n1_ctx)
        d_out = jnp.concatenate([d_c.reshape(bsz, -1, HY_WIDTH)[:, :n_ctx], d_l.reshape(bsz, n_lat, HY_WIDTH)],
                                axis=1).reshape(r, HY_WIDTH).astype(BF16)

        y = mm_parts([a_out, b_out, r_out, d_out], w_out, tn=512, lead=(l,))
        xs, f_in = gate_res(xs, y, norm_g[l, 1], gt_m, n_t, n_c,
                            nxt=(norm_g[l, 2], sh_f, sc_f, F32 if moe_layer else BF16))

        if not moe_layer:
            hid = swiglu_up(f_in, ffn_w_gate, ffn_w_up, tn=512, lead=(i,))
            f_out = mm(hid, ffn_w_down, tn=512, out_dtype=F32, lead=(i,), max_tile_mb=6)
            res = gate_res(xs, f_out, norm_g[l, 3], gt_f, n_t, n_c, nxt=nxt_mixer)
        else:
            top_i, top_w = router(f_in, moe_router[i].T)
            n_pad_rows = (2 * r // MOE_TM + N_EXPERTS) * MOE_TM
            row_token, row_weight, tile_expert, n_used, pos = route_tokens(top_i, top_w, n_pad_rows)
            y_sorted = moe_experts(f_in, moe_w_gate, moe_w_up, moe_w_down, i, row_token, row_weight,
                                   tile_expert, n_used)
            res = moe_combine_gate_res(xs, y_sorted, pos, norm_g[l, 3], gt_f, n_t, n_c, nxt=nxt_mixer)
        xs, h = res if nxt_mixer is not None else (res, None)

    return xs.reshape(bsz, t, d)[:, n_ctx:]
```

```python
import functools
import math

import numpy as np
import jax
import jax.numpy as jnp
from jax import lax
from jax.experimental import pallas as pl
from jax.experimental.pallas import tpu as pltpu

F32 = jnp.float32
BF16 = jnp.bfloat16

D_MODEL = 2048
DEPTH = 4
GRID_W = 64
HEAD_DIM = 64
ROPE_THETA = 10000.0
NORM_EPS = 1e-6
GROUP_WIDTH = 512
GQA_Q_HEADS = 8
GQA_KV_HEADS = 2
GQA_REP = 4
DIFF_HEADS = 4
RET_HEADS = 4
HY_WIDTH = 512
HY_ORDER = 2
HY_EMB_DIM = 33
HY_BANDS = 16
HY_FILT_HIDDEN = 64
HY_MIN_DECAY = math.log(1e-2) / 1.5
HY_MAX_DECAY = math.log(1e-2) / 0.3
N_EXPERTS = 8

COL_ATTN = 3840
COL_DIFF = 768
COL_RET = 2304
COL_RET_GATE = 3328
COL_HY = 3840
IN_WIDTH = 5376

ROW_TILE = 256
MOE_TM = 1024
MOE_TF = 256
MOE_STEPS = 4096 // MOE_TF
FFT_N2 = 128
FFT_MID_F = 8
FFT_SUB = 8
VMEM_LIMIT_MB = 48


def _cparams(sem, vmem_mb=VMEM_LIMIT_MB):
    return pltpu.CompilerParams(dimension_semantics=sem, vmem_limit_bytes=vmem_mb * 1024 * 1024)


def _pcall(kern, **kw):
    fn = kern.func if isinstance(kern, functools.partial) else kern
    return pl.pallas_call(kern, name=fn.__name__.strip("_"), **kw)


def _pick_tm(m, k, max_tile_mb):
    for tm in (1536, 1280, 1024, 768, 512, 256, 128, 16):
        if m % tm == 0 and tm * k * 2 <= max_tile_mb * 2 ** 20:
            return tm
    raise ValueError((m, k))


def _mm_kernel(a_ref, b_ref, o_ref, bb_ref):
    @pl.when(pl.program_id(1) == 0)
    def _():
        bb_ref[...] = b_ref[...].astype(BF16)

    o_ref[...] = jnp.dot(a_ref[...], bb_ref[...], preferred_element_type=F32).astype(o_ref.dtype)


def _b_spec(b, lead, k, tn, col_off):
    nlead = len(lead)
    return pl.BlockSpec((None,) * nlead + (k, tn), lambda j, i: tuple(lead) + (0, j + col_off))


def mm(a, b, *, tn, out_dtype, lead=(), col_off=0, n_cols=None, max_tile_mb=9):
    m, k = a.shape
    n = b.shape[-1] if n_cols is None else n_cols
    tm = _pick_tm(m, k, max_tile_mb)
    assert n % tn == 0 and a.dtype == BF16, (n, tn, a.dtype)
    return _pcall(
        _mm_kernel,
        grid=(n // tn, m // tm),
        in_specs=[pl.BlockSpec((tm, k), lambda j, i: (i, 0)), _b_spec(b, lead, k, tn, col_off)],
        out_specs=pl.BlockSpec((tm, tn), lambda j, i: (i, j)),
        out_shape=jax.ShapeDtypeStruct((m, n), out_dtype),
        scratch_shapes=[pltpu.VMEM((k, tn), BF16)],
        compiler_params=_cparams(("parallel", "arbitrary")),
    )(a, b)


def _mm_parts_kernel(*refs, n_parts):
    a_refs, (b_ref, o_ref, bb_ref) = refs[:n_parts], refs[n_parts:]

    @pl.when(pl.program_id(1) == 0)
    def _():
        bb_ref[...] = b_ref[...].astype(BF16)

    kp = a_refs[0].shape[1]
    acc = jnp.dot(a_refs[0][...], bb_ref[0:kp, :], preferred_element_type=F32)
    for g in range(1, n_parts):
        acc += jnp.dot(a_refs[g][...], bb_ref[g * kp:(g + 1) * kp, :], preferred_element_type=F32)
    o_ref[...] = acc


def mm_parts(parts, b, *, tn, lead):
    m, kp = parts[0].shape
    k = kp * len(parts)
    n = b.shape[-1]
    tm = _pick_tm(m, k, 9)
    row = pl.BlockSpec((tm, kp), lambda j, i: (i, 0))
    return _pcall(
        functools.partial(_mm_parts_kernel, n_parts=len(parts)),
        grid=(n // tn, m // tm),
        in_specs=[row] * len(parts) + [_b_spec(b, lead, k, tn, 0)],
        out_specs=pl.BlockSpec((tm, tn), lambda j, i: (i, j)),
        out_shape=jax.ShapeDtypeStruct((m, n), F32),
        scratch_shapes=[pltpu.VMEM((k, tn), BF16)],
        compiler_params=_cparams(("parallel", "arbitrary")),
    )(*parts, b)


def _swiglu_up_kernel(a_ref, wg_ref, wu_ref, o_ref, bg_ref, bu_ref):
    @pl.when(pl.program_id(1) == 0)
    def _():
        bg_ref[...] = wg_ref[...].astype(BF16)
        bu_ref[...] = wu_ref[...].astype(BF16)

    a = a_ref[...]
    g = jnp.dot(a, bg_ref[...], preferred_element_type=F32)
    u = jnp.dot(a, bu_ref[...], preferred_element_type=F32)
    o_ref[...] = (g * jax.nn.sigmoid(g) * u).astype(o_ref.dtype)


def swiglu_up(a, wg, wu, *, tn, lead):
    m, k = a.shape
    n = wg.shape[-1]
    tm = _pick_tm(m, k, 7)
    assert n % tn == 0
    return _pcall(
        _swiglu_up_kernel,
        grid=(n // tn, m // tm),
        in_specs=[pl.BlockSpec((tm, k), lambda j, i: (i, 0)),
                  _b_spec(wg, lead, k, tn, 0), _b_spec(wu, lead, k, tn, 0)],
        out_specs=pl.BlockSpec((tm, tn), lambda j, i: (i, j)),
        out_shape=jax.ShapeDtypeStruct((m, n), BF16),
        scratch_shapes=[pltpu.VMEM((k, tn), BF16)] * 2,
        compiler_params=_cparams(("parallel", "arbitrary")),
    )(a, wg, wu)


def _seg_map(n_t, n_c):
    return lambda i: (2 * (i // n_t) + jnp.where(i % n_t >= n_c, 1, 0), 0, 0)


def _rms(x):
    return x * lax.rsqrt(jnp.mean(x * x, axis=-1, keepdims=True) + NORM_EPS)


def _norm_mod_kernel(x_ref, g_ref, sh_ref, sc_ref, *o_refs):
    y = _rms(x_ref[...]) * g_ref[...]
    y = y * (1.0 + sc_ref[...]) + sh_ref[...]
    for o_ref in o_refs:
        o_ref[...] = y.astype(o_ref.dtype)


def norm_mod(x, g, shift, scale, n_t, n_c, out_dtypes):
    r, d = x.shape
    seg = _seg_map(n_t, n_c)
    row = pl.BlockSpec((ROW_TILE, d), lambda i: (i, 0))
    outs = _pcall(
        _norm_mod_kernel,
        grid=(r // ROW_TILE,),
        in_specs=[row, pl.BlockSpec((1, d), lambda i: (0, 0)),
                  pl.BlockSpec((None, 1, d), seg), pl.BlockSpec((None, 1, d), seg)],
        out_specs=[row] * len(out_dtypes),
        out_shape=[jax.ShapeDtypeStruct((r, d), dt) for dt in out_dtypes],
        compiler_params=_cparams(("parallel",)),
    )(x, g.reshape(1, d), shift, scale)
    return outs


def _gate_res_kernel(x_ref, y_ref, g_ref, gt_ref, *rest):
    x = x_ref[...] + gt_ref[...] * (_rms(y_ref[...]) * g_ref[...])
    if len(rest) == 1:
        rest[0][...] = x
    else:
        g2_ref, sh_ref, sc_ref, o_ref, h_ref = rest
        o_ref[...] = x
        h_ref[...] = (_rms(x) * g2_ref[...] * (1.0 + sc_ref[...]) + sh_ref[...]).astype(h_ref.dtype)


def gate_res(x, y, g, gate, n_t, n_c, nxt=None):
    r, d = x.shape
    seg = _seg_map(n_t, n_c)
    row = pl.BlockSpec((ROW_TILE, d), lambda i: (i, 0))
    vec = pl.BlockSpec((1, d), lambda i: (0, 0))
    mod = pl.BlockSpec((None, 1, d), seg)
    in_specs, args = [row, row, vec, mod], [x, y, g.reshape(1, d), gate]
    out_specs, out_shape = row, jax.ShapeDtypeStruct((r, d), F32)
    if nxt is not None:
        g2, shift, scale, dtype = nxt
        in_specs, args = in_specs + [vec, mod, mod], args + [g2.reshape(1, d), shift, scale]
        out_specs, out_shape = [row, row], [out_shape, jax.ShapeDtypeStruct((r, d), dtype)]
    return _pcall(
        _gate_res_kernel,
        grid=(r // ROW_TILE,),
        in_specs=in_specs,
        out_specs=out_specs,
        out_shape=out_shape,
        compiler_params=_cparams(("parallel",)),
    )(*args)


def _router_kernel(f_ref, rt_ref, idx_ref, w_ref):
    lt = lax.dot_general(rt_ref[...], f_ref[...], (((1,), (1,)), ((), ())),
                         precision=lax.Precision.HIGHEST, preferred_element_type=F32)
    e_idx = lax.broadcasted_iota(jnp.int32, lt.shape, 0)
    m1 = jnp.max(lt, axis=0, keepdims=True)
    i1 = jnp.min(jnp.where(lt == m1, e_idx, N_EXPERTS), axis=0, keepdims=True)
    rest = jnp.where(e_idx == i1, -jnp.inf, lt)
    m2 = jnp.max(rest, axis=0, keepdims=True)
    i2 = jnp.min(jnp.where(rest == m2, e_idx, N_EXPERTS), axis=0, keepdims=True)
    e2 = jnp.exp(m2 - m1)
    idx_ref[...] = jnp.concatenate([i1, i2], axis=0)
    w_ref[...] = jnp.concatenate([1.0 / (1.0 + e2), e2 / (1.0 + e2)], axis=0)


def router(f, router_t):
    r, d = f.shape
    return _pcall(
        _router_kernel,
        grid=(r // ROW_TILE,),
        in_specs=[pl.BlockSpec((ROW_TILE, d), lambda i: (i, 0)),
                  pl.BlockSpec((N_EXPERTS, d), lambda i: (0, 0))],
        out_specs=[pl.BlockSpec((2, ROW_TILE), lambda i: (0, i))] * 2,
        out_shape=[jax.ShapeDtypeStruct((2, r), jnp.int32), jax.ShapeDtypeStruct((2, r), F32)],
        compiler_params=_cparams(("parallel",)),
    )(f, router_t)


def _row_copy(src_hbm, dst, src_row, dst_row, sem):
    return pltpu.make_async_copy(src_hbm.at[pl.ds(src_row, 1), :], dst.at[pl.ds(dst_row, 1), :], sem)


def _start_rows(src_hbm, dst, idx_ref, base, n_rows, sem):
    def start(r, carry):
        _row_copy(src_hbm, dst, idx_ref[base + r], r, sem).start()
        return carry

    lax.fori_loop(0, n_rows, start, 0, unroll=8)


def _wait_rows(src_hbm, dst, idx_ref, base, n_rows, sem):
    def wait(r, carry):
        _row_copy(src_hbm, dst, idx_ref[base + r], r, sem).wait()
        return carry

    lax.fori_loop(0, n_rows, wait, 0, unroll=8)


def _gather_rows(src_hbm, dst, idx_ref, base, n_rows, sem):
    _start_rows(src_hbm, dst, idx_ref, base, n_rows, sem)
    _wait_rows(src_hbm, dst, idx_ref, base, n_rows, sem)


def route_tokens(idx, w, n_pad_rows):
    r = idx.shape[1]
    n_assign = 2 * r
    n_tiles = n_pad_rows // MOE_TM
    e_flat = idx.reshape(n_assign)
    w_flat = w.reshape(n_assign)
    counts = jnp.sum((e_flat[:, None] == jnp.arange(N_EXPERTS)[None, :]).astype(jnp.int32), axis=0)
    padded = (counts + MOE_TM - 1) // MOE_TM * MOE_TM
    group_end = jnp.cumsum(padded)
    group_start = group_end - padded
    sorted_start = jnp.cumsum(counts) - counts
    order = jnp.argsort(e_flat, stable=True).astype(jnp.int32)
    inv = jnp.argsort(order).astype(jnp.int32)
    pos = group_start[e_flat] + (inv - sorted_start[e_flat])
    n_used = group_end[-1] // MOE_TM
    tile_start = jnp.arange(n_tiles, dtype=jnp.int32) * MOE_TM
    tile_start = jnp.minimum(tile_start, group_end[-1] - 1)
    tile_expert = jnp.sum((tile_start[:, None] >= group_end[None, :]).astype(jnp.int32), axis=1)
    rows = jnp.arange(n_pad_rows, dtype=jnp.int32)
    row_e = jnp.repeat(tile_expert, MOE_TM)
    rank = rows - group_start[row_e]
    valid = rank < counts[row_e]
    src = order[jnp.clip(sorted_start[row_e] + rank, 0, n_assign - 1)]
    row_token = jnp.where(valid, src % r, 0)
    row_weight = jnp.where(valid, w_flat[src], 0.0)
    return (row_token, row_weight.reshape(n_pad_rows, 1), tile_expert.astype(jnp.int32),
            n_used.astype(jnp.int32).reshape(1), pos)


def _moe_expert_kernel(te_ref, nu_ref, tok_ref, f_hbm, wg_ref, wu_ref, wd_ref, rw_ref, o_ref, land_ref, xb_ref, sem):
    i, f = pl.program_id(0), pl.program_id(1)
    nf = pl.num_programs(1)
    n_used = nu_ref[0]
    used = i < n_used
    rows_per_step = MOE_TM // MOE_STEPS
    nxt = jnp.minimum(i + 1, n_used - 1)

    def row_slice(tile, step):
        base = tile * MOE_TM + step * rows_per_step
        return f_hbm, land_ref.at[pl.ds(step * rows_per_step, rows_per_step)], tok_ref, base, rows_per_step, sem

    @pl.when(jnp.logical_and(used, f == 0))
    def _():
        @pl.when(i == 0)
        def _():
            _start_rows(f_hbm, land_ref, tok_ref, 0, MOE_TM, sem)

        _wait_rows(f_hbm, land_ref, tok_ref, i * MOE_TM, MOE_TM, sem)
        xb_ref[...] = land_ref[...].astype(BF16)

    @pl.when(f == 0)
    def _():
        o_ref[...] = jnp.zeros_like(o_ref)

    @pl.when(used)
    def _():
        src, dst, idx_ref, base, n, _ = row_slice(nxt, f)
        for r in range(n):
            _row_copy(src, dst, idx_ref[base + r], r, sem).start()
        x = xb_ref[...]
        g = jnp.dot(x, wg_ref[...].astype(BF16), preferred_element_type=F32)
        u = jnp.dot(x, wu_ref[...].astype(BF16), preferred_element_type=F32)
        h = (g * jax.nn.sigmoid(g) * u).astype(BF16)
        o_ref[...] += jnp.dot(h, wd_ref[...].astype(BF16), preferred_element_type=F32)

    @pl.when(jnp.logical_and(used, f == nf - 1))
    def _():
        o_ref[...] = o_ref[...] * rw_ref[...]

    @pl.when(jnp.logical_and(i == n_used - 1, f == nf - 1))
    def _():
        _wait_rows(f_hbm, land_ref, tok_ref, i * MOE_TM, MOE_TM, sem)


def moe_experts(f_in, wg, wu, wd, layer, row_token, row_weight, tile_expert, n_used):
    r, d = f_in.shape
    ff = wg.shape[-1]
    n_pad_rows = row_token.shape[0]
    nt, nf = n_pad_rows // MOE_TM, ff // MOE_TF
    assert nf == MOE_STEPS and MOE_TM % MOE_STEPS == 0

    def fi(i, f, nu):
        return jnp.where(i < nu[0], f, nf - 1)

    grid_spec = pltpu.PrefetchScalarGridSpec(
        num_scalar_prefetch=3,
        grid=(nt, nf),
        in_specs=[pl.BlockSpec(memory_space=pl.ANY),
                  pl.BlockSpec((None, None, d, MOE_TF), lambda i, f, te, nu, tok: (layer, te[i], 0, fi(i, f, nu))),
                  pl.BlockSpec((None, None, d, MOE_TF), lambda i, f, te, nu, tok: (layer, te[i], 0, fi(i, f, nu))),
                  pl.BlockSpec((None, None, MOE_TF, d), lambda i, f, te, nu, tok: (layer, te[i], fi(i, f, nu), 0)),
                  pl.BlockSpec((MOE_TM, 1), lambda i, f, te, nu, tok: (i, 0))],
        out_specs=pl.BlockSpec((MOE_TM, d), lambda i, f, te, nu, tok: (i, 0)),
        scratch_shapes=[pltpu.VMEM((MOE_TM, d), F32), pltpu.VMEM((MOE_TM, d), BF16), pltpu.SemaphoreType.DMA(())],
    )
    return _pcall(
        _moe_expert_kernel,
        grid_spec=grid_spec,
        out_shape=jax.ShapeDtypeStruct((n_pad_rows, d), F32),
        compiler_params=_cparams(("arbitrary", "arbitrary")),
    )(tile_expert, n_used, row_token, f_in, wg, wu, wd, row_weight)


def _moe_combine_kernel(pos_ref, y_hbm, x_ref, g_ref, gt_ref, *rest):
    buf_ref, sem = rest[-2:]
    i = pl.program_id(0)

    def rows(fn, tile):
        for k in range(2):
            fn(y_hbm, buf_ref.at[tile % 2, k], pos_ref, (2 * tile + k) * ROW_TILE, ROW_TILE, sem)

    @pl.when(i == 0)
    def _():
        rows(_start_rows, i)

    rows(_wait_rows, i)

    @pl.when(i + 1 < pl.num_programs(0))
    def _():
        rows(_start_rows, i + 1)

    y = buf_ref[i % 2, 0] + buf_ref[i % 2, 1]
    x = x_ref[...] + gt_ref[...] * (_rms(y) * g_ref[...])
    if len(rest) == 3:
        rest[0][...] = x
    else:
        g2_ref, sh_ref, sc_ref, o_ref, h_ref = rest[:5]
        o_ref[...] = x
        h_ref[...] = (_rms(x) * g2_ref[...] * (1.0 + sc_ref[...]) + sh_ref[...]).astype(h_ref.dtype)


def moe_combine_gate_res(x, y_sorted, pos, g, gate, n_t, n_c, nxt=None):
    r, d = x.shape
    nrt = r // ROW_TILE
    pos_tiles = pos.reshape(2, nrt, ROW_TILE).transpose(1, 0, 2).reshape(2 * r)
    seg = _seg_map(n_t, n_c)
    row = pl.BlockSpec((ROW_TILE, d), lambda i, p: (i, 0))
    vec = pl.BlockSpec((1, d), lambda i, p: (0, 0))
    mod = pl.BlockSpec((None, 1, d), lambda i, p: seg(i))
    in_specs, args = [pl.BlockSpec(memory_space=pl.ANY), row, vec, mod], [y_sorted, x, g.reshape(1, d), gate]
    out_specs, out_shape = row, jax.ShapeDtypeStruct((r, d), F32)
    if nxt is not None:
        g2, shift, scale, dtype = nxt
        in_specs, args = in_specs + [vec, mod, mod], args + [g2.reshape(1, d), shift, scale]
        out_specs, out_shape = [row, row], [out_shape, jax.ShapeDtypeStruct((r, d), dtype)]
    grid_spec = pltpu.PrefetchScalarGridSpec(
        num_scalar_prefetch=1,
        grid=(nrt,),
        in_specs=in_specs,
        out_specs=out_specs,
        scratch_shapes=[pltpu.VMEM((2, 2, ROW_TILE, d), F32), pltpu.SemaphoreType.DMA(())],
    )
    return _pcall(
        _moe_combine_kernel,
        grid_spec=grid_spec,
        out_shape=out_shape,
        compiler_params=_cparams(("arbitrary",)),
    )(pos_tiles, *args)


def _rope(x, c, sa, sb):
    return x * c + pltpu.roll(x, 112, 1) * sa + pltpu.roll(x, 16, 1) * sb


def _head_rms(x, g, bd):
    sq = x * x
    hi = sq.astype(BF16)
    lo = (sq - hi.astype(F32)).astype(BF16)
    ms = jnp.dot(hi, bd, preferred_element_type=F32) + jnp.dot(lo, bd, preferred_element_type=F32)
    return x * lax.rsqrt(ms + NORM_EPS) * g


def _prep_kernel(p_ref, c_ref, sa_ref, sb_ref, gq_ref, gk_ref, bd_ref,
                 aq_ref, ak_ref, av_ref, dq_ref, dk_ref, dv_ref, rq_ref, rk_ref, rv_ref, rg_ref):
    c, sa, sb = c_ref[...], sa_ref[...], sb_ref[...]
    bd = bd_ref[...]
    scale = HEAD_DIM ** -0.5
    qscale = scale * math.log2(math.e)

    def chunk(j):
        return p_ref[:, 128 * j:128 * (j + 1)]

    def put_heads(ref, first, val):
        ref[first] = val[:, :HEAD_DIM].astype(ref.dtype)
        ref[first + 1] = val[:, HEAD_DIM:].astype(ref.dtype)

    for j in range(4):
        put_heads(aq_ref, 2 * j, _rope(_head_rms(chunk(j), gq_ref[...], bd), c, sa, sb) * qscale)
    put_heads(ak_ref, 0, _rope(_head_rms(chunk(4), gk_ref[...], bd), c, sa, sb))
    one64 = (lax.broadcasted_iota(jnp.int32, (ROW_TILE, HEAD_DIM), 1) == 0).astype(av_ref.dtype)
    one128 = (lax.broadcasted_iota(jnp.int32, (ROW_TILE, 2 * HEAD_DIM), 1) == 0).astype(dv_ref.dtype)
    v_gqa = chunk(5)
    for h in range(GQA_KV_HEADS):
        av_ref[h, :, :HEAD_DIM] = v_gqa[:, HEAD_DIM * h:HEAD_DIM * (h + 1)].astype(av_ref.dtype)
        av_ref[h, :, HEAD_DIM:] = one64
    for j in range(4):
        put_heads(dq_ref, 2 * j, _rope(chunk(6 + j), c, sa, sb) * qscale)
        put_heads(dk_ref, 2 * j, _rope(chunk(10 + j), c, sa, sb))
        dv_ref[j, :, :2 * HEAD_DIM] = chunk(14 + j).astype(dv_ref.dtype)
        dv_ref[j, :, 2 * HEAD_DIM:] = one128
    for j in range(2):
        put_heads(rq_ref, 2 * j, _rope(chunk(18 + j), c, sa, sb))
        put_heads(rk_ref, 2 * j, _rope(chunk(20 + j), c, sa, sb) * scale)
    for j in range(4):
        rv_ref[j] = chunk(22 + j).astype(rv_ref.dtype)
    rg_ref[...] = p_ref[:, COL_RET_GATE:COL_ATTN]


def prep_heads(p_attn, rope_c, rope_sa, rope_sb, qk_g, bsz, n_t):
    t = n_t * ROW_TILE
    bd = jnp.asarray(np.kron(np.eye(2), np.full((HEAD_DIM, HEAD_DIM), 1.0 / HEAD_DIM)), F32).astype(BF16)
    gq = jnp.tile(qk_g[0], 2).reshape(1, 128)
    gk = jnp.tile(qk_g[1], 2).reshape(1, 128)
    tab = pl.BlockSpec((ROW_TILE, 128), lambda i: (i % n_t, 0))
    vec = pl.BlockSpec((1, 128), lambda i: (0, 0))

    def heads(nh, dh):
        return (pl.BlockSpec((None, nh, ROW_TILE, dh), lambda i: (i // n_t, 0, i % n_t, 0)),
                jax.ShapeDtypeStruct((bsz, nh, t, dh), BF16))

    outs = [heads(8, 64), heads(2, 64), heads(2, 128), heads(8, 64), heads(8, 64), heads(4, 256),
            heads(4, 64), heads(4, 64), heads(4, 128)]
    return _pcall(
        _prep_kernel,
        grid=(bsz * n_t,),
        in_specs=[pl.BlockSpec((ROW_TILE, COL_ATTN), lambda i: (i, 0)), tab, tab, tab, vec, vec,
                  pl.BlockSpec((128, 128), lambda i: (0, 0))],
        out_specs=[o[0] for o in outs] + [pl.BlockSpec((ROW_TILE, GROUP_WIDTH), lambda i: (i, 0))],
        out_shape=[o[1] for o in outs] + [jax.ShapeDtypeStruct((bsz * t, GROUP_WIDTH), F32)],
        compiler_params=_cparams(("parallel",)),
    )(p_attn, rope_c, rope_sa, rope_sb, gq, gk, bd)


def _nt_dot(a, b):
    return lax.dot_general(a, b, (((1,), (1,)), ((), ())), preferred_element_type=F32)


def _softmax_chunks(score_fn, v_ref, s_ref, rows, dv, n_chunks, tk, n_ctx_keys, is_ctx):
    def update(s, start, carry):
        m, acc = carry
        m_new = jnp.maximum(m, jnp.max(s, axis=-1, keepdims=True))
        p = jnp.exp2(s - m_new)
        acc = jnp.exp2(m - m_new) * acc + jnp.dot(p.astype(BF16), v_ref[pl.ds(start, tk), :],
                                                  preferred_element_type=F32)
        return m_new, acc

    init = (jnp.full((rows, 1), -1e30, F32), jnp.zeros((rows, v_ref.shape[-1]), F32))

    def ctx_tile():
        s = score_fn(0, tk)
        col = lax.broadcasted_iota(jnp.int32, s.shape, 1)
        return update(jnp.where(col < n_ctx_keys, s, -1e30), 0, init)

    def lat_tile():
        stats = init
        s_ref[0] = score_fn(0, tk)
        for j in range(n_chunks):
            if j + 1 < n_chunks:
                s_ref[(j + 1) % 2] = score_fn((j + 1) * tk, tk)
            stats = update(s_ref[j % 2], j * tk, stats)
        return stats

    _, acc = lax.cond(is_ctx, ctx_tile, lat_tile)
    return acc[:, :dv] / acc[:, dv:dv + 1]


def _gqa_kernel(q_ref, k_ref, v_ref, beta_ref, o_ref, s_ref, *, tq, tk, n_chunks, n_ctx_tiles, n_ctx_keys):
    is_ctx = pl.program_id(2) < n_ctx_tiles
    q = q_ref[...].reshape(GQA_REP * tq, HEAD_DIM)

    def scores(start, size):
        return _nt_dot(q, k_ref[pl.ds(start, size), :])

    o = _softmax_chunks(scores, v_ref, s_ref, GQA_REP * tq, HEAD_DIM, n_chunks, tk, n_ctx_keys, is_ctx)
    for r in range(GQA_REP):
        sl = slice(HEAD_DIM * r, HEAD_DIM * (r + 1))
        o_ref[:, sl] = (o[r * tq:(r + 1) * tq] * beta_ref[:, sl]).astype(o_ref.dtype)


def gqa_attention(q, k, v, beta, *, n_t, n_c, tk):
    bsz, _, t, _ = q.shape
    tq = ROW_TILE
    kern = functools.partial(_gqa_kernel, tq=tq, tk=tk, n_chunks=t // tk, n_ctx_tiles=n_c,
                             n_ctx_keys=n_c * ROW_TILE)
    kspec = pl.BlockSpec((None, None, t, HEAD_DIM), lambda b, g, i: (b, g, 0, 0))
    vspec = pl.BlockSpec((None, None, t, v.shape[-1]), lambda b, g, i: (b, g, 0, 0))
    w = GQA_REP * HEAD_DIM
    return _pcall(
        kern,
        grid=(bsz, GQA_KV_HEADS, n_t),
        in_specs=[pl.BlockSpec((None, GQA_REP, tq, HEAD_DIM), lambda b, g, i: (b, g, i, 0)), kspec, vspec,
                  pl.BlockSpec((1, w), lambda b, g, i: (0, g))],
        out_specs=pl.BlockSpec((tq, w), lambda b, g, i: (b * n_t + i, g)),
        out_shape=jax.ShapeDtypeStruct((bsz * t, GROUP_WIDTH), BF16),
        scratch_shapes=[pltpu.VMEM((2, GQA_REP * tq, tk), F32)],
        compiler_params=_cparams(("parallel", "parallel", "arbitrary")),
    )(q, k, v, beta)


def _diff_kernel(q_ref, k_ref, v_ref, lam_ref, g_ref, beta_ref, o_ref, s_ref, *, tq, tk, n_chunks, n_ctx_tiles,
                 n_ctx_keys, out_scale):
    is_ctx = pl.program_id(2) < n_ctx_tiles
    q0, q1 = q_ref[0], q_ref[1]

    def scores(start, size):
        return jnp.concatenate([_nt_dot(q0, k_ref[0, pl.ds(start, size), :]),
                                _nt_dot(q1, k_ref[1, pl.ds(start, size), :])], axis=0)

    a = _softmax_chunks(scores, v_ref, s_ref, 2 * tq, 2 * HEAD_DIM, n_chunks, tk, n_ctx_keys, is_ctx)
    o = a[:tq] - lam_ref[...] * a[tq:]
    o = _rms(o) * g_ref[...] * out_scale
    o_ref[...] = (o * beta_ref[...]).astype(o_ref.dtype)


def diff_attention(q, k, v, lam, subln_g, beta, out_scale, *, n_t, n_c, tk):
    bsz, _, t, _ = q.shape
    tq = ROW_TILE
    dv = 2 * HEAD_DIM
    kern = functools.partial(_diff_kernel, tq=tq, tk=tk, n_chunks=t // tk, n_ctx_tiles=n_c,
                             n_ctx_keys=n_c * ROW_TILE, out_scale=out_scale)
    return _pcall(
        kern,
        grid=(bsz, DIFF_HEADS, n_t),
        in_specs=[pl.BlockSpec((None, 2, tq, HEAD_DIM), lambda b, h, i: (b, h, i, 0)),
                  pl.BlockSpec((None, 2, t, HEAD_DIM), lambda b, h, i: (b, h, 0, 0)),
                  pl.BlockSpec((None, None, t, v.shape[-1]), lambda b, h, i: (b, h, 0, 0)),
                  pl.BlockSpec((1, 1), lambda b, h, i: (0, 0)),
                  pl.BlockSpec((1, dv), lambda b, h, i: (0, 0)),
                  pl.BlockSpec((1, dv), lambda b, h, i: (0, h))],
        out_specs=pl.BlockSpec((tq, dv), lambda b, h, i: (b * n_t + i, h)),
        out_shape=jax.ShapeDtypeStruct((bsz * t, GROUP_WIDTH), BF16),
        scratch_shapes=[pltpu.VMEM((2, 2 * tq, tk), F32)],
        compiler_params=_cparams(("parallel", "parallel", "arbitrary")),
    )(q, k, v, lam, subln_g, beta)


def _ret_kernel(lg_ref, q_ref, k_ref, v_ref, o_ref, state_ref, decay_ref, xi_ref, zeta_ref):
    d = pl.program_id(1)
    c = ROW_TILE
    dv = v_ref.shape[-1]

    @pl.when(pl.program_id(2) == 0)
    def _():
        state_ref[...] = jnp.zeros_like(state_ref)
        fwd = d == 0
        ii = lax.broadcasted_iota(jnp.int32, (c, c), 0)
        jj = lax.broadcasted_iota(jnp.int32, (c, c), 1)
        rel = jnp.where(fwd, ii - jj, jj - ii).astype(F32)
        pos = lax.broadcasted_iota(jnp.int32, (c, 1), 0).astype(F32)
        for h in range(RET_HEADS):
            lg = lg_ref[h]
            decay_ref[h] = jnp.where(rel >= 0, jnp.exp(jnp.maximum(rel, 0.0) * lg), 0.0)
            xi_ref[h] = jnp.exp(jnp.where(fwd, pos + 1.0, c - pos) * lg)
            zeta_ref[h] = jnp.exp(jnp.where(fwd, c - 1.0 - pos, pos) * lg)

    for h in range(RET_HEADS):
        q, k, v = q_ref[h], k_ref[h], v_ref[h]
        state = state_ref[h]
        scores = _nt_dot(q, k) * decay_ref[h]
        inner = jnp.dot(scores.astype(BF16), v, preferred_element_type=F32)
        cross = jnp.dot(q, state.astype(BF16), preferred_element_type=F32) * xi_ref[h]
        o_ref[:, dv * h:dv * (h + 1)] = inner + cross
        kz = (k.astype(F32) * zeta_ref[h]).astype(BF16)
        upd = lax.dot_general(kz, v, (((0,), (0,)), ((), ())), preferred_element_type=F32)
        state_ref[h] = jnp.exp(c * lg_ref[h]) * state + upd


def retention(q, k, v, log_decay, *, n_t, n_c):
    bsz, nh, t, dk = q.shape
    dv = v.shape[-1]

    def blk(d, j):
        back = jnp.where(j < n_c, n_c - 1 - j, n_t - 1 - (j - n_c))
        return jnp.where(d == 0, j, back)

    qk = pl.BlockSpec((None, nh, ROW_TILE, dk), lambda b, d, j: (b, 0, blk(d, j), 0))
    return _pcall(
        _ret_kernel,
        grid=(bsz, 2, n_t),
        in_specs=[pl.BlockSpec((None, nh, 1, 1), lambda b, d, j: (d, 0, 0, 0)), qk, qk,
                  pl.BlockSpec((None, nh, ROW_TILE, dv), lambda b, d, j: (b, 0, blk(d, j), 0))],
        out_specs=pl.BlockSpec((None, ROW_TILE, nh * dv), lambda b, d, j: (d, b * n_t + blk(d, j), 0)),
        out_shape=jax.ShapeDtypeStruct((2, bsz * t, nh * dv), F32),
        scratch_shapes=[pltpu.VMEM((nh, dk, dv), F32), pltpu.VMEM((nh, ROW_TILE, ROW_TILE), F32),
                        pltpu.VMEM((nh, ROW_TILE, 1), F32), pltpu.VMEM((nh, ROW_TILE, 1), F32)],
        compiler_params=_cparams(("parallel", "arbitrary", "arbitrary")),
    )(log_decay.reshape(2, RET_HEADS, 1, 1), q, k, v)


def _ret_gate_kernel(of_ref, ob_ref, g_ref, gn_ref, beta_ref, o_ref):
    dv = 2 * HEAD_DIM
    for h in range(RET_HEADS):
        sl = slice(dv * h, dv * (h + 1))
        o = of_ref[:, sl] + ob_ref[:, sl]
        mu = jnp.mean(o, axis=-1, keepdims=True)
        var = jnp.mean(jnp.square(o - mu), axis=-1, keepdims=True)
        y = (o - mu) * lax.rsqrt(var + NORM_EPS) * gn_ref[:, sl]
        g = g_ref[:, sl]
        o_ref[:, sl] = (g * jax.nn.sigmoid(g) * y * beta_ref[:, sl]).astype(o_ref.dtype)


def ret_gate(o2, gate, gn_g, beta):
    _, r, w = o2.shape
    vec = pl.BlockSpec((1, w), lambda i: (0, 0))
    row = pl.BlockSpec((ROW_TILE, w), lambda i: (i, 0))
    return _pcall(
        _ret_gate_kernel,
        grid=(r // ROW_TILE,),
        in_specs=[pl.BlockSpec((None, ROW_TILE, w), lambda i: (0, i, 0)),
                  pl.BlockSpec((None, ROW_TILE, w), lambda i: (1, i, 0)), row, vec, vec],
        out_specs=row,
        out_shape=jax.ShapeDtypeStruct((r, w), BF16),
        compiler_params=_cparams(("parallel",)),
    )(o2, o2, gate, gn_g, beta)


def _short_conv_kernel(cur_ref, prev_ref, next_ref, w_ref, b_ref, z_ref, x1_ref, x2_ref, *, n_seq_tiles):
    j = pl.program_id(1)

    @pl.when(j < n_seq_tiles)
    def _():
        u = cur_ref[...]
        rows = lax.broadcasted_iota(jnp.int32, u.shape, 0)
        prev_row = jnp.where(j == 0, 0.0, prev_ref[7:8, :])
        next_row = jnp.where(j == n_seq_tiles - 1, 0.0, next_ref[0:1, :])
        up = jnp.where(rows == 0, prev_row, pltpu.roll(u, 1, 0))
        un = jnp.where(rows == ROW_TILE - 1, next_row, pltpu.roll(u, ROW_TILE - 1, 0))
        y = up * w_ref[0:1, :] + u * w_ref[1:2, :] + un * w_ref[2:3, :] + b_ref[...]
        z_ref[...] = y[:, :HY_WIDTH]
        x1_ref[...] = y[:, HY_WIDTH:2 * HY_WIDTH]
        x2_ref[...] = y[:, 2 * HY_WIDTH:]

    @pl.when(j >= n_seq_tiles)
    def _():
        z_ref[...] = jnp.zeros_like(z_ref)
        x1_ref[...] = jnp.zeros_like(x1_ref)
        x2_ref[...] = jnp.zeros_like(x2_ref)


def short_conv(p_hy, w, b, *, bsz, n_t, first_tile, n_seq_tiles, n_pad_tiles):
    w3 = 3 * HY_WIDTH
    sub = ROW_TILE // 8
    last_blk8 = p_hy.shape[0] // 8 - 1

    def cur(bb, j):
        return (bb * n_t + first_tile + jnp.minimum(j, n_seq_tiles - 1), 0)

    def prev(bb, j):
        return (jnp.maximum(cur(bb, j)[0] * sub - 1, 0), 0)

    def nxt(bb, j):
        return (jnp.minimum((cur(bb, j)[0] + 1) * sub, last_blk8), 0)

    kern = functools.partial(_short_conv_kernel, n_seq_tiles=n_seq_tiles)
    rows = bsz * n_pad_tiles * ROW_TILE
    return _pcall(
        kern,
        grid=(bsz, n_pad_tiles),
        in_specs=[pl.BlockSpec((ROW_TILE, w3), cur), pl.BlockSpec((8, w3), prev), pl.BlockSpec((8, w3), nxt),
                  pl.BlockSpec((3, w3), lambda bb, j: (0, 0)), pl.BlockSpec((1, w3), lambda bb, j: (0, 0))],
        out_specs=[pl.BlockSpec((ROW_TILE, HY_WIDTH), lambda bb, j: (bb * n_pad_tiles + j, 0))] * 3,
        out_shape=[jax.ShapeDtypeStruct((rows, HY_WIDTH), F32)] * 3,
        compiler_params=_cparams(("parallel", "arbitrary")),
    )(p_hy, p_hy, p_hy, w, b.reshape(1, w3))


def _filter_kernel(feat_ref, w1_ref, b1_ref, w2_ref, b2_ref, w3_ref, win_ref, h_ref, asum_ref):
    i = pl.program_id(0)
    hp = lax.Precision.HIGHEST
    h = jnp.sin(jnp.dot(feat_ref[...], w1_ref[...], precision=hp, preferred_element_type=F32) + b1_ref[...])
    h = jnp.sin(jnp.dot(h, w2_ref[...], precision=hp, preferred_element_type=F32) + b2_ref[...])
    w3 = w3_ref[...]
    h_hi, w_hi = h.astype(BF16), w3.astype(BF16)
    h_lo = (h - h_hi.astype(F32)).astype(BF16)
    w_lo = (w3 - w_hi.astype(F32)).astype(BF16)
    h = (jnp.dot(h_hi, w_hi, preferred_element_type=F32) + jnp.dot(h_lo, w_hi, preferred_element_type=F32)
         + jnp.dot(h_hi, w_lo, preferred_element_type=F32))
    win = win_ref[...]
    h = h * jnp.concatenate([win] * (2 * HY_ORDER), axis=-1)
    h_ref[...] = h
    rows = lax.broadcasted_iota(jnp.int32, h.shape, 0) + i * ROW_TILE
    cols = lax.broadcasted_iota(jnp.int32, h.shape, 1)
    is_bwd = (cols // HY_WIDTH) % 2 == 1
    part = jnp.sum(jnp.where(is_bwd & (rows == 0), 0.0, jnp.abs(h)), axis=0, keepdims=True)

    @pl.when(i == 0)
    def _():
        asum_ref[...] = part

    @pl.when(i > 0)
    def _():
        asum_ref[...] += part


def hyena_filter_taps(length, w1, b1, w2, b2, w3):
    t = jnp.arange(length, dtype=F32)
    t_norm = t / length
    f = jnp.linspace(1e-4, HY_BANDS - 1, HY_BANDS, dtype=F32)
    wt = 2.0 * math.pi * t_norm
    feats = jnp.concatenate([t_norm[:, None], jnp.cos(wt[:, None] * f), -jnp.sin(wt[:, None] * f)], axis=-1)
    feats = jnp.pad(feats, ((0, 0), (0, 128 - HY_EMB_DIM)))
    w1p = jnp.pad(w1, ((0, 128 - HY_EMB_DIM), (0, 0)))
    deltas = jnp.abs(jnp.linspace(HY_MIN_DECAY, HY_MAX_DECAY, HY_WIDTH, dtype=F32))
    window = jnp.exp(-t_norm[:, None] * deltas[None])
    wout = HY_ORDER * 2 * HY_WIDTH
    full = lambda shp: pl.BlockSpec(shp, lambda i: (0, 0))
    return _pcall(
        _filter_kernel,
        grid=(length // ROW_TILE,),
        in_specs=[pl.BlockSpec((ROW_TILE, 128), lambda i: (i, 0)), full((128, HY_FILT_HIDDEN)),
                  full((1, HY_FILT_HIDDEN)), full((HY_FILT_HIDDEN, HY_FILT_HIDDEN)), full((1, HY_FILT_HIDDEN)),
                  full((HY_FILT_HIDDEN, wout)), pl.BlockSpec((ROW_TILE, HY_WIDTH), lambda i: (i, 0))],
        out_specs=[pl.BlockSpec((ROW_TILE, wout), lambda i: (i, 0)), full((1, wout))],
        out_shape=[jax.ShapeDtypeStruct((length, wout), F32), jax.ShapeDtypeStruct((1, wout), F32)],
        compiler_params=_cparams(("arbitrary",)),
    )(feats, w1p, b1.reshape(1, -1), w2, b2.reshape(1, -1), w3, window)


class _Dft:
    def __init__(self, n1):
        n2 = FFT_N2
        assert n1 % 16 == 0
        self.n1, self.n = n1, n1 * n2
        self.half = n1 // 2
        self.nf = n1 // 2 + 1
        self.nfp = -(-self.nf // 16) * 16
        f1 = np.arange(self.nf)[:, None]
        ang = 2.0 * np.pi * f1 * np.arange(n1)[None, :] / n1
        s1 = np.zeros((2 * self.nfp, n1))
        s1[:self.nf] = np.cos(ang)
        s1[self.nfp:self.nfp + self.nf] = -np.sin(ang)
        self.s1_full = s1
        tw = 2.0 * np.pi * f1 * np.arange(n2)[None, :] / self.n
        self.tw_cos = np.cos(tw)[:, :, None]
        self.tw_sin = np.sin(tw)[:, :, None]
        a2 = 2.0 * np.pi * np.arange(n2)[:, None] * np.arange(n2)[None, :] / n2
        wc, ws = np.cos(a2), np.sin(a2)
        self.m_fwd = np.block([[wc, ws], [-ws, wc]])
        self.m_inv = np.block([[wc, -ws], [ws, wc]])
        wgt = np.full(self.nf, 2.0)
        wgt[0] = 1.0
        wgt[-1] = 1.0
        ango = 2.0 * np.pi * np.arange(self.half)[:, None] * np.arange(self.nf)[None, :] / n1
        self.s4_re = np.zeros((self.half, self.nfp))
        self.s4_im = np.zeros((self.half, self.nfp))
        self.s4_re[:, :self.nf] = np.cos(ango) * wgt / self.n
        self.s4_im[:, :self.nf] = -np.sin(ango) * wgt / self.n

    @staticmethod
    def const(a, dtype=BF16):
        return jnp.asarray(a, F32).astype(dtype)


def _fft_s1_kernel(m_ref, x_ref, o_ref):
    k_rows, n_sub, w = x_ref.shape
    x = x_ref[...].reshape(k_rows * n_sub, w)
    y = jnp.dot(m_ref[...], x.astype(BF16), preferred_element_type=F32)
    o_ref[...] = y.reshape(o_ref.shape)


def fft_stage1(mat, x3, *, n_batch, k_rows):
    w = x3.shape[-1]
    m = mat.shape[0]
    big = _Dft.const(np.kron(mat, np.eye(FFT_SUB)))
    return _pcall(
        _fft_s1_kernel,
        grid=(n_batch, FFT_N2 // FFT_SUB),
        in_specs=[pl.BlockSpec((m * FFT_SUB, k_rows * FFT_SUB), lambda b, j: (0, 0)),
                  pl.BlockSpec((k_rows, FFT_SUB, w), lambda b, j: (b, j, 0))],
        out_specs=pl.BlockSpec((None, m, FFT_SUB, w), lambda b, j: (b, 0, j, 0)),
        out_shape=jax.ShapeDtypeStruct((n_batch, m, FFT_N2, w), F32),
        compiler_params=_cparams(("parallel", "parallel")),
    )(big, x3)


def _twiddle(ar, ai, c, s):
    return ar * c + ai * s, ai * c - ar * s


def _fft_filter_mid_kernel(ar_ref, ai_ref, c_ref, s_ref, mf_ref, sc_ref, hb0_ref, kr_ref, ki_ref):
    br, bi = _twiddle(ar_ref[...], ai_ref[...], c_ref[...], s_ref[...])
    x = jnp.dot(mf_ref[...], jnp.concatenate([br, bi], axis=0).astype(BF16), preferred_element_type=F32)
    xr, xi = x[:FFT_N2], x[FFT_N2:]
    w = HY_WIDTH
    for o in range(HY_ORDER):
        fw = slice(2 * o * w, (2 * o + 1) * w)
        bw = slice((2 * o + 1) * w, (2 * o + 2) * w)
        oc = slice(o * w, (o + 1) * w)
        kr_ref[:, oc] = (xr[:, fw] + xr[:, bw] - hb0_ref[:, oc]) * sc_ref[:, oc]
        ki_ref[:, oc] = (xi[:, fw] - xi[:, bw]) * sc_ref[:, oc]


def fft_filter_mid(a, dft, kscale, hb0):
    cols = a.shape[-1]
    n2 = FFT_N2
    tw = pl.BlockSpec((None, n2, 1), lambda f: (f, 0, 0))
    blk = pl.BlockSpec((None, n2, cols // 2), lambda f: (f, 0, 0))
    vec = pl.BlockSpec((1, cols // 2), lambda f: (0, 0))
    return _pcall(
        _fft_filter_mid_kernel,
        grid=(dft.nf,),
        in_specs=[pl.BlockSpec((None, n2, cols), lambda f: (f, 0, 0)),
                  pl.BlockSpec((None, n2, cols), lambda f: (dft.nfp + f, 0, 0)), tw, tw,
                  pl.BlockSpec((2 * n2, 2 * n2), lambda f: (0, 0)), vec, vec],
        out_specs=[blk, blk],
        out_shape=[jax.ShapeDtypeStruct((dft.nf, n2, cols // 2), F32)] * 2,
        compiler_params=_cparams(("parallel",)),
    )(a, a, _Dft.const(dft.tw_cos, F32), _Dft.const(dft.tw_sin, F32), _Dft.const(dft.m_fwd), kscale, hb0)


def _fft_mid_kernel(ar_ref, ai_ref, c_ref, s_ref, mf_ref, mi_ref, kr_ref, ki_ref, er_ref, ei_ref, *, nf):
    for u in range(FFT_MID_F):
        f1 = pl.program_id(1) * FFT_MID_F + u

        @pl.when(f1 < nf)
        def _():
            c, s = c_ref[u], s_ref[u]
            br, bi = _twiddle(ar_ref[u], ai_ref[u], c, s)
            x = jnp.dot(mf_ref[...], jnp.concatenate([br, bi], axis=0).astype(BF16), preferred_element_type=F32)
            xr, xi = x[:FFT_N2], x[FFT_N2:]
            kr, ki = kr_ref[u], ki_ref[u]
            yr = xr * kr - xi * ki
            yi = xr * ki + xi * kr
            dd = jnp.dot(mi_ref[...], jnp.concatenate([yr, yi], axis=0).astype(BF16), preferred_element_type=F32)
            dr, di = dd[:FFT_N2], dd[FFT_N2:]
            er_ref[u] = dr * c - di * s
            ei_ref[u] = di * c + dr * s

        @pl.when(f1 >= nf)
        def _():
            er_ref[u] = jnp.zeros((FFT_N2, er_ref.shape[-1]), F32)
            ei_ref[u] = jnp.zeros((FFT_N2, ei_ref.shape[-1]), F32)


def fft_mid(a, dft, kr, ki, order):
    nb = a.shape[0]
    n2, w, g = FFT_N2, HY_WIDTH, FFT_MID_F
    nf, nfp = dft.nf, dft.nfp
    assert nfp % g == 0
    last = (nf - 1) // g
    fc = lambda f: jnp.minimum(f, last)
    tw = pl.BlockSpec((g, n2, 1), lambda b, f: (fc(f), 0, 0))
    mat = pl.BlockSpec((2 * n2, 2 * n2), lambda b, f: (0, 0))
    kf = pl.BlockSpec((g, n2, w), lambda b, f: (fc(f), 0, order))
    out = pl.BlockSpec((None, g, n2, w), lambda b, f: (b, f, 0, 0))
    return _pcall(
        functools.partial(_fft_mid_kernel, nf=nf),
        grid=(nb, nfp // g),
        in_specs=[pl.BlockSpec((None, g, n2, w), lambda b, f: (b, fc(f), 0, 0)),
                  pl.BlockSpec((None, g, n2, w), lambda b, f: (b, nfp // g + fc(f), 0, 0)),
                  tw, tw, mat, mat, kf, kf],
        out_specs=[out, out],
        out_shape=[jax.ShapeDtypeStruct((nb, nfp, n2, w), F32)] * 2,
        compiler_params=_cparams(("parallel", "arbitrary")),
    )(a, a, _Dft.const(dft.tw_cos, F32), _Dft.const(dft.tw_sin, F32), _Dft.const(dft.m_fwd),
      _Dft.const(dft.m_inv), kr, ki)


def _fft_s4_kernel(mr_ref, mi_ref, er_ref, ei_ref, gate_ref, z_ref, bias_ref, scale_ref, o_ref):
    nfp, n_sub, w = er_ref.shape
    er = er_ref[...].reshape(nfp * n_sub, w).astype(BF16)
    ei = ei_ref[...].reshape(nfp * n_sub, w).astype(BF16)
    y = (jnp.dot(mr_ref[...], er, preferred_element_type=F32)
         + jnp.dot(mi_ref[...], ei, preferred_element_type=F32)).reshape(o_ref.shape)
    o_ref[...] = gate_ref[...] * (y + bias_ref[...] * z_ref[...]) * scale_ref[...]


def fft_stage4(dft, er, ei, gate, z, bias_row, scale_row):
    nb = er.shape[0]
    m, nfp = dft.half, dft.nfp
    w = z.shape[-1]
    eye = np.eye(FFT_SUB)
    row = pl.BlockSpec((m, FFT_SUB, w), lambda b, j: (b, j, 0))
    vec = pl.BlockSpec((1, 1, w), lambda b, j: (0, 0, 0))
    mat = pl.BlockSpec((m * FFT_SUB, nfp * FFT_SUB), lambda b, j: (0, 0))
    spec = pl.BlockSpec((None, nfp, FFT_SUB, w), lambda b, j: (b, 0, j, 0))
    return _pcall(
        _fft_s4_kernel,
        grid=(nb, FFT_N2 // FFT_SUB),
        in_specs=[mat, mat, spec, spec, row, row, vec, vec],
        out_specs=row,
        out_shape=jax.ShapeDtypeStruct((nb * m, FFT_N2, w), F32),
        compiler_params=_cparams(("parallel", "parallel")),
    )(_Dft.const(np.kron(dft.s4_re, eye)), _Dft.const(np.kron(dft.s4_im, eye)), er, ei, gate, z,
      bias_row.reshape(1, 1, w), scale_row.reshape(1, 1, w))


def hyena_long_conv(z, x1, x2, filt, hbias, beta_hy, *, bsz, length, n1):
    dft = _Dft(n1)
    n2, w = FFT_N2, HY_WIDTH
    half = dft.half
    assert half >= 8 and length % n2 == 0
    taps, asum = filt
    asum = asum.reshape(HY_ORDER, 2, w)
    kscale = (1.0 / (asum[:, 0] + asum[:, 1] + NORM_EPS)).reshape(1, HY_ORDER * w)
    hb0 = taps[0].reshape(HY_ORDER, 2, w)[:, 1].reshape(1, HY_ORDER * w)
    k_taps = length // n2
    a_f = fft_stage1(dft.s1_full[:, :k_taps], taps.reshape(k_taps, n2, 2 * HY_ORDER * w), n_batch=1, k_rows=k_taps)
    kr, ki = fft_filter_mid(a_f[0], dft, kscale, hb0)
    s1 = dft.s1_full[:, :half]
    ones = jnp.ones((1, w), F32)
    shape3 = (bsz * half, n2, w)
    zc = z.reshape(shape3)
    for o, gate in enumerate((x1, x2)):
        a = fft_stage1(s1, zc, n_batch=bsz, k_rows=half)
        er, ei = fft_mid(a, dft, kr, ki, o)
        last = o == HY_ORDER - 1
        zc = fft_stage4(dft, er, ei, gate.reshape(shape3), zc, hbias[o].reshape(1, w),
                        beta_hy.reshape(1, w) if last else ones)
    return zc.reshape(bsz * half * n2, w)


def _rope_tables(n_ctx, n_lat):
    rows = n_lat // GRID_W
    row = jnp.repeat(jnp.arange(rows, dtype=F32), GRID_W)
    col = jnp.tile(jnp.arange(GRID_W, dtype=F32), rows)
    n_freq = HEAD_DIM // 4
    freqs = ROPE_THETA ** (-jnp.arange(n_freq, dtype=F32) / n_freq)
    ar = row[:, None] * freqs
    ac = col[:, None] * freqs
    cos = jnp.concatenate([jnp.cos(ar), jnp.cos(ar), jnp.cos(ac), jnp.cos(ac)], axis=-1)
    sin = jnp.concatenate([jnp.sin(ar), jnp.sin(ar), jnp.sin(ac), jnp.sin(ac)], axis=-1)
    cos = jnp.concatenate([jnp.ones((n_ctx, HEAD_DIM), F32), cos], axis=0)
    sin = jnp.concatenate([jnp.zeros((n_ctx, HEAD_DIM), F32), sin], axis=0)
    even = (np.arange(HEAD_DIM) // n_freq) % 2 == 0
    sin_a = jnp.where(even, -sin, 0.0)
    sin_b = jnp.where(even, 0.0, sin)
    tile2 = lambda a: jnp.concatenate([a, a], axis=-1)
    return tile2(cos), tile2(sin_a), tile2(sin_b)


def _attn_key_chunk(t):
    for tk in (2816, 1408, 1280, 1024, 768, 512, 256):
        if t % tk == 0:
            return tk
    raise ValueError(t)


def kernel(x, c, ctx, c_ctx, ada_w, ada_b, norm_g, w_in, w_out, mix_beta, gqa_qk_g, diff_lambda, diff_subln_g,
           ret_log_decay, ret_gn_g, hy_short_w, hy_short_b, hy_filt_w1, hy_filt_b1, hy_filt_w2, hy_filt_b2,
           hy_filt_w3, hy_bias, ffn_w_gate, ffn_w_up, ffn_w_down, moe_router, moe_w_gate, moe_w_up, moe_w_down):
    bsz, n_lat, d = x.shape
    n_ctx = ctx.shape[1]
    t = n_ctx + n_lat
    r = bsz * t
    n_t, n_c = t // ROW_TILE, n_ctx // ROW_TILE
    n_s = n_lat // ROW_TILE
    assert n_ctx % ROW_TILE == 0 and n_lat % ROW_TILE == 0 and r % 512 == 0
    tk = _attn_key_chunk(t)
    assert tk >= n_ctx

    rope_c, rope_sa, rope_sb = _rope_tables(n_ctx, n_lat)
    xs = jnp.concatenate([ctx, x], axis=1).reshape(r, d)

    cvec = jnp.stack([jnp.broadcast_to(c_ctx, c.shape), c], axis=1).reshape(2 * bsz, d)
    cvec = jax.nn.silu(cvec)
    cvec = jnp.pad(cvec, ((0, 16 - 2 * bsz), (0, 0))).astype(BF16)

    n1_lat = 2 * n_lat // FFT_N2
    ctx_pad_tiles = max(n_c, 1024 // ROW_TILE)
    n1_ctx = 2 * ctx_pad_tiles * ROW_TILE // FFT_N2

    mods = []
    for l in range(DEPTH):
        mod = mm(cvec, ada_w, tn=1536, out_dtype=F32, lead=(l,))[:2 * bsz] + ada_b[l]
        mods.append(mod.reshape(2 * bsz, 6, 1, d))

    (h,) = norm_mod(xs, norm_g[0, 0], mods[0][:, 0], mods[0][:, 1], n_t, n_c, (BF16,))
    for l in range(DEPTH):
        lambda_init = 0.8 - 0.6 * math.exp(-0.3 * l)
        sh_m, sc_m, gt_m, sh_f, sc_f, gt_f = (mods[l][:, k] for k in range(6))
        beta = mix_beta[l].reshape(1, 4 * GROUP_WIDTH)
        beta_a, beta_b, beta_r, beta_d = (beta[:, GROUP_WIDTH * k:GROUP_WIDTH * (k + 1)] for k in range(4))
        moe_layer = l % 2 == 1
        i = l // 2
        nxt_mixer = (norm_g[l + 1, 0], mods[l + 1][:, 0], mods[l + 1][:, 1], BF16) if l + 1 < DEPTH else None

        p_attn = mm(h, w_in, tn=768, out_dtype=F32, lead=(l,), n_cols=COL_ATTN)
        p_hy = mm(h, w_in, tn=768, out_dtype=F32, lead=(l,), col_off=COL_HY // 768,
                  n_cols=IN_WIDTH - COL_HY)
        aq, ak, av, dq, dk, dv, rq, rk, rv, rg = prep_heads(p_attn, rope_c, rope_sa, rope_sb, gqa_qk_g[l], bsz, n_t)

        a_out = gqa_attention(aq, ak, av, beta_a, n_t=n_t, n_c=n_c, tk=tk)

        lamf = diff_lambda[l]
        lam_full = (jnp.exp(jnp.sum(lamf[0] * lamf[1])) - jnp.exp(jnp.sum(lamf[2] * lamf[3])) + lambda_init)
        b_out = diff_attention(dq, dk, dv, lam_full.reshape(1, 1), diff_subln_g[l].reshape(1, -1), beta_b,
                               1.0 - lambda_init, n_t=n_t, n_c=n_c, tk=tk)

        o2 = retention(rq, rk, rv, ret_log_decay[l], n_t=n_t, n_c=n_c)
        r_out = ret_gate(o2, rg, ret_gn_g[l].reshape(1, -1), beta_r)

        filt_args = (hy_filt_w1[l], hy_filt_b1[l], hy_filt_w2[l], hy_filt_b2[l], hy_filt_w3[l])
        z_l, x1_l, x2_l = short_conv(p_hy, hy_short_w[l], hy_short_b[l], bsz=bsz, n_t=n_t, first_tile=n_c,
                                     n_seq_tiles=n_s, n_pad_tiles=n_s)
        d_l = hyena_long_conv(z_l, x1_l, x2_l, hyena_filter_taps(n_lat, *filt_args), hy_bias[l], beta_d[0],
                              bsz=bsz, length=n_lat, n1=n1_lat)
        z_c, x1_c, x2_c = short_conv(p_hy, hy_short_w[l], hy_short_b[l], bsz=bsz, n_t=n_t, first_tile=0,
                                     n_seq_tiles=n_c, n_pad_tiles=ctx_pad_tiles)
        d_c = hyena_long_conv(z_c, x1_c, x2_c, hyena_filter_taps(n_ctx, *filt_args), hy_bias[l], beta_d[0],
                              bsz=bsz, length=n_ctx, n1=n1_ctx)
        d_out = jnp.concatenate([d_c.reshape(bsz, -1, HY_WIDTH)[:, :n_ctx], d_l.reshape(bsz, n_lat, HY_WIDTH)],
                                axis=1).reshape(r, HY_WIDTH).astype(BF16)

        y = mm_parts([a_out, b_out, r_out, d_out], w_out, tn=512, lead=(l,))
        xs, f_in = gate_res(xs, y, norm_g[l, 1], gt_m, n_t, n_c,
                            nxt=(norm_g[l, 2], sh_f, sc_f, F32 if moe_layer else BF16))

        if not moe_layer:
            hid = swiglu_up(f_in, ffn_w_gate, ffn_w_up, tn=512, lead=(i,))
            f_out = mm(hid, ffn_w_down, tn=512, out_dtype=F32, lead=(i,), max_tile_mb=6)
            res = gate_res(xs, f_out, norm_g[l, 3], gt_f, n_t, n_c, nxt=nxt_mixer)
        else:
            top_i, top_w = router(f_in, moe_router[i].T)
            n_pad_rows = (2 * r // MOE_TM + N_EXPERTS) * MOE_TM
            row_token, row_weight, tile_expert, n_used, pos = route_tokens(top_i, top_w, n_pad_rows)
            y_sorted = moe_experts(f_in, moe_w_gate, moe_w_up, moe_w_down, i, row_token, row_weight,
                                   tile_expert, n_used)
            res = moe_combine_gate_res(xs, y_sorted, pos, norm_g[l, 3], gt_f, n_t, n_c, nxt=nxt_mixer)
        xs, h = res if nxt_mixer is not None else (res, None)

    return xs.reshape(bsz, t, d)[:, n_ctx:]
```
